```python
import math
import jax, jax.numpy as jnp
from jax import lax
import numpy as np

D_MODEL = 1024
BATCH = 4
SEQ = 8192
DEPTH = 1

PLE_DIM = 256
D_FF = 2816
MB_HEADS = 8
MB_HEAD_DIM = 64
MB_WIDTH = MB_HEADS * MB_HEAD_DIM
MB_BLOCK = 256
MB_TOPK = 3
MB_QCHUNK = 64
HG_HEADS = 4
HG_DK = 128
HG_DV = 128
HG_KEY_WIDTH = HG_HEADS * HG_DK
HG_WIDTH = HG_HEADS * HG_DV
HG_CHUNK = 64
REL_BUCKETS = 32
REL_MAX_EXACT = REL_BUCKETS // 2
REL_MAX_DIST = 128
SPLIT_SIZES = (MB_WIDTH, MB_WIDTH, MB_WIDTH, HG_KEY_WIDTH, HG_KEY_WIDTH, HG_WIDTH, HG_WIDTH, D_MODEL, D_MODEL)
IN_COLS = 3 * MB_WIDTH + 2 * HG_KEY_WIDTH + 2 * HG_WIDTH + 2 * D_MODEL
N_NORMS = 7
EPS = 1e-6

kernel_name = "moba_hgrn2_gated_hybrid_macaron"


def rmsnorm(x, w):
    xf = x.astype(jnp.float32)
    y = xf * lax.rsqrt(jnp.mean(xf * xf, axis=-1, keepdims=True) + EPS)
    return y.astype(x.dtype) * w


def swiglu(x, w_gu, w_down):
    g, u = jnp.split(x @ w_gu, 2, axis=-1)
    return (jax.nn.silu(g) * u) @ w_down


def rel_bucket(n):
    n = jnp.maximum(n, 0)
    nf = jnp.maximum(n, 1).astype(jnp.float32)
    large = REL_MAX_EXACT + (jnp.log(nf / REL_MAX_EXACT) / math.log(REL_MAX_DIST / REL_MAX_EXACT)
                             * (REL_BUCKETS - REL_MAX_EXACT)).astype(jnp.int32)
    large = jnp.minimum(large, REL_BUCKETS - 1)
    return jnp.where(n < REL_MAX_EXACT, n, large)


def moba_attention(q, k, v, rel_table):
    B, S, H, Dh = q.shape
    nb = -(-S // MB_BLOCK)
    s_pad = nb * MB_BLOCK
    pad = ((0, 0), (0, s_pad - S), (0, 0), (0, 0))
    q = jnp.pad(q, pad).transpose(0, 2, 1, 3)
    k = jnp.pad(k, pad).transpose(0, 2, 1, 3)
    v = jnp.pad(v, pad).transpose(0, 2, 1, 3)
    kb = k.reshape(B, H, nb, MB_BLOCK, Dh)
    vb = v.reshape(B, H, nb, MB_BLOCK, Dh)
    scale = Dh ** -0.5

    kmean = jnp.mean(kb.astype(jnp.float32), axis=3)
    gate = jnp.einsum('bhtd,bhnd->bhtn', q.astype(jnp.float32), kmean)
    pos = jnp.arange(s_pad)
    own = pos // MB_BLOCK
    past = jnp.arange(nb)[None, :] < own[:, None]
    gate = jnp.where(past[None, None], gate, -jnp.inf)
    kk = min(MB_TOPK, nb)
    _, top_idx = lax.top_k(gate, kk)
    valid = jnp.arange(kk)[None, :] < own[:, None]

    nq = s_pad // MB_QCHUNK
    qc = q.reshape(B, H, nq, MB_QCHUNK, Dh).transpose(2, 0, 1, 3, 4)
    idxc = top_idx.reshape(B, H, nq, MB_QCHUNK, kk).transpose(2, 0, 1, 3, 4)
    validc = valid.reshape(nq, MB_QCHUNK, kk)
    chunk_ids = jnp.arange(nq)
    bidx = jnp.arange(B)[:, None, None, None]
    hidx = jnp.arange(H)[None, :, None, None]
    hsel = jnp.arange(H)[None, :, None, None, None]
    blk_off = jnp.arange(MB_BLOCK)

    def step(args):
        qi, ii, vi, c = args
        qpos = c * MB_QCHUNK + jnp.arange(MB_QCHUNK)
        blk = (c * MB_QCHUNK) // MB_BLOCK
        k_own = lax.dynamic_index_in_dim(kb, blk, axis=2, keepdims=False)
        v_own = lax.dynamic_index_in_dim(vb, blk, axis=2, keepdims=False)
        dist_own = qpos[:, None] - (blk * MB_BLOCK + blk_off)[None, :]
        bias_own = jnp.moveaxis(rel_table[rel_bucket(dist_own)], -1, 0)[None]
        s_own = jnp.einsum('bhqd,bhkd->bhqk', qi, k_own).astype(jnp.float32) * scale + bias_own
        s_own = jnp.where((dist_own >= 0)[None, None], s_own, -jnp.inf)
        k_sel = kb[bidx, hidx, ii]
        v_sel = vb[bidx, hidx, ii]
        kpos_sel = ii[..., None] * MB_BLOCK + blk_off
        dist_sel = qpos[None, None, :, None, None] - kpos_sel
        bias_sel = rel_table[rel_bucket(dist_sel), hsel]
        s_sel = jnp.einsum('bhqd,bhqjkd->bhqjk', qi, k_sel).astype(jnp.float32) * scale + bias_sel
        s_sel = jnp.where(vi[None, None, :, :, None], s_sel, -jnp.inf)
        s = jnp.concatenate([s_own, s_sel.reshape(B, H, MB_QCHUNK, kk * MB_BLOCK)], axis=-1)
        w = jax.nn.softmax(s, axis=-1)
        w_own = w[..., :MB_BLOCK].astype(v.dtype)
        w_sel = w[..., MB_BLOCK:].reshape(B, H, MB_QCHUNK, kk, MB_BLOCK).astype(v.dtype)
        return (jnp.einsum('bhqk,bhkd->bhqd', w_own, v_own)
                + jnp.einsum('bhqjk,bhqjkd->bhqd', w_sel, v_sel))

    out = lax.map(step, (qc, idxc, validc, chunk_ids))
    out = out.transpose(1, 0, 3, 2, 4).reshape(B, s_pad, H, Dh)[:, :S]
    return out.reshape(B, S, H * Dh)


def hgrn2_scan(q, k, v, logf):
    B, S, H, DK = q.shape
    DV = v.shape[-1]
    n = S // HG_CHUNK

    def to_chunks(a):
        return a.astype(jnp.float32).reshape(B, n, HG_CHUNK, H, a.shape[-1]).transpose(1, 0, 3, 2, 4)

    qc, kc, vc, fc = to_chunks(q), to_chunks(k), to_chunks(v), to_chunks(logf)
    causal = jnp.tril(jnp.ones((HG_CHUNK, HG_CHUNK), dtype=bool))

    def step(state, xs):
        qi, ki, vi, fi = xs
        b = jnp.cumsum(fi, axis=2)
        o_inter = jnp.einsum('bhck,bhkv->bhcv', qi * jnp.exp(b), state)
        diff = b[:, :, :, None, :] - b[:, :, None, :, :]
        decay = jnp.exp(jnp.where(causal[None, None, :, :, None], diff, -jnp.inf))
        attn = jnp.einsum('bhtk,bhsk,bhtsk->bhts', qi, ki, decay)
        o_intra = jnp.einsum('bhts,bhsv->bhtv', attn, vi)
        b_last = b[:, :, -1:, :]
        new_state = (state * jnp.exp(b_last[:, :, 0, :, None])
                     + jnp.einsum('bhsk,bhsv->bhkv', ki * jnp.exp(b_last - b), vi))
        return new_state, o_inter + o_intra

    state0 = jnp.zeros((B, H, DK, DV), jnp.float32)
    _, ys = lax.scan(step, state0, (qc, kc, vc, fc))
    return ys.transpose(1, 0, 3, 2, 4).reshape(B, S, H, DV).astype(q.dtype)


def setup_inputs(seed: int = 0) -> dict:
    key = jax.random.key(seed)
    ks = jax.random.split(key, 16)
    f32 = jnp.float32

    def nrm(k, shape, scale):
        return jax.random.normal(k, shape, f32) * scale

    return {
        "x": nrm(ks[0], (BATCH, SEQ, D_MODEL), 1.0),
        "p": nrm(ks[1], (DEPTH, BATCH, SEQ, PLE_DIM), 1.0),
        "w_ffn1_gu": nrm(ks[2], (DEPTH, D_MODEL, 2 * D_FF), D_MODEL ** -0.5),
        "w_ffn1_down": nrm(ks[3], (DEPTH, D_FF, D_MODEL), D_FF ** -0.5),
        "w_in": nrm(ks[4], (DEPTH, D_MODEL, IN_COLS), D_MODEL ** -0.5),
        "w_branch_a": nrm(ks[5], (DEPTH, MB_WIDTH, D_MODEL), MB_WIDTH ** -0.5),
        "w_branch_b": nrm(ks[6], (DEPTH, HG_WIDTH, D_MODEL), HG_WIDTH ** -0.5),
        "w_out": nrm(ks[7], (DEPTH, D_MODEL, D_MODEL), D_MODEL ** -0.5),
        "w_ffn2_gu": nrm(ks[8], (DEPTH, D_MODEL, 2 * D_FF), D_MODEL ** -0.5),
        "w_ffn2_down": nrm(ks[9], (DEPTH, D_FF, D_MODEL), D_FF ** -0.5),
        "w_ple": nrm(ks[10], (DEPTH, PLE_DIM, D_MODEL), PLE_DIM ** -0.5),
        "w_ple_gate": nrm(ks[11], (DEPTH, D_MODEL, D_MODEL), D_MODEL ** -0.5),
        "norm_gains": 1.0 + nrm(ks[12], (DEPTH, N_NORMS, D_MODEL), 0.1),
        "hg_norm_w": 1.0 + nrm(ks[13], (DEPTH, HG_DV), 0.1),
        "lb_param": nrm(ks[14], (DEPTH + 1, HG_KEY_WIDTH), 0.5),
        "rel_table": nrm(ks[15], (REL_BUCKETS, MB_HEADS), 0.5),
    }


def reference(x, p, w_ffn1_gu, w_ffn1_down, w_in, w_branch_a, w_branch_b, w_out,
              w_ffn2_gu, w_ffn2_down, w_ple, w_ple_gate, norm_gains, hg_norm_w,
              lb_param, rel_table):
    B, S, _ = x.shape
    offsets = np.cumsum(SPLIT_SIZES)[:-1].tolist()
    lbs = jnp.cumsum(jax.nn.softmax(lb_param.astype(jnp.float32), axis=0), axis=0)
    h = x
    for i in range(DEPTH):
        g = norm_gains[i]
        h = h + 0.5 * rmsnorm(swiglu(rmsnorm(h, g[0]), w_ffn1_gu[i], w_ffn1_down[i]), g[1])

        u = rmsnorm(h, g[2])
        z = u @ w_in[i]
        mq, mk, mv, hq, hf, hi, hg, ga, gb = jnp.split(z, offsets, axis=-1)

        o_a = moba_attention(mq.reshape(B, S, MB_HEADS, MB_HEAD_DIM),
                             mk.reshape(B, S, MB_HEADS, MB_HEAD_DIM),
                             mv.reshape(B, S, MB_HEADS, MB_HEAD_DIM), rel_table)

        lb = lbs[i].reshape(HG_HEADS, HG_DK)
        f = lb + (1.0 - lb) * jax.nn.sigmoid(hf.astype(jnp.float32).reshape(B, S, HG_HEADS, HG_DK))
        logf = jnp.log(f)
        k_in = (1.0 - f).astype(x.dtype)
        o_b = hgrn2_scan(jax.nn.silu(hq).reshape(B, S, HG_HEADS, HG_DK), k_in,
                         hi.reshape(B, S, HG_HEADS, HG_DV), logf)
        o_b = rmsnorm(o_b, hg_norm_w[i]).reshape(B, S, HG_WIDTH) * jax.nn.silu(hg)

        merged = (jax.nn.sigmoid(ga) * (o_a @ w_branch_a[i])
                  + jax.nn.sigmoid(gb) * (o_b @ w_branch_b[i]))
        h = h + rmsnorm(merged @ w_out[i], g[3])

        h = h + 0.5 * rmsnorm(swiglu(rmsnorm(h, g[4]), w_ffn2_gu[i], w_ffn2_down[i]), g[5])

        e = p[i] @ w_ple[i]
        h = h + rmsnorm(jax.nn.sigmoid(h @ w_ple_gate[i]) * e, g[6])
    return h
```

```python
import functools
import math

import numpy as np
import jax
import jax.numpy as jnp
from jax import lax
from jax.experimental import pallas as pl
from jax.experimental.pallas import tpu as pltpu

F32 = jnp.float32
BF16 = jnp.bfloat16

D_MODEL = 1024
PLE_DIM = 256
D_FF = 2816
MB_HEADS = 8
MB_HEAD_DIM = 64
MB_WIDTH = MB_HEADS * MB_HEAD_DIM
MB_BLOCK = 256
MB_TOPK = 3
HG_HEADS = 4
HG_DK = 128
HG_DV = 128
HG_WIDTH = HG_HEADS * HG_DV
REL_BUCKETS = 32
REL_MAX_EXACT = REL_BUCKETS // 2
REL_MAX_DIST = 128
N_NORMS = 7
EPS = 1e-6

LANES = 128
VMEM_LIMIT = 56 * 1024 * 1024
TOKEN_TILE = 512
FF_CHUNK = 256
HG_CHUNK = 256
MB_QTILE = 1024
NEG = -1e30
LOG2E = 1.4426950408889634


def _rms(x, g):
    return x * lax.rsqrt(jnp.mean(x * x, axis=-1, keepdims=True) + EPS) * g


def _sigmoid(x):
    return 1.0 / (1.0 + jnp.exp(-x))


def _dot(a, b):
    return jnp.dot(a, b, preferred_element_type=F32)


def _dot_nt(a, b):
    return lax.dot_general(a, b, (((1,), (1,)), ((), ())), preferred_element_type=F32)


def _split3(x):
    hi = x.astype(BF16)
    r1 = x - hi.astype(F32)
    mid = r1.astype(BF16)
    lo = (r1 - mid.astype(F32)).astype(BF16)
    return hi, mid, lo


def _const_spec(shape):
    nd = len(shape)
    return pl.BlockSpec(shape, lambda *_: (0,) * nd, pipeline_mode=pl.Buffered(1))


def _params(sem):
    return pltpu.CompilerParams(dimension_semantics=sem, vmem_limit_bytes=VMEM_LIMIT)


def _ffn_kernel(pre, post, x_ref, g_ref, wgu_ref, wd_ref, o_ref, act_ref):
    x = x_ref[...]
    u = _rms(x, g_ref[pre:pre + 1, :]).astype(BF16)
    for c in range(D_FF // FF_CHUNK):
        lo = c * FF_CHUNK
        g = _dot(u, wgu_ref[:, lo:lo + FF_CHUNK])
        v = _dot(u, wgu_ref[:, D_FF + lo:D_FF + lo + FF_CHUNK])
        act_ref[:, lo:lo + FF_CHUNK] = (g * _sigmoid(g) * v).astype(BF16)
    y = _dot(act_ref[...], wd_ref[...])
    o_ref[...] = x + 0.5 * _rms(y, g_ref[post:post + 1, :])


def _ffn(x, gains, w_gu, w_down, pre, post):
    t = x.shape[0]
    return pl.pallas_call(
        functools.partial(_ffn_kernel, pre, post),
        grid=(t // TOKEN_TILE,),
        in_specs=[
            pl.BlockSpec((TOKEN_TILE, D_MODEL), lambda i: (i, 0)),
            _const_spec((N_NORMS, D_MODEL)),
            _const_spec((D_MODEL, 2 * D_FF)),
            _const_spec((D_FF, D_MODEL)),
        ],
        out_specs=pl.BlockSpec((TOKEN_TILE, D_MODEL), lambda i: (i, 0)),
        out_shape=jax.ShapeDtypeStruct((t, D_MODEL), F32),
        scratch_shapes=[pltpu.VMEM((TOKEN_TILE, D_FF), BF16)],
        compiler_params=_params(("parallel",)),
        name="ffn",
    )(x, gains, w_gu, w_down)


IN_SPLITS = (MB_WIDTH, MB_WIDTH, MB_WIDTH, HG_WIDTH, HG_WIDTH, HG_WIDTH, HG_WIDTH, 2 * D_MODEL)
IN_COLS = sum(IN_SPLITS)


def _inproj_kernel(h_ref, g_ref, w_ref, *out_refs):
    u = _rms(h_ref[...], g_ref[2:3, :]).astype(BF16)
    off = 0
    for ref in out_refs:
        n = ref.shape[-1]
        for c in range(0, n, 512):
            ref[:, c:c + 512] = _dot(u, w_ref[:, off + c:off + c + 512])
        off += n


def _inproj(h, gains, w_in):
    t = h.shape[0]
    return pl.pallas_call(
        _inproj_kernel,
        grid=(t // TOKEN_TILE,),
        in_specs=[
            pl.BlockSpec((TOKEN_TILE, D_MODEL), lambda i: (i, 0)),
            _const_spec((N_NORMS, D_MODEL)),
            _const_spec((D_MODEL, IN_COLS)),
        ],
        out_specs=[pl.BlockSpec((TOKEN_TILE, n), lambda i: (i, 0)) for n in IN_SPLITS],
        out_shape=[jax.ShapeDtypeStruct((t, n), F32) for n in IN_SPLITS],
        compiler_params=_params(("parallel",)),
        name="in_proj",
    )(h, gains, w_in)


def _merge_kernel(h_ref, oa_ref, ob_ref, gg_ref, g_ref, wa_ref, wb_ref, wo_ref, o_ref):
    a = _dot(oa_ref[...].astype(BF16), wa_ref[...])
    b = _dot(ob_ref[...].astype(BF16), wb_ref[...])
    merged = _sigmoid(gg_ref[:, :D_MODEL]) * a + _sigmoid(gg_ref[:, D_MODEL:]) * b
    y = _dot(merged.astype(BF16), wo_ref[...])
    o_ref[...] = h_ref[...] + _rms(y, g_ref[3:4, :])


def _merge(h, o_a, o_b, gates, gains, w_a, w_b, w_out):
    t = h.shape[0]
    row = lambda n: pl.BlockSpec((TOKEN_TILE, n), lambda i: (i, 0))
    return pl.pallas_call(
        _merge_kernel,
        grid=(t // TOKEN_TILE,),
        in_specs=[
            row(D_MODEL), row(MB_WIDTH), row(HG_WIDTH), row(2 * D_MODEL),
            _const_spec((N_NORMS, D_MODEL)),
            _const_spec((MB_WIDTH, D_MODEL)),
            _const_spec((HG_WIDTH, D_MODEL)),
            _const_spec((D_MODEL, D_MODEL)),
        ],
        out_specs=row(D_MODEL),
        out_shape=jax.ShapeDtypeStruct((t, D_MODEL), F32),
        compiler_params=_params(("parallel",)),
        name="merge",
    )(h, o_a, o_b, gates, gains, w_a, w_b, w_out)


def _ple_kernel(h_ref, p_ref, g_ref, wp_ref, wg_ref, o_ref):
    h = h_ref[...]
    e = _dot(p_ref[...].astype(BF16), wp_ref[...])
    t = _dot(h.astype(BF16), wg_ref[...])
    o_ref[...] = h + _rms(_sigmoid(t) * e, g_ref[6:7, :])


def _ple(h, p, gains, w_ple, w_ple_gate):
    t = h.shape[0]
    row = lambda n: pl.BlockSpec((TOKEN_TILE, n), lambda i: (i, 0))
    return pl.pallas_call(
        _ple_kernel,
        grid=(t // TOKEN_TILE,),
        in_specs=[
            row(D_MODEL), row(PLE_DIM),
            _const_spec((N_NORMS, D_MODEL)),
            _const_spec((PLE_DIM, D_MODEL)),
            _const_spec((D_MODEL, D_MODEL)),
        ],
        out_specs=row(D_MODEL),
        out_shape=jax.ShapeDtypeStruct((t, D_MODEL), F32),
        compiler_params=_params(("parallel",)),
        name="ple",
    )(h, p, gains, w_ple, w_ple_gate)


def _bucket_starts():
    n = np.arange(0, 2 * MB_BLOCK, dtype=np.int32)
    nf = np.maximum(n, 1).astype(np.float32)
    large = REL_MAX_EXACT + (
        np.log(nf / np.float32(REL_MAX_EXACT)) / np.float32(math.log(REL_MAX_DIST / REL_MAX_EXACT))
        * np.float32(REL_BUCKETS - REL_MAX_EXACT)).astype(np.int32)
    large = np.minimum(large, REL_BUCKETS - 1)
    bucket = np.where(n < REL_MAX_EXACT, n, large)
    assert np.all(np.diff(bucket) >= 0) and bucket[-1] == REL_BUCKETS - 1
    return [int(np.argmax(bucket >= b)) for b in range(REL_BUCKETS)]


def _moba_kernel(starts, rel_ref, q_ref, k_ref, v_ref, o_ref,
                 kb_ref, vt_ref, kmean_ref, bown_ref, bprev_ref, sel_ref):
    pair = pl.program_id(1)
    step = pl.program_id(2)
    seq = k_ref.shape[1]
    nb = seq // MB_BLOCK
    hd = MB_HEAD_DIM
    qscale = (hd ** -0.5) * LOG2E

    @pl.when(step == 0)
    def _prepare():
        ones = jnp.ones((16, seq), BF16)
        vt_ref[0, hd:hd + 16, :] = ones
        vt_ref[1, hd:hd + 16, :] = ones

        def blk(j, carry):
            r0 = pl.multiple_of(j * MB_BLOCK, MB_BLOCK)
            kblk = k_ref[0, pl.ds(r0, MB_BLOCK), :]
            kb_ref[pl.ds(r0, MB_BLOCK), :] = kblk.astype(BF16)
            kmean_ref[pl.ds(j, 1), :] = jnp.mean(kblk, axis=0, keepdims=True)
            vt = v_ref[0, pl.ds(r0, MB_BLOCK), :].T.astype(BF16)
            vt_ref[0, 0:hd, pl.ds(r0, MB_BLOCK)] = vt[0:hd]
            vt_ref[1, 0:hd, pl.ds(r0, MB_BLOCK)] = vt[hd:2 * hd]
            return carry

        lax.fori_loop(0, nb, blk, 0)

        kk = lax.broadcasted_iota(jnp.int32, (MB_BLOCK, MB_BLOCK), 0)
        qq = lax.broadcasted_iota(jnp.int32, (MB_BLOCK, MB_BLOCK), 1)
        d_own = qq - kk
        d_prev = d_own + MB_BLOCK
        for hh in range(2):
            head = pair * 2 + hh
            b_own = jnp.full((MB_BLOCK, MB_BLOCK), rel_ref[0, head], F32)
            b_prev = jnp.full((MB_BLOCK, MB_BLOCK), rel_ref[0, head], F32)
            for b in range(1, REL_BUCKETS):
                val = rel_ref[b, head]
                b_own = jnp.where(d_own >= starts[b], val, b_own)
                b_prev = jnp.where(d_prev >= starts[b], val, b_prev)
            bown_ref[hh] = jnp.where(d_own >= 0, b_own * LOG2E, NEG)
            bprev_ref[hh] = b_prev * LOG2E

    km_hi, km_mid, km_lo = _split3(kmean_ref[...])
    row_head = lax.broadcasted_iota(jnp.int32, (2 * hd, MB_BLOCK), 0) // hd
    blk_row = lax.broadcasted_iota(jnp.int32, (nb, MB_BLOCK), 0)

    def qblock(qi, carry):
        qb = step * (MB_QTILE // MB_BLOCK) + qi
        q0 = pl.multiple_of(qi * MB_BLOCK, MB_BLOCK)
        qt = q_ref[0, pl.ds(q0, MB_BLOCK), :].T
        outs = []
        for hh in range(2):
            c_far = rel_ref[REL_BUCKETS - 1, pair * 2 + hh] * LOG2E
            qm = jnp.where(row_head == hh, qt, 0.0)
            q_hi, q_mid, q_lo = _split3(qm)
            gate = (_dot(km_hi, q_hi) + (_dot(km_hi, q_mid) + _dot(km_mid, q_hi))
                    + (_dot(km_hi, q_lo) + _dot(km_mid, q_mid) + _dot(km_lo, q_hi)))
            gate = jnp.where(blk_row < qb, gate, -jnp.inf)
            sel = jnp.full((nb, MB_BLOCK), NEG, F32)
            for _ in range(MB_TOPK):
                mx = jnp.max(gate, axis=0, keepdims=True)
                cand = jnp.where((gate == mx) & (mx > -jnp.inf), blk_row, nb)
                pick = blk_row == jnp.min(cand, axis=0, keepdims=True)
                sel = jnp.where(pick, 0.0, sel)
                gate = jnp.where(pick, -jnp.inf, gate)
            sel_ref[hh] = sel

            qs = (qm * qscale).astype(BF16)
            c0 = pl.multiple_of(qb * MB_BLOCK, MB_BLOCK)
            s = _dot(kb_ref[pl.ds(c0, MB_BLOCK), :], qs) + bown_ref[hh]
            m = jnp.max(s, axis=0, keepdims=True)
            p = jnp.exp2(s - m).astype(BF16)
            acc = _dot(vt_ref[hh, :, pl.ds(c0, MB_BLOCK)], p)

            def past(j, mc, bias):
                m_old, acc_old = mc
                r0 = pl.multiple_of(j * MB_BLOCK, MB_BLOCK)
                s = _dot(kb_ref[pl.ds(r0, MB_BLOCK), :], qs) + bias
                m_new = jnp.maximum(m_old, jnp.max(s, axis=0, keepdims=True))
                p = jnp.exp2(s - m_new).astype(BF16)
                acc_new = acc_old * jnp.exp2(m_old - m_new) + _dot(
                    vt_ref[hh, :, pl.ds(r0, MB_BLOCK)], p)
                return m_new, acc_new

            def prev_body(_, mc):
                j = qb - 1
                return past(j, mc, bprev_ref[hh] + sel_ref[hh, pl.ds(j, 1), :])

            def far_body(j, mc):
                return past(j, mc, sel_ref[hh, pl.ds(j, 1), :] + c_far)

            mc = lax.fori_loop(0, jnp.minimum(qb, 1), prev_body, (m, acc))
            m, acc = lax.fori_loop(0, jnp.maximum(qb - 1, 0), far_body, mc)
            outs.append(acc[0:hd] / acc[hd:hd + 1])
        o_ref[0, pl.ds(q0, MB_BLOCK), :] = jnp.concatenate(outs, axis=0).T
        return carry

    lax.fori_loop(0, MB_QTILE // MB_BLOCK, qblock, 0)


def _moba(q, k, v, rel_table):
    bsz, seq, _ = q.shape
    assert seq % MB_QTILE == 0
    nb = seq // MB_BLOCK
    npair = MB_HEADS // 2
    return pl.pallas_call(
        functools.partial(_moba_kernel, _bucket_starts()),
        grid=(bsz, npair, seq // MB_QTILE),
        in_specs=[
            pl.BlockSpec(memory_space=pltpu.SMEM),
            pl.BlockSpec((1, MB_QTILE, LANES), lambda b, p, t: (b, t, p)),
            pl.BlockSpec((1, seq, LANES), lambda b, p, t: (b, 0, p)),
            pl.BlockSpec((1, seq, LANES), lambda b, p, t: (b, 0, p)),
        ],
        out_specs=pl.BlockSpec((1, MB_QTILE, LANES), lambda b, p, t: (b, t, p)),
        out_shape=jax.ShapeDtypeStruct((bsz, seq, MB_WIDTH), F32),
        scratch_shapes=[
            pltpu.VMEM((seq, LANES), BF16),
            pltpu.VMEM((2, MB_HEAD_DIM + 16, seq), BF16),
            pltpu.VMEM((nb, LANES), F32),
            pltpu.VMEM((2, MB_BLOCK, MB_BLOCK), F32),
            pltpu.VMEM((2, MB_BLOCK, MB_BLOCK), F32),
            pltpu.VMEM((2, nb, MB_BLOCK), F32),
        ],
        compiler_params=_params(("parallel", "parallel", "arbitrary")),
        name="moba",
    )(rel_table, q, k, v)


def _segment_ref(b, n):
    c = b.shape[0]
    if 2 * n >= 8:
        b3 = b.reshape(c // (2 * n), 2 * n, b.shape[1])
        return jnp.broadcast_to(b3[:, n - 1:n, :], b3.shape).reshape(b.shape)
    pos = lax.broadcasted_iota(jnp.int32, b.shape, 0) % (2 * n)
    out = b
    for p in range(2 * n):
        shift = p - (n - 1)
        if shift != 0:
            out = jnp.where(pos == p, pltpu.roll(b, shift % c, 0), out)
    return out


def _hgrn_kernel(layer, hq_ref, hf_ref, hi_ref, hg_ref, lbp_ref, nw_ref, o_ref, state_ref):
    c = HG_CHUNK

    @pl.when(pl.program_id(1) == 0)
    def _reset():
        state_ref[...] = jnp.zeros_like(state_ref)

    lp = lbp_ref[...]
    e = jnp.exp(lp - jnp.max(lp, axis=0, keepdims=True))
    sm = e / jnp.sum(e, axis=0, keepdims=True)
    lb_all = sm[0:1, :]
    for r in range(1, layer + 1):
        lb_all = lb_all + sm[r:r + 1, :]

    row = lax.broadcasted_iota(jnp.int32, (c, c), 0)
    col = lax.broadcasted_iota(jnp.int32, (c, c), 1)
    tril = jnp.where(row >= col, 1.0, 0.0).astype(BF16)
    rpos = lax.broadcasted_iota(jnp.int32, (c, HG_DK), 0)
    levels = [1 << i for i in range(int(math.log2(c)) - 1, -1, -1)]

    for hh in range(HG_HEADS):
        sl = slice(hh * HG_DK, (hh + 1) * HG_DK)
        lb = lb_all[:, sl]
        xq = hq_ref[0, :, sl]
        f = lb + (1.0 - lb) * _sigmoid(hf_ref[0, :, sl])
        logf = jnp.log(f)
        kk = 1.0 - f
        q = xq * _sigmoid(xq)
        v = hi_ref[0, :, sl]
        vb = v.astype(BF16)

        l_hi, l_mid, l_lo = _split3(logf)
        b = _dot(tril, l_hi) + (_dot(tril, l_mid) + _dot(tril, l_lo))

        attn = jnp.zeros((c, c), F32)
        for n in levels:
            e_l = jnp.exp(-jnp.abs(b - _segment_ref(b, n)))
            second = (rpos & n) != 0
            q_l = jnp.where(second, q * e_l, 0.0).astype(BF16)
            k_l = jnp.where(second, 0.0, kk * e_l).astype(BF16)
            same_seg = (row // (2 * n)) == (col // (2 * n))
            attn = jnp.where(same_seg, attn + _dot_nt(q_l, k_l), attn)

        st = state_ref[hh]
        o = (_dot(attn.astype(BF16), vb)
             + jnp.sum(q * kk, axis=-1, keepdims=True) * v
             + _dot_nt((q * jnp.exp(b)).astype(BF16), st.astype(BF16)))
        b_last = b[c - 1:c, :]
        k_dec = (kk * jnp.exp(b_last - b)).astype(BF16)
        state_ref[hh] = st * jnp.exp(b_last) + lax.dot_general(
            vb, k_dec, (((0,), (0,)), ((), ())), preferred_element_type=F32)

        xg = hg_ref[0, :, sl]
        o_ref[0, :, sl] = _rms(o, nw_ref[:, :]) * (xg * _sigmoid(xg))


def _hgrn(hq, hf, hi, hg, lb_param, norm_w, layer):
    bsz, seq, _ = hq.shape
    assert seq % HG_CHUNK == 0
    blk = pl.BlockSpec((1, HG_CHUNK, HG_WIDTH), lambda b, t: (b, t, 0))
    return pl.pallas_call(
        functools.partial(_hgrn_kernel, layer),
        grid=(bsz, seq // HG_CHUNK),
        in_specs=[blk, blk, blk, blk,
                  pl.BlockSpec(lb_param.shape, lambda b, t: (0, 0)),
                  pl.BlockSpec((1, HG_DV), lambda b, t: (0, 0))],
        out_specs=blk,
        out_shape=jax.ShapeDtypeStruct((bsz, seq, HG_WIDTH), F32),
        scratch_shapes=[pltpu.VMEM((HG_HEADS, HG_DV, HG_DK), F32)],
        compiler_params=_params(("parallel", "arbitrary")),
        name="hgrn",
    )(hq, hf, hi, hg, lb_param, norm_w)


def kernel(x, p, w_ffn1_gu, w_ffn1_down, w_in, w_branch_a, w_branch_b, w_out,
           w_ffn2_gu, w_ffn2_down, w_ple, w_ple_gate, norm_gains, hg_norm_w,
           lb_param, rel_table):
    bsz, seq, _ = x.shape
    t = bsz * seq
    assert t % TOKEN_TILE == 0
    h = x.reshape(t, D_MODEL)
    for i in range(p.shape[0]):
        g = norm_gains[i]
        h = _ffn(h, g, w_ffn1_gu[i].astype(BF16), w_ffn1_down[i].astype(BF16), 0, 1)
        mq, mk, mv, hq, hf, hi, hg, gates = _inproj(h, g, w_in[i].astype(BF16))
        r3 = lambda a: a.reshape(bsz, seq, a.shape[-1])
        o_a = _moba(r3(mq), r3(mk), r3(mv), rel_table)
        o_b = _hgrn(r3(hq), r3(hf), r3(hi), r3(hg), lb_param, hg_norm_w[i:i + 1], i)
        h = _merge(h, o_a.reshape(t, MB_WIDTH), o_b.reshape(t, HG_WIDTH), gates, g,
                   w_branch_a[i].astype(BF16), w_branch_b[i].astype(BF16),
                   w_out[i].astype(BF16))
        h = _ffn(h, g, w_ffn2_gu[i].astype(BF16), w_ffn2_down[i].astype(BF16), 4, 5)
        h = _ple(h, p[i].reshape(t, PLE_DIM), g, w_ple[i].astype(BF16),
                 w_ple_gate[i].astype(BF16))
    return h.reshape(bsz, seq, D_MODEL)
```

```python
import functools
import math

import numpy as np
import jax
import jax.numpy as jnp
from jax import lax
from jax.experimental import pallas as pl
from jax.experimental.pallas import tpu as pltpu

F32 = jnp.float32
BF16 = jnp.bfloat16

D_MODEL = 1024
PLE_DIM = 256
D_FF = 2816
MB_HEADS = 8
MB_HEAD_DIM = 64
MB_WIDTH = MB_HEADS * MB_HEAD_DIM
MB_BLOCK = 256
MB_TOPK = 3
HG_HEADS = 4
HG_DK = 128
HG_DV = 128
HG_WIDTH = HG_HEADS * HG_DV
REL_BUCKETS = 32
REL_MAX_EXACT = REL_BUCKETS // 2
REL_MAX_DIST = 128
N_NORMS = 7
EPS = 1e-6

LANES = 128
VMEM_LIMIT = 56 * 1024 * 1024
TOKEN_TILE = 512
FF_CHUNK = 256
HG_CHUNK = 256
MB_QTILE = 1024
FAR_GROUP = 2
PAD_END = (FAR_GROUP - 1) * 256
NEG = -1e30
LOG2E = 1.4426950408889634


def _rms(x, g):
    return x * lax.rsqrt(jnp.mean(x * x, axis=-1, keepdims=True) + EPS) * g


def _sigmoid(x):
    return 1.0 / (1.0 + jnp.exp(-x))


def _dot(a, b):
    return jnp.dot(a, b, preferred_element_type=F32)


def _dot_nt(a, b):
    return lax.dot_general(a, b, (((1,), (1,)), ((), ())), preferred_element_type=F32)


def _split3(x):
    hi = x.astype(BF16)
    r1 = x - hi.astype(F32)
    mid = r1.astype(BF16)
    lo = (r1 - mid.astype(F32)).astype(BF16)
    return hi, mid, lo


def _const_spec(shape):
    nd = len(shape)
    return pl.BlockSpec(shape, lambda *_: (0,) * nd, pipeline_mode=pl.Buffered(1))


def _params(sem):
    return pltpu.CompilerParams(dimension_semantics=sem, vmem_limit_bytes=VMEM_LIMIT)


def _ffn_kernel(pre, post, x_ref, g_ref, wgu_ref, wd_ref, o_ref, act_ref):
    x = x_ref[...]
    u = _rms(x, g_ref[pre:pre + 1, :]).astype(BF16)
    for c in range(D_FF // FF_CHUNK):
        lo = c * FF_CHUNK
        g = _dot(u, wgu_ref[:, lo:lo + FF_CHUNK])
        v = _dot(u, wgu_ref[:, D_FF + lo:D_FF + lo + FF_CHUNK])
        act_ref[:, lo:lo + FF_CHUNK] = (g * _sigmoid(g) * v).astype(BF16)
    y = _dot(act_ref[...], wd_ref[...])
    o_ref[...] = x + 0.5 * _rms(y, g_ref[post:post + 1, :])


def _ffn(x, gains, w_gu, w_down, pre, post):
    t = x.shape[0]
    return pl.pallas_call(
        functools.partial(_ffn_kernel, pre, post),
        grid=(t // TOKEN_TILE,),
        in_specs=[
            pl.BlockSpec((TOKEN_TILE, D_MODEL), lambda i: (i, 0)),
            _const_spec((N_NORMS, D_MODEL)),
            _const_spec((D_MODEL, 2 * D_FF)),
            _const_spec((D_FF, D_MODEL)),
        ],
        out_specs=pl.BlockSpec((TOKEN_TILE, D_MODEL), lambda i: (i, 0)),
        out_shape=jax.ShapeDtypeStruct((t, D_MODEL), F32),
        scratch_shapes=[pltpu.VMEM((TOKEN_TILE, D_FF), BF16)],
        compiler_params=_params(("parallel",)),
        name="ffn",
    )(x, gains, w_gu, w_down)


IN_SPLITS = (MB_WIDTH, MB_WIDTH, MB_WIDTH, HG_WIDTH, HG_WIDTH, HG_WIDTH, HG_WIDTH, 2 * D_MODEL)
IN_COLS = sum(IN_SPLITS)


def _inproj_kernel(h_ref, g_ref, w_ref, *out_refs):
    u = _rms(h_ref[...], g_ref[2:3, :]).astype(BF16)
    off = 0
    for ref in out_refs:
        n = ref.shape[-1]
        for c in range(0, n, 512):
            ref[:, c:c + 512] = _dot(u, w_ref[:, off + c:off + c + 512])
        off += n


def _inproj(h, gains, w_in):
    t = h.shape[0]
    return pl.pallas_call(
        _inproj_kernel,
        grid=(t // TOKEN_TILE,),
        in_specs=[
            pl.BlockSpec((TOKEN_TILE, D_MODEL), lambda i: (i, 0)),
            _const_spec((N_NORMS, D_MODEL)),
            _const_spec((D_MODEL, IN_COLS)),
        ],
        out_specs=[pl.BlockSpec((TOKEN_TILE, n), lambda i: (i, 0)) for n in IN_SPLITS],
        out_shape=[jax.ShapeDtypeStruct((t, n), F32) for n in IN_SPLITS],
        compiler_params=_params(("parallel",)),
        name="in_proj",
    )(h, gains, w_in)


def _merge_kernel(h_ref, oa_ref, ob_ref, gg_ref, g_ref, wa_ref, wb_ref, wo_ref, o_ref):
    a = _dot(oa_ref[...].astype(BF16), wa_ref[...])
    b = _dot(ob_ref[...].astype(BF16), wb_ref[...])
    merged = _sigmoid(gg_ref[:, :D_MODEL]) * a + _sigmoid(gg_ref[:, D_MODEL:]) * b
    y = _dot(merged.astype(BF16), wo_ref[...])
    o_ref[...] = h_ref[...] + _rms(y, g_ref[3:4, :])


def _merge(h, o_a, o_b, gates, gains, w_a, w_b, w_out):
    t = h.shape[0]
    row = lambda n: pl.BlockSpec((TOKEN_TILE, n), lambda i: (i, 0))
    return pl.pallas_call(
        _merge_kernel,
        grid=(t // TOKEN_TILE,),
        in_specs=[
            row(D_MODEL), row(MB_WIDTH), row(HG_WIDTH), row(2 * D_MODEL),
            _const_spec((N_NORMS, D_MODEL)),
            _const_spec((MB_WIDTH, D_MODEL)),
            _const_spec((HG_WIDTH, D_MODEL)),
            _const_spec((D_MODEL, D_MODEL)),
        ],
        out_specs=row(D_MODEL),
        out_shape=jax.ShapeDtypeStruct((t, D_MODEL), F32),
        compiler_params=_params(("parallel",)),
        name="merge",
    )(h, o_a, o_b, gates, gains, w_a, w_b, w_out)


def _ple_kernel(h_ref, p_ref, g_ref, wp_ref, wg_ref, o_ref):
    h = h_ref[...]
    e = _dot(p_ref[...].astype(BF16), wp_ref[...])
    t = _dot(h.astype(BF16), wg_ref[...])
    o_ref[...] = h + _rms(_sigmoid(t) * e, g_ref[6:7, :])


def _ple(h, p, gains, w_ple, w_ple_gate):
    t = h.shape[0]
    row = lambda n: pl.BlockSpec((TOKEN_TILE, n), lambda i: (i, 0))
    return pl.pallas_call(
        _ple_kernel,
        grid=(t // TOKEN_TILE,),
        in_specs=[
            row(D_MODEL), row(PLE_DIM),
            _const_spec((N_NORMS, D_MODEL)),
            _const_spec((PLE_DIM, D_MODEL)),
            _const_spec((D_MODEL, D_MODEL)),
        ],
        out_specs=row(D_MODEL),
        out_shape=jax.ShapeDtypeStruct((t, D_MODEL), F32),
        compiler_params=_params(("parallel",)),
        name="ple",
    )(h, p, gains, w_ple, w_ple_gate)


def _bucket_starts():
    n = np.arange(0, 2 * MB_BLOCK, dtype=np.int32)
    nf = np.maximum(n, 1).astype(np.float32)
    large = REL_MAX_EXACT + (
        np.log(nf / np.float32(REL_MAX_EXACT)) / np.float32(math.log(REL_MAX_DIST / REL_MAX_EXACT))
        * np.float32(REL_BUCKETS - REL_MAX_EXACT)).astype(np.int32)
    large = np.minimum(large, REL_BUCKETS - 1)
    bucket = np.where(n < REL_MAX_EXACT, n, large)
    assert np.all(np.diff(bucket) >= 0) and bucket[-1] == REL_BUCKETS - 1
    return [int(np.argmax(bucket >= b)) for b in range(REL_BUCKETS)]


def _moba_kernel(starts, rel_ref, q_ref, k_ref, v_ref, o_ref,
                 kb_ref, vt_ref, kmean_ref, bown_ref, bprev_ref, sel_ref, qs_ref,
                 m_ref, acc_ref):
    pair = pl.program_id(1)
    step = pl.program_id(2)
    seq = k_ref.shape[1]
    nb = seq // MB_BLOCK
    hd = MB_HEAD_DIM
    qscale = (hd ** -0.5) * LOG2E

    @pl.when(step == 0)
    def _prepare():
        kb_ref[0:MB_BLOCK, :] = jnp.zeros((MB_BLOCK, LANES), BF16)
        kb_ref[MB_BLOCK + seq:, :] = jnp.zeros((PAD_END, LANES), BF16)
        sel_ref[:, nb:, :] = jnp.full((2, FAR_GROUP, MB_QTILE), NEG, F32)
        for hh in range(2):
            vt_ref[hh, :, 0:MB_BLOCK] = jnp.zeros((hd + 16, MB_BLOCK), BF16)
            vt_ref[hh, :, MB_BLOCK + seq:] = jnp.zeros((hd + 16, PAD_END), BF16)
            vt_ref[hh, hd:hd + 16, MB_BLOCK:MB_BLOCK + seq] = jnp.ones((16, seq), BF16)

        def blk(j, carry):
            r0 = pl.multiple_of(j * MB_BLOCK, MB_BLOCK)
            r1 = pl.multiple_of(r0 + MB_BLOCK, MB_BLOCK)
            kblk = k_ref[0, pl.ds(r0, MB_BLOCK), :]
            kb_ref[pl.ds(r1, MB_BLOCK), :] = kblk.astype(BF16)
            kmean_ref[pl.ds(j, 1), :] = jnp.mean(kblk, axis=0, keepdims=True)
            vt = v_ref[0, pl.ds(r0, MB_BLOCK), :].T.astype(BF16)
            vt_ref[0, 0:hd, pl.ds(r1, MB_BLOCK)] = vt[0:hd]
            vt_ref[1, 0:hd, pl.ds(r1, MB_BLOCK)] = vt[hd:2 * hd]
            return carry

        lax.fori_loop(0, nb, blk, 0)

        kk = lax.broadcasted_iota(jnp.int32, (MB_BLOCK, MB_BLOCK), 0)
        qq = lax.broadcasted_iota(jnp.int32, (MB_BLOCK, MB_BLOCK), 1)
        d_own = qq - kk
        d_prev = d_own + MB_BLOCK
        for hh in range(2):
            head = pair * 2 + hh
            b_own = jnp.full((MB_BLOCK, MB_BLOCK), rel_ref[0, head], F32)
            b_prev = jnp.full((MB_BLOCK, MB_BLOCK), rel_ref[0, head], F32)
            for b in range(1, REL_BUCKETS):
                val = rel_ref[b, head]
                b_own = jnp.where(d_own >= starts[b], val, b_own)
                b_prev = jnp.where(d_prev >= starts[b], val, b_prev)
            bown_ref[hh] = jnp.where(d_own >= 0, b_own * LOG2E, NEG)
            bprev_ref[hh] = b_prev * LOG2E

    nq = MB_QTILE // MB_BLOCK
    qb0 = step * nq
    c_far = [rel_ref[REL_BUCKETS - 1, pair * 2 + hh] * LOG2E for hh in range(2)]

    km_hi, km_mid, km_lo = _split3(kmean_ref[...])
    qt = q_ref[0].T
    row_head = lax.broadcasted_iota(jnp.int32, (2 * hd, MB_QTILE), 0) // hd
    blk_row = lax.broadcasted_iota(jnp.int32, (nb, MB_QTILE), 0)
    own_blk = qb0 + lax.broadcasted_iota(jnp.int32, (nb, MB_QTILE), 1) // MB_BLOCK
    for hh in range(2):
        qm = jnp.where(row_head == hh, qt, 0.0)
        qs = (qm * qscale).astype(BF16)
        for qi in range(nq):
            c0 = (2 * qi + hh) * MB_BLOCK
            qs_ref[:, c0:c0 + MB_BLOCK] = qs[:, qi * MB_BLOCK:(qi + 1) * MB_BLOCK]
        q_hi, q_mid, q_lo = _split3(qm)
        gate = (_dot(km_hi, q_hi) + (_dot(km_hi, q_mid) + _dot(km_mid, q_hi))
                + (_dot(km_hi, q_lo) + _dot(km_mid, q_mid) + _dot(km_lo, q_hi)))
        gate = jnp.where(blk_row < own_blk, gate, -jnp.inf)
        sel = jnp.full((nb, MB_QTILE), NEG, F32)
        for _ in range(MB_TOPK):
            mx = jnp.max(gate, axis=0, keepdims=True)
            cand = jnp.where((gate == mx) & (mx > -jnp.inf), blk_row, nb)
            pick = blk_row == jnp.min(cand, axis=0, keepdims=True)
            sel = jnp.where(pick, 0.0, sel)
            gate = jnp.where(pick, -jnp.inf, gate)
        sel_ref[hh, 0:nb, :] = sel

    def update(chains, kpos, biases, first):
        c_lo = chains[0][0] * 2 + chains[0][1]
        s_all = _dot(kb_ref[kpos, :],
                     qs_ref[:, c_lo * MB_BLOCK:(c_lo + len(chains)) * MB_BLOCK])
        for n, (qi, hh) in enumerate(chains):
            c = qi * 2 + hh
            s = jnp.concatenate(
                [s_all[u * MB_BLOCK:(u + 1) * MB_BLOCK, n * MB_BLOCK:(n + 1) * MB_BLOCK] + bias
                 for u, bias in enumerate(biases[n])], axis=0)
            m_blk = jnp.max(s, axis=0, keepdims=True)
            m_old = None if first else m_ref[c:c + 1, :]
            m_new = m_blk if first else jnp.maximum(m_old, m_blk)
            pv = _dot(vt_ref[hh, :, kpos], jnp.exp2(s - m_new).astype(BF16))
            acc_ref[c] = pv if first else acc_ref[c] * jnp.exp2(m_old - m_new) + pv
            m_ref[c:c + 1, :] = m_new

    def far_row(hh, qi, j, limit):
        row = sel_ref[hh, pl.ds(jnp.maximum(j, 0), 1), qi * MB_BLOCK:(qi + 1) * MB_BLOCK]
        return jnp.where((j >= 0) & (j < limit), row + c_far[hh], NEG)

    for qi in range(nq):
        qb = qb0 + qi
        kpos = pl.ds(pl.multiple_of(qb0 * MB_BLOCK, MB_BLOCK), (qi + 2) * MB_BLOCK)
        biases = []
        for hh in range(2):
            far = [far_row(hh, qi, qb0 - 1 + u, qb - 1) for u in range(qi)]
            prev = sel_ref[hh, pl.ds(jnp.maximum(qb - 1, 0), 1), qi * MB_BLOCK:(qi + 1) * MB_BLOCK]
            prev = bprev_ref[hh] + jnp.where(qb >= 1, prev, NEG)
            biases.append(far + [prev, bown_ref[hh]])
        update([(qi, 0), (qi, 1)], kpos, biases, True)

    nfar = qb0 - 1
    chains = [(qi, hh) for qi in range(nq) for hh in range(2)]

    def group(g, carry):
        j0 = g * FAR_GROUP
        kpos = pl.ds(pl.multiple_of((j0 + 1) * MB_BLOCK, MB_BLOCK), FAR_GROUP * MB_BLOCK)
        biases = [[far_row(hh, qi, j0 + u, nfar) for u in range(FAR_GROUP)]
                  for qi, hh in chains]
        update(chains, kpos, biases, False)
        return carry

    lax.fori_loop(0, (jnp.maximum(nfar, 0) + FAR_GROUP - 1) // FAR_GROUP, group, 0)

    for qi in range(nq):
        outs = [acc_ref[2 * qi + hh, 0:hd, :] / acc_ref[2 * qi + hh, hd:hd + 1, :]
                for hh in range(2)]
        o_ref[0, qi * MB_BLOCK:(qi + 1) * MB_BLOCK, :] = jnp.concatenate(outs, axis=0).T


def _moba(q, k, v, rel_table):
    bsz, seq, _ = q.shape
    assert seq % MB_QTILE == 0
    nb = seq // MB_BLOCK
    npair = MB_HEADS // 2
    padded = MB_BLOCK + seq + PAD_END
    return pl.pallas_call(
        functools.partial(_moba_kernel, _bucket_starts()),
        grid=(bsz, npair, seq // MB_QTILE),
        in_specs=[
            pl.BlockSpec(memory_space=pltpu.SMEM),
            pl.BlockSpec((1, MB_QTILE, LANES), lambda b, p, t: (b, t, p)),
            pl.BlockSpec((1, seq, LANES), lambda b, p, t: (b, 0, p)),
            pl.BlockSpec((1, seq, LANES), lambda b, p, t: (b, 0, p)),
        ],
        out_specs=pl.BlockSpec((1, MB_QTILE, LANES), lambda b, p, t: (b, t, p)),
        out_shape=jax.ShapeDtypeStruct((bsz, seq, MB_WIDTH), F32),
        scratch_shapes=[
            pltpu.VMEM((padded, LANES), BF16),
            pltpu.VMEM((2, MB_HEAD_DIM + 16, padded), BF16),
            pltpu.VMEM((nb, LANES), F32),
            pltpu.VMEM((2, MB_BLOCK, MB_BLOCK), F32),
            pltpu.VMEM((2, MB_BLOCK, MB_BLOCK), F32),
            pltpu.VMEM((2, nb + FAR_GROUP, MB_QTILE), F32),
            pltpu.VMEM((LANES, 2 * MB_QTILE), BF16),
            pltpu.VMEM((2 * MB_QTILE // MB_BLOCK, MB_BLOCK), F32),
            pltpu.VMEM((2 * MB_QTILE // MB_BLOCK, MB_HEAD_DIM + 16, MB_BLOCK), F32),
        ],
        compiler_params=_params(("parallel", "parallel", "arbitrary")),
        name="moba",
    )(rel_table, q, k, v)


def _segment_ref(b, n):
    c = b.shape[0]
    if 2 * n >= 8:
        b3 = b.reshape(c // (2 * n), 2 * n, b.shape[1])
        return jnp.broadcast_to(b3[:, n - 1:n, :], b3.shape).reshape(b.shape)
    pos = lax.broadcasted_iota(jnp.int32, b.shape, 0) % (2 * n)
    out = b
    for p in range(2 * n):
        shift = p - (n - 1)
        if shift != 0:
            out = jnp.where(pos == p, pltpu.roll(b, shift % c, 0), out)
    return out


def _hgrn_kernel(layer, hq_ref, hf_ref, hi_ref, hg_ref, lbp_ref, nw_ref, o_ref, state_ref):
    c = HG_CHUNK

    @pl.when(pl.program_id(1) == 0)
    def _reset():
        state_ref[...] = jnp.zeros_like(state_ref)

    lp = lbp_ref[...]
    e = jnp.exp(lp - jnp.max(lp, axis=0, keepdims=True))
    sm = e / jnp.sum(e, axis=0, keepdims=True)
    lb_all = sm[0:1, :]
    for r in range(1, layer + 1):
        lb_all = lb_all + sm[r:r + 1, :]

    row = lax.broadcasted_iota(jnp.int32, (c, c), 0)
    col = lax.broadcasted_iota(jnp.int32, (c, c), 1)
    tril = jnp.where(row >= col, 1.0, 0.0).astype(BF16)
    rpos = lax.broadcasted_iota(jnp.int32, (c, HG_DK), 0)
    levels = [1 << i for i in range(int(math.log2(c)) - 1, -1, -1)]

    for hh in range(HG_HEADS):
        sl = slice(hh * HG_DK, (hh + 1) * HG_DK)
        lb = lb_all[:, sl]
        xq = hq_ref[0, :, sl]
        f = lb + (1.0 - lb) * _sigmoid(hf_ref[0, :, sl])
        logf = jnp.log(f)
        kk = 1.0 - f
        q = xq * _sigmoid(xq)
        v = hi_ref[0, :, sl]
        vb = v.astype(BF16)

        l_hi, l_mid, l_lo = _split3(logf)
        b = _dot(tril, l_hi) + (_dot(tril, l_mid) + _dot(tril, l_lo))

        attn = jnp.zeros((c, c), F32)
        for n in levels:
            e_l = jnp.exp(-jnp.abs(b - _segment_ref(b, n)))
            second = (rpos & n) != 0
            q_l = jnp.where(second, q * e_l, 0.0).astype(BF16)
            k_l = jnp.where(second, 0.0, kk * e_l).astype(BF16)
            same_seg = (row // (2 * n)) == (col // (2 * n))
            attn = jnp.where(same_seg, attn + _dot_nt(q_l, k_l), attn)

        st = state_ref[hh]
        o = (_dot(attn.astype(BF16), vb)
             + jnp.sum(q * kk, axis=-1, keepdims=True) * v
             + _dot_nt((q * jnp.exp(b)).astype(BF16), st.astype(BF16)))
        b_last = b[c - 1:c, :]
        k_dec = (kk * jnp.exp(b_last - b)).astype(BF16)
        state_ref[hh] = st * jnp.exp(b_last) + lax.dot_general(
            vb, k_dec, (((0,), (0,)), ((), ())), preferred_element_type=F32)

        xg = hg_ref[0, :, sl]
        o_ref[0, :, sl] = _rms(o, nw_ref[:, :]) * (xg * _sigmoid(xg))


def _hgrn(hq, hf, hi, hg, lb_param, norm_w, layer):
    bsz, seq, _ = hq.shape
    assert seq % HG_CHUNK == 0
    blk = pl.BlockSpec((1, HG_CHUNK, HG_WIDTH), lambda b, t: (b, t, 0))
    return pl.pallas_call(
        functools.partial(_hgrn_kernel, layer),
        grid=(bsz, seq // HG_CHUNK),
        in_specs=[blk, blk, blk, blk,
                  pl.BlockSpec(lb_param.shape, lambda b, t: (0, 0)),
                  pl.BlockSpec((1, HG_DV), lambda b, t: (0, 0))],
        out_specs=blk,
        out_shape=jax.ShapeDtypeStruct((bsz, seq, HG_WIDTH), F32),
        scratch_shapes=[pltpu.VMEM((HG_HEADS, HG_DV, HG_DK), F32)],
        compiler_params=_params(("parallel", "arbitrary")),
        name="hgrn",
    )(hq, hf, hi, hg, lb_param, norm_w)


def kernel(x, p, w_ffn1_gu, w_ffn1_down, w_in, w_branch_a, w_branch_b, w_out,
           w_ffn2_gu, w_ffn2_down, w_ple, w_ple_gate, norm_gains, hg_norm_w,
           lb_param, rel_table):
    bsz, seq, _ = x.shape
    t = bsz * seq
    assert t % TOKEN_TILE == 0
    h = x.reshape(t, D_MODEL)
    for i in range(p.shape[0]):
        g = norm_gains[i]
        h = _ffn(h, g, w_ffn1_gu[i].astype(BF16), w_ffn1_down[i].astype(BF16), 0, 1)
        mq, mk, mv, hq, hf, hi, hg, gates = _inproj(h, g, w_in[i].astype(BF16))
        r3 = lambda a: a.reshape(bsz, seq, a.shape[-1])
        o_a = _moba(r3(mq), r3(mk), r3(mv), rel_table)
        o_b = _hgrn(r3(hq), r3(hf), r3(hi), r3(hg), lb_param, hg_norm_w[i:i + 1], i)
        h = _merge(h, o_a.reshape(t, MB_WIDTH), o_b.reshape(t, HG_WIDTH), gates, g,
                   w_branch_a[i].astype(BF16), w_branch_b[i].astype(BF16),
                   w_out[i].astype(BF16))
        h = _ffn(h, g, w_ffn2_gu[i].astype(BF16), w_ffn2_down[i].astype(BF16), 4, 5)
        h = _ple(h, p[i].reshape(t, PLE_DIM), g, w_ple[i].astype(BF16),
                 w_ple_gate[i].astype(BF16))
    return h.reshape(bsz, seq, D_MODEL)
```

```python
import functools
import math

import numpy as np
import jax
import jax.numpy as jnp
from jax import lax
from jax.experimental import pallas as pl
from jax.experimental.pallas import tpu as pltpu

F32 = jnp.float32
BF16 = jnp.bfloat16

D_MODEL = 1024
PLE_DIM = 256
D_FF = 2816
MB_HEADS = 8
MB_HEAD_DIM = 64
MB_WIDTH = MB_HEADS * MB_HEAD_DIM
MB_BLOCK = 256
MB_TOPK = 3
HG_HEADS = 4
HG_DK = 128
HG_DV = 128
HG_WIDTH = HG_HEADS * HG_DV
REL_BUCKETS = 32
REL_MAX_EXACT = REL_BUCKETS // 2
REL_MAX_DIST = 128
N_NORMS = 7
EPS = 1e-6

LANES = 128
VMEM_LIMIT = 56 * 1024 * 1024
TOKEN_TILE = 512
FF_CHUNK = 256
HG_CHUNK = 256
MB_QTILE = 1024
FAR_GROUP = 2
PAD_BLOCKS = 3 * FAR_GROUP
PAD_END = PAD_BLOCKS * 256
NEG = -1e30
LOG2E = 1.4426950408889634


def _rms(x, g):
    return x * lax.rsqrt(jnp.mean(x * x, axis=-1, keepdims=True) + EPS) * g


def _sigmoid(x):
    return 1.0 / (1.0 + jnp.exp(-x))


def _dot(a, b):
    return jnp.dot(a, b, preferred_element_type=F32)


def _dot_nt(a, b):
    return lax.dot_general(a, b, (((1,), (1,)), ((), ())), preferred_element_type=F32)


def _split3(x):
    hi = x.astype(BF16)
    r1 = x - hi.astype(F32)
    mid = r1.astype(BF16)
    lo = (r1 - mid.astype(F32)).astype(BF16)
    return hi, mid, lo


def _const_spec(shape):
    nd = len(shape)
    return pl.BlockSpec(shape, lambda *_: (0,) * nd, pipeline_mode=pl.Buffered(1))


def _params(sem, flags=None):
    return pltpu.CompilerParams(dimension_semantics=sem, vmem_limit_bytes=VMEM_LIMIT, flags=flags)


def _ffn_kernel(pre, post, x_ref, g_ref, wgu_ref, wd_ref, o_ref, act_ref):
    x = x_ref[...]
    u = _rms(x, g_ref[pre:pre + 1, :]).astype(BF16)
    for c in range(D_FF // FF_CHUNK):
        lo = c * FF_CHUNK
        g = _dot(u, wgu_ref[:, lo:lo + FF_CHUNK])
        v = _dot(u, wgu_ref[:, D_FF + lo:D_FF + lo + FF_CHUNK])
        act_ref[:, lo:lo + FF_CHUNK] = (g * _sigmoid(g) * v).astype(BF16)
    y = _dot(act_ref[...], wd_ref[...])
    o_ref[...] = x + 0.5 * _rms(y, g_ref[post:post + 1, :])


def _ffn(x, gains, w_gu, w_down, pre, post):
    t = x.shape[0]
    return pl.pallas_call(
        functools.partial(_ffn_kernel, pre, post),
        grid=(t // TOKEN_TILE,),
        in_specs=[
            pl.BlockSpec((TOKEN_TILE, D_MODEL), lambda i: (i, 0)),
            _const_spec((N_NORMS, D_MODEL)),
            _const_spec((D_MODEL, 2 * D_FF)),
            _const_spec((D_FF, D_MODEL)),
        ],
        out_specs=pl.BlockSpec((TOKEN_TILE, D_MODEL), lambda i: (i, 0)),
        out_shape=jax.ShapeDtypeStruct((t, D_MODEL), F32),
        scratch_shapes=[pltpu.VMEM((TOKEN_TILE, D_FF), BF16)],
        compiler_params=_params(("parallel",)),
        name="ffn",
    )(x, gains, w_gu, w_down)


IN_SPLITS = (MB_WIDTH, MB_WIDTH, MB_WIDTH, HG_WIDTH, HG_WIDTH, HG_WIDTH, HG_WIDTH, 2 * D_MODEL)
IN_COLS = sum(IN_SPLITS)


def _inproj_kernel(h_ref, g_ref, w_ref, *out_refs):
    u = _rms(h_ref[...], g_ref[2:3, :]).astype(BF16)
    off = 0
    for ref in out_refs:
        n = ref.shape[-1]
        for c in range(0, n, 512):
            ref[:, c:c + 512] = _dot(u, w_ref[:, off + c:off + c + 512])
        off += n


def _inproj(h, gains, w_in):
    t = h.shape[0]
    return pl.pallas_call(
        _inproj_kernel,
        grid=(t // TOKEN_TILE,),
        in_specs=[
            pl.BlockSpec((TOKEN_TILE, D_MODEL), lambda i: (i, 0)),
            _const_spec((N_NORMS, D_MODEL)),
            _const_spec((D_MODEL, IN_COLS)),
        ],
        out_specs=[pl.BlockSpec((TOKEN_TILE, n), lambda i: (i, 0)) for n in IN_SPLITS],
        out_shape=[jax.ShapeDtypeStruct((t, n), F32) for n in IN_SPLITS],
        compiler_params=_params(("parallel",)),
        name="in_proj",
    )(h, gains, w_in)


def _merge_kernel(h_ref, oa_ref, ob_ref, gg_ref, g_ref, wa_ref, wb_ref, wo_ref, o_ref):
    a = _dot(oa_ref[...].astype(BF16), wa_ref[...])
    b = _dot(ob_ref[...].astype(BF16), wb_ref[...])
    merged = _sigmoid(gg_ref[:, :D_MODEL]) * a + _sigmoid(gg_ref[:, D_MODEL:]) * b
    y = _dot(merged.astype(BF16), wo_ref[...])
    o_ref[...] = h_ref[...] + _rms(y, g_ref[3:4, :])


def _merge(h, o_a, o_b, gates, gains, w_a, w_b, w_out):
    t = h.shape[0]
    row = lambda n: pl.BlockSpec((TOKEN_TILE, n), lambda i: (i, 0))
    return pl.pallas_call(
        _merge_kernel,
        grid=(t // TOKEN_TILE,),
        in_specs=[
            row(D_MODEL), row(MB_WIDTH), row(HG_WIDTH), row(2 * D_MODEL),
            _const_spec((N_NORMS, D_MODEL)),
            _const_spec((MB_WIDTH, D_MODEL)),
            _const_spec((HG_WIDTH, D_MODEL)),
            _const_spec((D_MODEL, D_MODEL)),
        ],
        out_specs=row(D_MODEL),
        out_shape=jax.ShapeDtypeStruct((t, D_MODEL), F32),
        compiler_params=_params(("parallel",)),
        name="merge",
    )(h, o_a, o_b, gates, gains, w_a, w_b, w_out)


def _ple_kernel(h_ref, p_ref, g_ref, wp_ref, wg_ref, o_ref):
    h = h_ref[...]
    e = _dot(p_ref[...].astype(BF16), wp_ref[...])
    t = _dot(h.astype(BF16), wg_ref[...])
    o_ref[...] = h + _rms(_sigmoid(t) * e, g_ref[6:7, :])


def _ple(h, p, gains, w_ple, w_ple_gate):
    t = h.shape[0]
    row = lambda n: pl.BlockSpec((TOKEN_TILE, n), lambda i: (i, 0))
    return pl.pallas_call(
        _ple_kernel,
        grid=(t // TOKEN_TILE,),
        in_specs=[
            row(D_MODEL), row(PLE_DIM),
            _const_spec((N_NORMS, D_MODEL)),
            _const_spec((PLE_DIM, D_MODEL)),
            _const_spec((D_MODEL, D_MODEL)),
        ],
        out_specs=row(D_MODEL),
        out_shape=jax.ShapeDtypeStruct((t, D_MODEL), F32),
        compiler_params=_params(("parallel",)),
        name="ple",
    )(h, p, gains, w_ple, w_ple_gate)


def _bucket_starts():
    n = np.arange(0, 2 * MB_BLOCK, dtype=np.int32)
    nf = np.maximum(n, 1).astype(np.float32)
    large = REL_MAX_EXACT + (
        np.log(nf / np.float32(REL_MAX_EXACT)) / np.float32(math.log(REL_MAX_DIST / REL_MAX_EXACT))
        * np.float32(REL_BUCKETS - REL_MAX_EXACT)).astype(np.int32)
    large = np.minimum(large, REL_BUCKETS - 1)
    bucket = np.where(n < REL_MAX_EXACT, n, large)
    assert np.all(np.diff(bucket) >= 0) and bucket[-1] == REL_BUCKETS - 1
    return [int(np.argmax(bucket >= b)) for b in range(REL_BUCKETS)]


def _moba_kernel(starts, rel_ref, q_ref, k_ref, v_ref, o_ref,
                 kb_ref, vt_ref, kmean_ref, bown_ref, bprev_ref, sel_ref, qs_ref,
                 m_ref, acc_ref, sb0_ref, sb1_ref, mb0_ref, mb1_ref):
    pair = pl.program_id(1)
    step = pl.program_id(2)
    seq = k_ref.shape[1]
    nb = seq // MB_BLOCK
    hd = MB_HEAD_DIM
    qscale = (hd ** -0.5) * LOG2E

    @pl.when(step == 0)
    def _prepare():
        kb_ref[0:MB_BLOCK, :] = jnp.zeros((MB_BLOCK, LANES), BF16)
        kb_ref[MB_BLOCK + seq:, :] = jnp.zeros((PAD_END, LANES), BF16)
        sel_ref[:, nb:, :] = jnp.full((2, PAD_BLOCKS, MB_QTILE), NEG, F32)
        for hh in range(2):
            vt_ref[hh, :, 0:MB_BLOCK] = jnp.zeros((hd + 16, MB_BLOCK), BF16)
            vt_ref[hh, :, MB_BLOCK + seq:] = jnp.zeros((hd + 16, PAD_END), BF16)
            vt_ref[hh, hd:hd + 16, MB_BLOCK:MB_BLOCK + seq] = jnp.ones((16, seq), BF16)

        def blk(j, carry):
            r0 = pl.multiple_of(j * MB_BLOCK, MB_BLOCK)
            r1 = pl.multiple_of(r0 + MB_BLOCK, MB_BLOCK)
            kblk = k_ref[0, pl.ds(r0, MB_BLOCK), :]
            kb_ref[pl.ds(r1, MB_BLOCK), :] = kblk.astype(BF16)
            kmean_ref[pl.ds(j, 1), :] = jnp.mean(kblk, axis=0, keepdims=True)
            vt = v_ref[0, pl.ds(r0, MB_BLOCK), :].T.astype(BF16)
            vt_ref[0, 0:hd, pl.ds(r1, MB_BLOCK)] = vt[0:hd]
            vt_ref[1, 0:hd, pl.ds(r1, MB_BLOCK)] = vt[hd:2 * hd]
            return carry

        lax.fori_loop(0, nb, blk, 0)

        kk = lax.broadcasted_iota(jnp.int32, (MB_BLOCK, MB_BLOCK), 0)
        qq = lax.broadcasted_iota(jnp.int32, (MB_BLOCK, MB_BLOCK), 1)
        d_own = qq - kk
        d_prev = d_own + MB_BLOCK
        for hh in range(2):
            head = pair * 2 + hh
            b_own = jnp.full((MB_BLOCK, MB_BLOCK), rel_ref[0, head], F32)
            b_prev = jnp.full((MB_BLOCK, MB_BLOCK), rel_ref[0, head], F32)
            for b in range(1, REL_BUCKETS):
                val = rel_ref[b, head]
                b_own = jnp.where(d_own >= starts[b], val, b_own)
                b_prev = jnp.where(d_prev >= starts[b], val, b_prev)
            bown_ref[hh] = jnp.where(d_own >= 0, b_own * LOG2E, NEG)
            bprev_ref[hh] = b_prev * LOG2E

    nq = MB_QTILE // MB_BLOCK
    qb0 = step * nq
    c_far = [rel_ref[REL_BUCKETS - 1, pair * 2 + hh] * LOG2E for hh in range(2)]

    km_hi, km_mid, km_lo = _split3(kmean_ref[...])
    qt = q_ref[0].T
    row_head = lax.broadcasted_iota(jnp.int32, (2 * hd, MB_QTILE), 0) // hd
    blk_row = lax.broadcasted_iota(jnp.int32, (nb, MB_QTILE), 0)
    own_blk = qb0 + lax.broadcasted_iota(jnp.int32, (nb, MB_QTILE), 1) // MB_BLOCK
    for hh in range(2):
        qm = jnp.where(row_head == hh, qt, 0.0)
        qs = (qm * qscale).astype(BF16)
        for qi in range(nq):
            c0 = (2 * qi + hh) * MB_BLOCK
            qs_ref[:, c0:c0 + MB_BLOCK] = qs[:, qi * MB_BLOCK:(qi + 1) * MB_BLOCK]
        q_hi, q_mid, q_lo = _split3(qm)
        gate = (_dot(km_hi, q_hi) + (_dot(km_hi, q_mid) + _dot(km_mid, q_hi))
                + (_dot(km_hi, q_lo) + _dot(km_mid, q_mid) + _dot(km_lo, q_hi)))
        gate = jnp.where(blk_row < own_blk, gate, -jnp.inf)
        sel = jnp.full((nb, MB_QTILE), NEG, F32)
        for _ in range(MB_TOPK):
            mx = jnp.max(gate, axis=0, keepdims=True)
            cand = jnp.where((gate == mx) & (mx > -jnp.inf), blk_row, nb)
            pick = blk_row == jnp.min(cand, axis=0, keepdims=True)
            sel = jnp.where(pick, 0.0, sel)
            gate = jnp.where(pick, -jnp.inf, gate)
        sel_ref[hh, 0:nb, :] = sel

    def update(chains, kpos, biases, first):
        c_lo = chains[0][0] * 2 + chains[0][1]
        s_all = _dot(kb_ref[kpos, :],
                     qs_ref[:, c_lo * MB_BLOCK:(c_lo + len(chains)) * MB_BLOCK])
        for n, (qi, hh) in enumerate(chains):
            c = qi * 2 + hh
            s = jnp.concatenate(
                [s_all[u * MB_BLOCK:(u + 1) * MB_BLOCK, n * MB_BLOCK:(n + 1) * MB_BLOCK] + bias
                 for u, bias in enumerate(biases[n])], axis=0)
            m_blk = jnp.max(s, axis=0, keepdims=True).astype(BF16).astype(F32)
            m_old = None if first else m_ref[c:c + 1, :]
            m_new = m_blk if first else jnp.maximum(m_old, m_blk)
            pv = _dot(vt_ref[hh, :, kpos], jnp.exp2(s.astype(BF16) - m_new.astype(BF16)))
            acc_ref[c] = pv if first else acc_ref[c] * jnp.exp2(m_old - m_new) + pv
            m_ref[c:c + 1, :] = m_new

    def far_row(hh, qi, j, limit):
        row = sel_ref[hh, pl.ds(jnp.maximum(j, 0), 1), qi * MB_BLOCK:(qi + 1) * MB_BLOCK]
        return jnp.where((j >= 0) & (j < limit), row + c_far[hh], NEG)

    for qi in range(nq):
        qb = qb0 + qi
        kpos = pl.ds(pl.multiple_of(qb0 * MB_BLOCK, MB_BLOCK), (qi + 2) * MB_BLOCK)
        biases = []
        for hh in range(2):
            far = [far_row(hh, qi, qb0 - 1 + u, qb - 1) for u in range(qi)]
            prev = sel_ref[hh, pl.ds(jnp.maximum(qb - 1, 0), 1), qi * MB_BLOCK:(qi + 1) * MB_BLOCK]
            prev = bprev_ref[hh] + jnp.where(qb >= 1, prev, NEG)
            biases.append(far + [prev, bown_ref[hh]])
        update([(qi, 0), (qi, 1)], kpos, biases, True)

    nfar = qb0 - 1
    chains = [(qi, hh) for qi in range(nq) for hh in range(2)]

    def far_kpos(g):
        return pl.ds(pl.multiple_of((g * FAR_GROUP + 1) * MB_BLOCK, MB_BLOCK),
                     FAR_GROUP * MB_BLOCK)

    def score(g, qi, sb, mb):
        cols = slice(2 * qi * MB_BLOCK, 2 * (qi + 1) * MB_BLOCK)
        s_all = _dot(kb_ref[far_kpos(g), :], qs_ref[:, cols])
        for hh in range(2):
            c = 2 * qi + hh
            s = jnp.concatenate(
                [s_all[u * MB_BLOCK:(u + 1) * MB_BLOCK, hh * MB_BLOCK:(hh + 1) * MB_BLOCK]
                 + far_row(hh, qi, g * FAR_GROUP + u, nfar) for u in range(FAR_GROUP)], axis=0)
            mb[c:c + 1, :] = jnp.max(s, axis=0, keepdims=True).astype(BF16).astype(F32)
            sb[:, c * MB_BLOCK:(c + 1) * MB_BLOCK] = s.astype(BF16)

    def consume(g, qi, sb, mb):
        kpos = far_kpos(g)
        for hh in range(2):
            c = 2 * qi + hh
            m_old = m_ref[c:c + 1, :]
            m_new = jnp.maximum(m_old, mb[c:c + 1, :])
            p = jnp.exp2(sb[:, c * MB_BLOCK:(c + 1) * MB_BLOCK] - m_new.astype(BF16))
            acc_ref[c] = acc_ref[c] * jnp.exp2(m_old - m_new) + _dot(vt_ref[hh, :, kpos], p)
            m_ref[c:c + 1, :] = m_new

    for qi in range(nq):
        score(0, qi, sb0_ref, mb0_ref)

    def two_groups(t, carry):
        for qi in range(nq):
            consume(2 * t, qi, sb0_ref, mb0_ref)
            score(2 * t + 1, qi, sb1_ref, mb1_ref)
        for qi in range(nq):
            consume(2 * t + 1, qi, sb1_ref, mb1_ref)
            score(2 * t + 2, qi, sb0_ref, mb0_ref)
        return carry

    lax.fori_loop(0, (jnp.maximum(nfar, 0) + 2 * FAR_GROUP - 1) // (2 * FAR_GROUP),
                  two_groups, 0)

    for qi in range(nq):
        outs = [acc_ref[2 * qi + hh, 0:hd, :] / acc_ref[2 * qi + hh, hd:hd + 1, :]
                for hh in range(2)]
        o_ref[0, qi * MB_BLOCK:(qi + 1) * MB_BLOCK, :] = jnp.concatenate(outs, axis=0).T


def _moba(q, k, v, rel_table):
    bsz, seq, _ = q.shape
    assert seq % MB_QTILE == 0
    nb = seq // MB_BLOCK
    npair = MB_HEADS // 2
    padded = MB_BLOCK + seq + PAD_END
    return pl.pallas_call(
        functools.partial(_moba_kernel, _bucket_starts()),
        grid=(bsz, npair, seq // MB_QTILE),
        in_specs=[
            pl.BlockSpec(memory_space=pltpu.SMEM),
            pl.BlockSpec((1, MB_QTILE, LANES), lambda b, p, t: (b, t, p)),
            pl.BlockSpec((1, seq, LANES), lambda b, p, t: (b, 0, p)),
            pl.BlockSpec((1, seq, LANES), lambda b, p, t: (b, 0, p)),
        ],
        out_specs=pl.BlockSpec((1, MB_QTILE, LANES), lambda b, p, t: (b, t, p)),
        out_shape=jax.ShapeDtypeStruct((bsz, seq, MB_WIDTH), F32),
        scratch_shapes=[
            pltpu.VMEM((padded, LANES), BF16),
            pltpu.VMEM((2, MB_HEAD_DIM + 16, padded), BF16),
            pltpu.VMEM((nb, LANES), F32),
            pltpu.VMEM((2, MB_BLOCK, MB_BLOCK), F32),
            pltpu.VMEM((2, MB_BLOCK, MB_BLOCK), F32),
            pltpu.VMEM((2, nb + PAD_BLOCKS, MB_QTILE), F32),
            pltpu.VMEM((LANES, 2 * MB_QTILE), BF16),
            pltpu.VMEM((2 * MB_QTILE // MB_BLOCK, MB_BLOCK), F32),
            pltpu.VMEM((2 * MB_QTILE // MB_BLOCK, MB_HEAD_DIM + 16, MB_BLOCK), F32),
            pltpu.VMEM((FAR_GROUP * MB_BLOCK, 2 * MB_QTILE), BF16),
            pltpu.VMEM((FAR_GROUP * MB_BLOCK, 2 * MB_QTILE), BF16),
            pltpu.VMEM((2 * MB_QTILE // MB_BLOCK, MB_BLOCK), F32),
            pltpu.VMEM((2 * MB_QTILE // MB_BLOCK, MB_BLOCK), F32),
        ],
        compiler_params=_params(("parallel", "parallel", "arbitrary")),
        name="moba",
    )(rel_table, q, k, v)


def _segment_ref(b, n):
    c = b.shape[0]
    if 2 * n >= 8:
        b3 = b.reshape(c // (2 * n), 2 * n, b.shape[1])
        return jnp.broadcast_to(b3[:, n - 1:n, :], b3.shape).reshape(b.shape)
    pos = lax.broadcasted_iota(jnp.int32, b.shape, 0) % (2 * n)
    out = b
    for p in range(2 * n):
        shift = p - (n - 1)
        if shift != 0:
            out = jnp.where(pos == p, pltpu.roll(b, shift % c, 0), out)
    return out


def _hgrn_kernel(layer, hq_ref, hf_ref, hi_ref, hg_ref, lbp_ref, nw_ref, o_ref, state_ref):
    c = HG_CHUNK

    @pl.when(pl.program_id(1) == 0)
    def _reset():
        state_ref[...] = jnp.zeros_like(state_ref)

    lp = lbp_ref[...]
    e = jnp.exp(lp - jnp.max(lp, axis=0, keepdims=True))
    sm = e / jnp.sum(e, axis=0, keepdims=True)
    lb_all = sm[0:1, :]
    for r in range(1, layer + 1):
        lb_all = lb_all + sm[r:r + 1, :]

    row = lax.broadcasted_iota(jnp.int32, (c, c), 0)
    col = lax.broadcasted_iota(jnp.int32, (c, c), 1)
    tril = jnp.where(row >= col, 1.0, 0.0).astype(BF16)
    rpos = lax.broadcasted_iota(jnp.int32, (c, HG_DK), 0)
    levels = [1 << i for i in range(int(math.log2(c)) - 1, -1, -1)]

    for hh in range(HG_HEADS):
        sl = slice(hh * HG_DK, (hh + 1) * HG_DK)
        lb = lb_all[:, sl]
        xq = hq_ref[0, :, sl]
        f = lb + (1.0 - lb) * _sigmoid(hf_ref[0, :, sl])
        logf = jnp.log(f)
        kk = 1.0 - f
        q = xq * _sigmoid(xq)
        v = hi_ref[0, :, sl]
        vb = v.astype(BF16)

        l_hi, l_mid, l_lo = _split3(logf)
        b = _dot(tril, l_hi) + (_dot(tril, l_mid) + _dot(tril, l_lo))

        attn = jnp.zeros((c, c), F32)
        for n in levels:
            e_l = jnp.exp(-jnp.abs(b - _segment_ref(b, n)))
            second = (rpos & n) != 0
            q_l = jnp.where(second, q * e_l, 0.0).astype(BF16)
            k_l = jnp.where(second, 0.0, kk * e_l).astype(BF16)
            same_seg = (row // (2 * n)) == (col // (2 * n))
            attn = jnp.where(same_seg, attn + _dot_nt(q_l, k_l), attn)

        st = state_ref[hh]
        o = (_dot(attn.astype(BF16), vb)
             + jnp.sum(q * kk, axis=-1, keepdims=True) * v
             + _dot_nt((q * jnp.exp(b)).astype(BF16), st.astype(BF16)))
        b_last = b[c - 1:c, :]
        k_dec = (kk * jnp.exp(b_last - b)).astype(BF16)
        state_ref[hh] = st * jnp.exp(b_last) + lax.dot_general(
            vb, k_dec, (((0,), (0,)), ((), ())), preferred_element_type=F32)

        xg = hg_ref[0, :, sl]
        o_ref[0, :, sl] = _rms(o, nw_ref[:, :]) * (xg * _sigmoid(xg))


def _hgrn(hq, hf, hi, hg, lb_param, norm_w, layer):
    bsz, seq, _ = hq.shape
    assert seq % HG_CHUNK == 0
    blk = pl.BlockSpec((1, HG_CHUNK, HG_WIDTH), lambda b, t: (b, t, 0))
    return pl.pallas_call(
        functools.partial(_hgrn_kernel, layer),
        grid=(bsz, seq // HG_CHUNK),
        in_specs=[blk, blk, blk, blk,
                  pl.BlockSpec(lb_param.shape, lambda b, t: (0, 0)),
                  pl.BlockSpec((1, HG_DV), lambda b, t: (0, 0))],
        out_specs=blk,
        out_shape=jax.ShapeDtypeStruct((bsz, seq, HG_WIDTH), F32),
        scratch_shapes=[pltpu.VMEM((HG_HEADS, HG_DV, HG_DK), F32)],
        compiler_params=_params(("parallel", "arbitrary")),
        name="hgrn",
    )(hq, hf, hi, hg, lb_param, norm_w)


def kernel(x, p, w_ffn1_gu, w_ffn1_down, w_in, w_branch_a, w_branch_b, w_out,
           w_ffn2_gu, w_ffn2_down, w_ple, w_ple_gate, norm_gains, hg_norm_w,
           lb_param, rel_table):
    bsz, seq, _ = x.shape
    t = bsz * seq
    assert t % TOKEN_TILE == 0
    h = x.reshape(t, D_MODEL)
    for i in range(p.shape[0]):
        g = norm_gains[i]
        h = _ffn(h, g, w_ffn1_gu[i].astype(BF16), w_ffn1_down[i].astype(BF16), 0, 1)
        mq, mk, mv, hq, hf, hi, hg, gates = _inproj(h, g, w_in[i].astype(BF16))
        r3 = lambda a: a.reshape(bsz, seq, a.shape[-1])
        o_a = _moba(r3(mq), r3(mk), r3(mv), rel_table)
        o_b = _hgrn(r3(hq), r3(hf), r3(hi), r3(hg), lb_param, hg_norm_w[i:i + 1], i)
        h = _merge(h, o_a.reshape(t, MB_WIDTH), o_b.reshape(t, HG_WIDTH), gates, g,
                   w_branch_a[i].astype(BF16), w_branch_b[i].astype(BF16),
                   w_out[i].astype(BF16))
        h = _ffn(h, g, w_ffn2_gu[i].astype(BF16), w_ffn2_down[i].astype(BF16), 4, 5)
        h = _ple(h, p[i].reshape(t, PLE_DIM), g, w_ple[i].astype(BF16),
                 w_ple_gate[i].astype(BF16))
    return h.reshape(bsz, seq, D_MODEL)
```

```python
import functools
import math

import numpy as np
import jax
import jax.numpy as jnp
from jax import lax
from jax.experimental import pallas as pl
from jax.experimental.pallas import tpu as pltpu

F32 = jnp.float32
BF16 = jnp.bfloat16

D_MODEL = 1024
PLE_DIM = 256
D_FF = 2816
MB_HEADS = 8
MB_HEAD_DIM = 64
MB_WIDTH = MB_HEADS * MB_HEAD_DIM
MB_BLOCK = 256
MB_TOPK = 3
HG_HEADS = 4
HG_DK = 128
HG_DV = 128
HG_WIDTH = HG_HEADS * HG_DV
REL_BUCKETS = 32
REL_MAX_EXACT = REL_BUCKETS // 2
REL_MAX_DIST = 128
N_NORMS = 7
EPS = 1e-6

LANES = 128
SUBLANES = 8
VMEM_LIMIT = 56 * 1024 * 1024
TOKEN_TILE = 512
FF_CHUNK = 256
HG_CHUNK = 256
MB_QTILE = 1024
FAR_GROUP = 2
PAD_BLOCKS = 3 * FAR_GROUP
PAD_END = PAD_BLOCKS * 256
NEG = -1e30
LOG2E = 1.4426950408889634


def _rms(x, g):
    return x * lax.rsqrt(jnp.mean(x * x, axis=-1, keepdims=True) + EPS) * g


def _sigmoid(x):
    return 1.0 / (1.0 + jnp.exp(-x))


def _dot(a, b):
    return jnp.dot(a, b, preferred_element_type=F32)


def _dot_nt(a, b):
    return lax.dot_general(a, b, (((1,), (1,)), ((), ())), preferred_element_type=F32)


def _split3(x):
    hi = x.astype(BF16)
    r1 = x - hi.astype(F32)
    mid = r1.astype(BF16)
    lo = (r1 - mid.astype(F32)).astype(BF16)
    return hi, mid, lo


def _const_spec(shape):
    nd = len(shape)
    return pl.BlockSpec(shape, lambda *_: (0,) * nd, pipeline_mode=pl.Buffered(1))


def _params(sem, flags=None):
    return pltpu.CompilerParams(dimension_semantics=sem, vmem_limit_bytes=VMEM_LIMIT, flags=flags)


def _ffn_step(x, g_ref, pre, post, wgu_ref, wd_ref, act_ref):
    u = _rms(x, g_ref[pre:pre + 1, :]).astype(BF16)
    for c in range(D_FF // FF_CHUNK):
        lo = c * FF_CHUNK
        g = _dot(u, wgu_ref[:, lo:lo + FF_CHUNK])
        v = _dot(u, wgu_ref[:, D_FF + lo:D_FF + lo + FF_CHUNK])
        act_ref[:, lo:lo + FF_CHUNK] = (g * _sigmoid(g) * v).astype(BF16)
    y = _dot(act_ref[...], wd_ref[...])
    return x + 0.5 * _rms(y, g_ref[post:post + 1, :])


def _ffn_kernel(pre, post, x_ref, g_ref, wgu_ref, wd_ref, o_ref, act_ref):
    o_ref[...] = _ffn_step(x_ref[...], g_ref, pre, post, wgu_ref, wd_ref, act_ref)


def _ffn(x, gains, w_gu, w_down, pre, post):
    t = x.shape[0]
    return pl.pallas_call(
        functools.partial(_ffn_kernel, pre, post),
        grid=(t // TOKEN_TILE,),
        in_specs=[
            pl.BlockSpec((TOKEN_TILE, D_MODEL), lambda i: (i, 0)),
            _const_spec((N_NORMS, D_MODEL)),
            _const_spec((D_MODEL, 2 * D_FF)),
            _const_spec((D_FF, D_MODEL)),
        ],
        out_specs=pl.BlockSpec((TOKEN_TILE, D_MODEL), lambda i: (i, 0)),
        out_shape=jax.ShapeDtypeStruct((t, D_MODEL), F32),
        scratch_shapes=[pltpu.VMEM((TOKEN_TILE, D_FF), BF16)],
        compiler_params=_params(("parallel",)),
        name="ffn",
    )(x, gains, w_gu, w_down)


IN_SPLITS = (MB_WIDTH, MB_WIDTH, MB_WIDTH, HG_WIDTH, HG_WIDTH, HG_WIDTH, HG_WIDTH, 2 * D_MODEL)
IN_COLS = sum(IN_SPLITS)


HG_Q, HG_F, HG_G = 3, 4, 6


def _inproj_kernel(layer, h_ref, g_ref, lbp_ref, w_ref, *out_refs):
    u = _rms(h_ref[...], g_ref[2:3, :]).astype(BF16)
    lp = lbp_ref[...]
    e = jnp.exp(lp - jnp.max(lp, axis=0, keepdims=True))
    sm = e / jnp.sum(e, axis=0, keepdims=True)
    lb = sm[0:1, :]
    for r in range(1, layer + 1):
        lb = lb + sm[r:r + 1, :]
    off = 0
    for idx, ref in enumerate(out_refs):
        n = ref.shape[-1]
        for c in range(0, n, 512):
            z = _dot(u, w_ref[:, off + c:off + c + 512])
            if idx in (HG_Q, HG_G):
                z = z * _sigmoid(z)
            elif idx == HG_F:
                z = lb + (1.0 - lb) * _sigmoid(z)
            ref[:, c:c + 512] = z
        off += n


def _inproj(h, gains, lb_param, w_in, layer):
    t = h.shape[0]
    return pl.pallas_call(
        functools.partial(_inproj_kernel, layer),
        grid=(t // TOKEN_TILE,),
        in_specs=[
            pl.BlockSpec((TOKEN_TILE, D_MODEL), lambda i: (i, 0)),
            _const_spec((N_NORMS, D_MODEL)),
            _const_spec(lb_param.shape),
            _const_spec((D_MODEL, IN_COLS)),
        ],
        out_specs=[pl.BlockSpec((TOKEN_TILE, n), lambda i: (i, 0)) for n in IN_SPLITS],
        out_shape=[jax.ShapeDtypeStruct((t, n), F32) for n in IN_SPLITS],
        compiler_params=_params(("parallel",)),
        name="in_proj",
    )(h, gains, lb_param, w_in)


def _post_kernel(h_ref, oa_ref, ob_ref, gg_ref, p_ref, g_ref, wa_ref, wb_ref, wo_ref,
                 wgu_ref, wd_ref, wp_ref, wg_ref, o_ref, act_ref):
    a = _dot(oa_ref[...].astype(BF16), wa_ref[...])
    b = _dot(ob_ref[...].astype(BF16), wb_ref[...])
    merged = _sigmoid(gg_ref[:, :D_MODEL]) * a + _sigmoid(gg_ref[:, D_MODEL:]) * b
    h = h_ref[...] + _rms(_dot(merged.astype(BF16), wo_ref[...]), g_ref[3:4, :])
    h = _ffn_step(h, g_ref, 4, 5, wgu_ref, wd_ref, act_ref)
    e = _dot(p_ref[...].astype(BF16), wp_ref[...])
    t = _dot(h.astype(BF16), wg_ref[...])
    o_ref[...] = h + _rms(_sigmoid(t) * e, g_ref[6:7, :])


def _post(h, o_a, o_b, gates, p, gains, w_a, w_b, w_out, w_gu, w_down, w_ple, w_ple_gate):
    t = h.shape[0]
    row = lambda n: pl.BlockSpec((TOKEN_TILE, n), lambda i: (i, 0))
    return pl.pallas_call(
        _post_kernel,
        grid=(t // TOKEN_TILE,),
        in_specs=[
            row(D_MODEL), row(MB_WIDTH), row(HG_WIDTH), row(2 * D_MODEL), row(PLE_DIM),
            _const_spec((N_NORMS, D_MODEL)),
            _const_spec((MB_WIDTH, D_MODEL)),
            _const_spec((HG_WIDTH, D_MODEL)),
            _const_spec((D_MODEL, D_MODEL)),
            _const_spec((D_MODEL, 2 * D_FF)),
            _const_spec((D_FF, D_MODEL)),
            _const_spec((PLE_DIM, D_MODEL)),
            _const_spec((D_MODEL, D_MODEL)),
        ],
        out_specs=row(D_MODEL),
        out_shape=jax.ShapeDtypeStruct((t, D_MODEL), F32),
        scratch_shapes=[pltpu.VMEM((TOKEN_TILE, D_FF), BF16)],
        compiler_params=_params(("parallel",)),
        name="post",
    )(h, o_a, o_b, gates, p, gains, w_a, w_b, w_out, w_gu, w_down, w_ple, w_ple_gate)


def _bucket_starts():
    n = np.arange(0, 2 * MB_BLOCK, dtype=np.int32)
    nf = np.maximum(n, 1).astype(np.float32)
    large = REL_MAX_EXACT + (
        np.log(nf / np.float32(REL_MAX_EXACT)) / np.float32(math.log(REL_MAX_DIST / REL_MAX_EXACT))
        * np.float32(REL_BUCKETS - REL_MAX_EXACT)).astype(np.int32)
    large = np.minimum(large, REL_BUCKETS - 1)
    bucket = np.where(n < REL_MAX_EXACT, n, large)
    assert np.all(np.diff(bucket) >= 0) and bucket[-1] == REL_BUCKETS - 1
    return [int(np.argmax(bucket >= b)) for b in range(REL_BUCKETS)]


def _moba_kernel(starts, rel_ref, q_ref, k_ref, v_ref, o_ref,
                 kb_ref, vt_ref, kmean_ref, bown_ref, bprev_ref, sel_ref, qs_ref,
                 m_ref, acc_ref, sb0_ref, sb1_ref, mb0_ref, mb1_ref):
    pair = pl.program_id(1)
    step = pl.program_id(2)
    seq = k_ref.shape[1]
    nb = seq // MB_BLOCK
    hd = MB_HEAD_DIM
    qscale = (hd ** -0.5) * LOG2E

    @pl.when(step == 0)
    def _prepare():
        kb_ref[0:MB_BLOCK, :] = jnp.zeros((MB_BLOCK, LANES), BF16)
        kb_ref[MB_BLOCK + seq:, :] = jnp.zeros((PAD_END, LANES), BF16)
        sel_ref[:, nb:, :] = jnp.full((2, PAD_BLOCKS, MB_QTILE), NEG, F32)
        for hh in range(2):
            vt_ref[hh, :, 0:MB_BLOCK] = jnp.zeros((hd + 16, MB_BLOCK), BF16)
            vt_ref[hh, :, MB_BLOCK + seq:] = jnp.zeros((hd + 16, PAD_END), BF16)
            vt_ref[hh, hd:hd + 16, MB_BLOCK:MB_BLOCK + seq] = jnp.ones((16, seq), BF16)

        def blk(j, carry):
            r0 = pl.multiple_of(j * MB_BLOCK, MB_BLOCK)
            r1 = pl.multiple_of(r0 + MB_BLOCK, MB_BLOCK)
            kblk = k_ref[0, pl.ds(r0, MB_BLOCK), :]
            kb_ref[pl.ds(r1, MB_BLOCK), :] = kblk.astype(BF16)
            kmean_ref[pl.ds(j, 1), :] = jnp.mean(kblk, axis=0, keepdims=True)
            vt = v_ref[0, pl.ds(r0, MB_BLOCK), :].T.astype(BF16)
            vt_ref[0, 0:hd, pl.ds(r1, MB_BLOCK)] = vt[0:hd]
            vt_ref[1, 0:hd, pl.ds(r1, MB_BLOCK)] = vt[hd:2 * hd]
            return carry

        lax.fori_loop(0, nb, blk, 0)

        kk = lax.broadcasted_iota(jnp.int32, (MB_BLOCK, MB_BLOCK), 0)
        qq = lax.broadcasted_iota(jnp.int32, (MB_BLOCK, MB_BLOCK), 1)
        d_own = qq - kk
        d_prev = d_own + MB_BLOCK
        for hh in range(2):
            head = pair * 2 + hh
            b_own = jnp.full((MB_BLOCK, MB_BLOCK), rel_ref[0, head], F32)
            b_prev = jnp.full((MB_BLOCK, MB_BLOCK), rel_ref[0, head], F32)
            for b in range(1, REL_BUCKETS):
                val = rel_ref[b, head]
                b_own = jnp.where(d_own >= starts[b], val, b_own)
                b_prev = jnp.where(d_prev >= starts[b], val, b_prev)
            bown_ref[hh] = jnp.where(d_own >= 0, b_own * LOG2E, NEG)
            bprev_ref[hh] = b_prev * LOG2E

    nq = MB_QTILE // MB_BLOCK
    qb0 = step * nq
    c_far = [rel_ref[REL_BUCKETS - 1, pair * 2 + hh] * LOG2E for hh in range(2)]

    km_hi, km_mid, km_lo = _split3(kmean_ref[...])
    qt = q_ref[0].T
    row_head = lax.broadcasted_iota(jnp.int32, (2 * hd, MB_QTILE), 0) // hd
    blk_row = lax.broadcasted_iota(jnp.int32, (nb, MB_QTILE), 0)
    own_blk = qb0 + lax.broadcasted_iota(jnp.int32, (nb, MB_QTILE), 1) // MB_BLOCK
    for hh in range(2):
        qm = jnp.where(row_head == hh, qt, 0.0)
        qs = (qm * qscale).astype(BF16)
        for qi in range(nq):
            c0 = (2 * qi + hh) * MB_BLOCK
            qs_ref[:, c0:c0 + MB_BLOCK] = qs[:, qi * MB_BLOCK:(qi + 1) * MB_BLOCK]
        q_hi, q_mid, q_lo = _split3(qm)
        gate = (_dot(km_hi, q_hi) + (_dot(km_hi, q_mid) + _dot(km_mid, q_hi))
                + (_dot(km_hi, q_lo) + _dot(km_mid, q_mid) + _dot(km_lo, q_hi)))
        gate = jnp.where(blk_row < own_blk, gate, -jnp.inf)
        sel = jnp.full((nb, MB_QTILE), NEG, F32)
        for _ in range(MB_TOPK):
            mx = jnp.max(gate, axis=0, keepdims=True)
            cand = jnp.where((gate == mx) & (mx > -jnp.inf), blk_row, nb)
            pick = blk_row == jnp.min(cand, axis=0, keepdims=True)
            sel = jnp.where(pick, 0.0, sel)
            gate = jnp.where(pick, -jnp.inf, gate)
        sel_ref[hh, 0:nb, :] = sel

    def update(chains, kpos, biases, first):
        c_lo = chains[0][0] * 2 + chains[0][1]
        s_all = _dot(kb_ref[kpos, :],
                     qs_ref[:, c_lo * MB_BLOCK:(c_lo + len(chains)) * MB_BLOCK])
        for n, (qi, hh) in enumerate(chains):
            c = qi * 2 + hh
            s = jnp.concatenate(
                [s_all[u * MB_BLOCK:(u + 1) * MB_BLOCK, n * MB_BLOCK:(n + 1) * MB_BLOCK] + bias
                 for u, bias in enumerate(biases[n])], axis=0)
            m_blk = jnp.max(s, axis=0, keepdims=True).astype(BF16).astype(F32)
            m_old = None if first else m_ref[c:c + 1, :]
            m_new = m_blk if first else jnp.maximum(m_old, m_blk)
            pv = _dot(vt_ref[hh, :, kpos], jnp.exp2(s.astype(BF16) - m_new.astype(BF16)))
            acc_ref[c] = pv if first else acc_ref[c] * jnp.exp2(m_old - m_new) + pv
            m_ref[c:c + 1, :] = m_new

    def far_row(hh, qi, j, limit):
        row = sel_ref[hh, pl.ds(jnp.maximum(j, 0), 1), qi * MB_BLOCK:(qi + 1) * MB_BLOCK]
        return jnp.where((j >= 0) & (j < limit), row + c_far[hh], NEG)

    for qi in range(nq):
        qb = qb0 + qi
        kpos = pl.ds(pl.multiple_of(qb0 * MB_BLOCK, MB_BLOCK), (qi + 2) * MB_BLOCK)
        biases = []
        for hh in range(2):
            far = [far_row(hh, qi, qb0 - 1 + u, qb - 1) for u in range(qi)]
            prev = sel_ref[hh, pl.ds(jnp.maximum(qb - 1, 0), 1), qi * MB_BLOCK:(qi + 1) * MB_BLOCK]
            prev = bprev_ref[hh] + jnp.where(qb >= 1, prev, NEG)
            biases.append(far + [prev, bown_ref[hh]])
        update([(qi, 0), (qi, 1)], kpos, biases, True)

    nfar = qb0 - 1
    chains = [(qi, hh) for qi in range(nq) for hh in range(2)]

    def far_kpos(g):
        return pl.ds(pl.multiple_of((g * FAR_GROUP + 1) * MB_BLOCK, MB_BLOCK),
                     FAR_GROUP * MB_BLOCK)

    def score(g, qi, sb, mb):
        cols = slice(2 * qi * MB_BLOCK, 2 * (qi + 1) * MB_BLOCK)
        s_all = _dot(kb_ref[far_kpos(g), :], qs_ref[:, cols])
        for hh in range(2):
            c = 2 * qi + hh
            s = jnp.concatenate(
                [s_all[u * MB_BLOCK:(u + 1) * MB_BLOCK, hh * MB_BLOCK:(hh + 1) * MB_BLOCK]
                 + far_row(hh, qi, g * FAR_GROUP + u, nfar) for u in range(FAR_GROUP)], axis=0)
            mb[c:c + 1, :] = jnp.max(s, axis=0, keepdims=True).astype(BF16).astype(F32)
            sb[:, c * MB_BLOCK:(c + 1) * MB_BLOCK] = s.astype(BF16)

    def consume(g, qi, sb, mb):
        kpos = far_kpos(g)
        for hh in range(2):
            c = 2 * qi + hh
            m_old = m_ref[c:c + 1, :]
            m_new = jnp.maximum(m_old, mb[c:c + 1, :])
            p = jnp.exp2(sb[:, c * MB_BLOCK:(c + 1) * MB_BLOCK] - m_new.astype(BF16))
            acc_ref[c] = acc_ref[c] * jnp.exp2(m_old - m_new) + _dot(vt_ref[hh, :, kpos], p)
            m_ref[c:c + 1, :] = m_new

    for qi in range(nq):
        score(0, qi, sb0_ref, mb0_ref)

    def two_groups(t, carry):
        for qi in range(nq):
            consume(2 * t, qi, sb0_ref, mb0_ref)
            score(2 * t + 1, qi, sb1_ref, mb1_ref)
        for qi in range(nq):
            consume(2 * t + 1, qi, sb1_ref, mb1_ref)
            score(2 * t + 2, qi, sb0_ref, mb0_ref)
        return carry

    lax.fori_loop(0, (jnp.maximum(nfar, 0) + 2 * FAR_GROUP - 1) // (2 * FAR_GROUP),
                  two_groups, 0)

    for qi in range(nq):
        outs = [acc_ref[2 * qi + hh, 0:hd, :] / acc_ref[2 * qi + hh, hd:hd + 1, :]
                for hh in range(2)]
        o_ref[0, qi * MB_BLOCK:(qi + 1) * MB_BLOCK, :] = jnp.concatenate(outs, axis=0).T


def _moba(q, k, v, rel_table):
    bsz, seq, _ = q.shape
    assert seq % MB_QTILE == 0
    nb = seq // MB_BLOCK
    npair = MB_HEADS // 2
    padded = MB_BLOCK + seq + PAD_END
    return pl.pallas_call(
        functools.partial(_moba_kernel, _bucket_starts()),
        grid=(bsz, npair, seq // MB_QTILE),
        in_specs=[
            pl.BlockSpec(memory_space=pltpu.SMEM),
            pl.BlockSpec((1, MB_QTILE, LANES), lambda b, p, t: (b, t, p)),
            pl.BlockSpec((1, seq, LANES), lambda b, p, t: (b, 0, p)),
            pl.BlockSpec((1, seq, LANES), lambda b, p, t: (b, 0, p)),
        ],
        out_specs=pl.BlockSpec((1, MB_QTILE, LANES), lambda b, p, t: (b, t, p)),
        out_shape=jax.ShapeDtypeStruct((bsz, seq, MB_WIDTH), F32),
        scratch_shapes=[
            pltpu.VMEM((padded, LANES), BF16),
            pltpu.VMEM((2, MB_HEAD_DIM + 16, padded), BF16),
            pltpu.VMEM((nb, LANES), F32),
            pltpu.VMEM((2, MB_BLOCK, MB_BLOCK), F32),
            pltpu.VMEM((2, MB_BLOCK, MB_BLOCK), F32),
            pltpu.VMEM((2, nb + PAD_BLOCKS, MB_QTILE), F32),
            pltpu.VMEM((LANES, 2 * MB_QTILE), BF16),
            pltpu.VMEM((2 * MB_QTILE // MB_BLOCK, MB_BLOCK), F32),
            pltpu.VMEM((2 * MB_QTILE // MB_BLOCK, MB_HEAD_DIM + 16, MB_BLOCK), F32),
            pltpu.VMEM((FAR_GROUP * MB_BLOCK, 2 * MB_QTILE), BF16),
            pltpu.VMEM((FAR_GROUP * MB_BLOCK, 2 * MB_QTILE), BF16),
            pltpu.VMEM((2 * MB_QTILE // MB_BLOCK, MB_BLOCK), F32),
            pltpu.VMEM((2 * MB_QTILE // MB_BLOCK, MB_BLOCK), F32),
        ],
        compiler_params=_params(("parallel", "parallel", "arbitrary")),
        name="moba",
    )(rel_table, q, k, v)


def _segment_ref(b, n):
    c = b.shape[0]
    if 2 * n >= SUBLANES:
        b3 = b.reshape(c // (2 * n), 2 * n, b.shape[1])
        return jnp.broadcast_to(b3[:, n - 1:n, :], b3.shape).reshape(b.shape)
    b3 = b.reshape(c // SUBLANES, SUBLANES, b.shape[1])
    seg = lax.broadcasted_iota(jnp.int32, b3.shape, 1) // (2 * n)
    out = jnp.broadcast_to(b3[:, n - 1:n, :], b3.shape)
    for s in range(1, SUBLANES // (2 * n)):
        r = s * 2 * n + n - 1
        out = jnp.where(seg == s, jnp.broadcast_to(b3[:, r:r + 1, :], b3.shape), out)
    return out.reshape(b.shape)


def _hgrn_kernel(q_ref, f_ref, v_ref, g_ref, nw_ref, o_ref, state_ref):
    c = HG_CHUNK

    @pl.when(pl.program_id(1) == 0)
    def _reset():
        state_ref[...] = jnp.zeros_like(state_ref)

    half = c // 2
    row = lax.broadcasted_iota(jnp.int32, (c, c), 0)
    col = lax.broadcasted_iota(jnp.int32, (c, c), 1)
    tril = jnp.where(row >= col, 1.0, 0.0).astype(BF16)
    rh = lax.broadcasted_iota(jnp.int32, (half, half), 0)
    ch = lax.broadcasted_iota(jnp.int32, (half, half), 1)
    split = jnp.where(rh > ch, rh ^ ch, 0)
    levels = [1 << i for i in range(int(math.log2(half)))]

    for hh in range(HG_HEADS):
        sl = slice(hh * HG_DK, (hh + 1) * HG_DK)
        f = f_ref[0, :, sl]
        logf = jnp.log(f)
        kk = 1.0 - f
        q = q_ref[0, :, sl]
        v = v_ref[0, :, sl]
        vb = v.astype(BF16)
        qb = q.astype(BF16)
        kb = kk.astype(BF16)

        l_hi, l_mid, l_lo = _split3(logf * LOG2E)
        b = _dot(tril, l_hi) + (_dot(tril, l_mid) + _dot(tril, l_lo))

        def level_operands(n):
            if n == 1:
                return qb * f.astype(BF16), kb
            e_l = jnp.exp2(-jnp.abs(b - _segment_ref(b, n))).astype(BF16)
            return qb * e_l, kb * e_l

        d_lo = jnp.zeros((half, half), F32)
        d_hi = jnp.zeros((half, half), F32)
        for n in levels:
            q_l, k_l = level_operands(n)
            d_lo = jnp.where(split >= n, _dot_nt(q_l[:half], k_l[:half]), d_lo)
            d_hi = jnp.where(split >= n, _dot_nt(q_l[half:], k_l[half:]), d_hi)
        q_l, k_l = level_operands(half)
        cross = _dot_nt(q_l[half:], k_l[:half])

        st = state_ref[hh]
        intra = jnp.concatenate(
            [_dot(d_lo.astype(BF16), vb[:half]),
             _dot(jnp.concatenate([cross, d_hi], axis=1).astype(BF16), vb)], axis=0)
        o = (intra + jnp.sum(q * kk, axis=-1, keepdims=True) * v
             + _dot_nt((q * jnp.exp2(b)).astype(BF16), st.astype(BF16)))
        b_last = b[c - 1:c, :]
        k_dec = (kk * jnp.exp2(b_last - b)).astype(BF16)
        state_ref[hh] = st * jnp.exp2(b_last) + lax.dot_general(
            vb, k_dec, (((0,), (0,)), ((), ())), preferred_element_type=F32)

        o_ref[0, :, sl] = _rms(o, nw_ref[:, :]) * g_ref[0, :, sl]


def _hgrn(q, f, v, g, norm_w):
    bsz, seq, _ = q.shape
    assert seq % HG_CHUNK == 0
    blk = pl.BlockSpec((1, HG_CHUNK, HG_WIDTH), lambda b, t: (b, t, 0))
    return pl.pallas_call(
        _hgrn_kernel,
        grid=(bsz, seq // HG_CHUNK),
        in_specs=[blk, blk, blk, blk, pl.BlockSpec((1, HG_DV), lambda b, t: (0, 0))],
        out_specs=blk,
        out_shape=jax.ShapeDtypeStruct((bsz, seq, HG_WIDTH), F32),
        scratch_shapes=[pltpu.VMEM((HG_HEADS, HG_DV, HG_DK), F32)],
        compiler_params=_params(("parallel", "arbitrary")),
        name="hgrn",
    )(q, f, v, g, norm_w)


def kernel(x, p, w_ffn1_gu, w_ffn1_down, w_in, w_branch_a, w_branch_b, w_out,
           w_ffn2_gu, w_ffn2_down, w_ple, w_ple_gate, norm_gains, hg_norm_w,
           lb_param, rel_table):
    bsz, seq, _ = x.shape
    t = bsz * seq
    assert t % TOKEN_TILE == 0
    h = x.reshape(t, D_MODEL)
    for i in range(p.shape[0]):
        g = norm_gains[i]
        h = _ffn(h, g, w_ffn1_gu[i].astype(BF16), w_ffn1_down[i].astype(BF16), 0, 1)
        mq, mk, mv, hq, hf, hi, hg, gates = _inproj(h, g, lb_param, w_in[i].astype(BF16), i)
        r3 = lambda a: a.reshape(bsz, seq, a.shape[-1])
        o_a = _moba(r3(mq), r3(mk), r3(mv), rel_table)
        o_b = _hgrn(r3(hq), r3(hf), r3(hi), r3(hg), hg_norm_w[i:i + 1])
        h = _post(h, o_a.reshape(t, MB_WIDTH), o_b.reshape(t, HG_WIDTH), gates,
                  p[i].reshape(t, PLE_DIM), g,
                  w_branch_a[i].astype(BF16), w_branch_b[i].astype(BF16), w_out[i].astype(BF16),
                  w_ffn2_gu[i].astype(BF16), w_ffn2_down[i].astype(BF16),
                  w_ple[i].astype(BF16), w_ple_gate[i].astype(BF16))
    return h.reshape(bsz, seq, D_MODEL)
```

```python
import functools
import math

import numpy as np
import jax
import jax.numpy as jnp
from jax import lax
from jax.experimental import pallas as pl
from jax.experimental.pallas import tpu as pltpu

F32 = jnp.float32
BF16 = jnp.bfloat16

D_MODEL = 1024
PLE_DIM = 256
D_FF = 2816
MB_HEADS = 8
MB_HEAD_DIM = 64
MB_WIDTH = MB_HEADS * MB_HEAD_DIM
MB_BLOCK = 256
MB_TOPK = 3
HG_HEADS = 4
HG_DK = 128
HG_DV = 128
HG_WIDTH = HG_HEADS * HG_DV
REL_BUCKETS = 32
REL_MAX_EXACT = REL_BUCKETS // 2
REL_MAX_DIST = 128
N_NORMS = 7
EPS = 1e-6

LANES = 128
SUBLANES = 8
VMEM_LIMIT = 56 * 1024 * 1024
TOKEN_TILE = 512
FF_CHUNK = 256
HG_CHUNK = 256
MB_QTILE = 1024
FAR_GROUP = 2
PAD_BLOCKS = 3 * FAR_GROUP
PAD_END = PAD_BLOCKS * 256
NEG = -1e30
LOG2E = 1.4426950408889634


def _rms(x, g):
    return x * lax.rsqrt(jnp.mean(x * x, axis=-1, keepdims=True) + EPS) * g


def _sigmoid(x):
    return 1.0 / (1.0 + jnp.exp(-x))


def _dot(a, b):
    return jnp.dot(a, b, preferred_element_type=F32)


def _dot_nt(a, b):
    return lax.dot_general(a, b, (((1,), (1,)), ((), ())), preferred_element_type=F32)


def _split3(x):
    def top(y):
        bits = pltpu.bitcast(y, jnp.uint32) & jnp.uint32(0xFFFF0000)
        return pltpu.bitcast(bits, F32)
    hi = top(x)
    r1 = x - hi
    mid = top(r1)
    return hi.astype(BF16), mid.astype(BF16), (r1 - mid).astype(BF16)


def _const_spec(shape):
    nd = len(shape)
    return pl.BlockSpec(shape, lambda *_: (0,) * nd, pipeline_mode=pl.Buffered(1))


def _params(sem, flags=None):
    return pltpu.CompilerParams(dimension_semantics=sem, vmem_limit_bytes=VMEM_LIMIT, flags=flags)


def _ffn_step(x, g_ref, pre, post, wgu_ref, wd_ref, act_ref):
    u = _rms(x, g_ref[pre:pre + 1, :]).astype(BF16)
    for c in range(D_FF // FF_CHUNK):
        lo = c * FF_CHUNK
        g = _dot(u, wgu_ref[:, lo:lo + FF_CHUNK])
        v = _dot(u, wgu_ref[:, D_FF + lo:D_FF + lo + FF_CHUNK])
        act_ref[:, lo:lo + FF_CHUNK] = (g * _sigmoid(g) * v).astype(BF16)
    y = _dot(act_ref[...], wd_ref[...])
    return x + 0.5 * _rms(y, g_ref[post:post + 1, :])


def _ffn_kernel(pre, post, x_ref, g_ref, wgu_ref, wd_ref, o_ref, act_ref):
    o_ref[...] = _ffn_step(x_ref[...], g_ref, pre, post, wgu_ref, wd_ref, act_ref)


def _ffn(x, gains, w_gu, w_down, pre, post):
    t = x.shape[0]
    return pl.pallas_call(
        functools.partial(_ffn_kernel, pre, post),
        grid=(t // TOKEN_TILE,),
        in_specs=[
            pl.BlockSpec((TOKEN_TILE, D_MODEL), lambda i: (i, 0)),
            _const_spec((N_NORMS, D_MODEL)),
            _const_spec((D_MODEL, 2 * D_FF)),
            _const_spec((D_FF, D_MODEL)),
        ],
        out_specs=pl.BlockSpec((TOKEN_TILE, D_MODEL), lambda i: (i, 0)),
        out_shape=jax.ShapeDtypeStruct((t, D_MODEL), F32),
        scratch_shapes=[pltpu.VMEM((TOKEN_TILE, D_FF), BF16)],
        compiler_params=_params(("parallel",)),
        name="ffn",
    )(x, gains, w_gu, w_down)


IN_SPLITS = (MB_WIDTH, MB_WIDTH, MB_WIDTH, HG_WIDTH, HG_WIDTH, HG_WIDTH, HG_WIDTH, 2 * D_MODEL)
IN_COLS = sum(IN_SPLITS)


HG_Q, HG_F, HG_G = 3, 4, 6


def _inproj_kernel(layer, h_ref, g_ref, lbp_ref, w_ref, *out_refs):
    u = _rms(h_ref[...], g_ref[2:3, :]).astype(BF16)
    lp = lbp_ref[...]
    e = jnp.exp(lp - jnp.max(lp, axis=0, keepdims=True))
    sm = e / jnp.sum(e, axis=0, keepdims=True)
    lb = sm[0:1, :]
    for r in range(1, layer + 1):
        lb = lb + sm[r:r + 1, :]
    off = 0
    for idx, ref in enumerate(out_refs):
        n = ref.shape[-1]
        for c in range(0, n, 512):
            z = _dot(u, w_ref[:, off + c:off + c + 512])
            if idx in (HG_Q, HG_G):
                z = z * _sigmoid(z)
            elif idx == HG_F:
                z = lb + (1.0 - lb) * _sigmoid(z)
            ref[:, c:c + 512] = z
        off += n


def _inproj(h, gains, lb_param, w_in, layer):
    t = h.shape[0]
    return pl.pallas_call(
        functools.partial(_inproj_kernel, layer),
        grid=(t // TOKEN_TILE,),
        in_specs=[
            pl.BlockSpec((TOKEN_TILE, D_MODEL), lambda i: (i, 0)),
            _const_spec((N_NORMS, D_MODEL)),
            _const_spec(lb_param.shape),
            _const_spec((D_MODEL, IN_COLS)),
        ],
        out_specs=[pl.BlockSpec((TOKEN_TILE, n), lambda i: (i, 0)) for n in IN_SPLITS],
        out_shape=[jax.ShapeDtypeStruct((t, n), F32) for n in IN_SPLITS],
        compiler_params=_params(("parallel",)),
        name="in_proj",
    )(h, gains, lb_param, w_in)


def _post_kernel(h_ref, oa_ref, ob_ref, gg_ref, p_ref, g_ref, wa_ref, wb_ref, wo_ref,
                 wgu_ref, wd_ref, wp_ref, wg_ref, o_ref, act_ref):
    a = _dot(oa_ref[...].astype(BF16), wa_ref[...])
    b = _dot(ob_ref[...].astype(BF16), wb_ref[...])
    merged = _sigmoid(gg_ref[:, :D_MODEL]) * a + _sigmoid(gg_ref[:, D_MODEL:]) * b
    h = h_ref[...] + _rms(_dot(merged.astype(BF16), wo_ref[...]), g_ref[3:4, :])
    h = _ffn_step(h, g_ref, 4, 5, wgu_ref, wd_ref, act_ref)
    e = _dot(p_ref[...].astype(BF16), wp_ref[...])
    t = _dot(h.astype(BF16), wg_ref[...])
    o_ref[...] = h + _rms(_sigmoid(t) * e, g_ref[6:7, :])


def _post(h, o_a, o_b, gates, p, gains, w_a, w_b, w_out, w_gu, w_down, w_ple, w_ple_gate):
    t = h.shape[0]
    row = lambda n: pl.BlockSpec((TOKEN_TILE, n), lambda i: (i, 0))
    return pl.pallas_call(
        _post_kernel,
        grid=(t // TOKEN_TILE,),
        in_specs=[
            row(D_MODEL), row(MB_WIDTH), row(HG_WIDTH), row(2 * D_MODEL), row(PLE_DIM),
            _const_spec((N_NORMS, D_MODEL)),
            _const_spec((MB_WIDTH, D_MODEL)),
            _const_spec((HG_WIDTH, D_MODEL)),
            _const_spec((D_MODEL, D_MODEL)),
            _const_spec((D_MODEL, 2 * D_FF)),
            _const_spec((D_FF, D_MODEL)),
            _const_spec((PLE_DIM, D_MODEL)),
            _const_spec((D_MODEL, D_MODEL)),
        ],
        out_specs=row(D_MODEL),
        out_shape=jax.ShapeDtypeStruct((t, D_MODEL), F32),
        scratch_shapes=[pltpu.VMEM((TOKEN_TILE, D_FF), BF16)],
        compiler_params=_params(("parallel",)),
        name="post",
    )(h, o_a, o_b, gates, p, gains, w_a, w_b, w_out, w_gu, w_down, w_ple, w_ple_gate)


def _bucket_starts():
    n = np.arange(0, 2 * MB_BLOCK, dtype=np.int32)
    nf = np.maximum(n, 1).astype(np.float32)
    large = REL_MAX_EXACT + (
        np.log(nf / np.float32(REL_MAX_EXACT)) / np.float32(math.log(REL_MAX_DIST / REL_MAX_EXACT))
        * np.float32(REL_BUCKETS - REL_MAX_EXACT)).astype(np.int32)
    large = np.minimum(large, REL_BUCKETS - 1)
    bucket = np.where(n < REL_MAX_EXACT, n, large)
    assert np.all(np.diff(bucket) >= 0) and bucket[-1] == REL_BUCKETS - 1
    return [int(np.argmax(bucket >= b)) for b in range(REL_BUCKETS)]


def _moba_kernel(starts, rel_ref, q_ref, k_ref, v_ref, o_ref,
                 kb_ref, vt_ref, kmean_ref, bown_ref, bprev_ref, sel_ref, qs_ref,
                 m_ref, acc_ref, sb0_ref, sb1_ref, mb0_ref, mb1_ref, sn_ref, mbn_ref):
    pair = pl.program_id(1)
    step = pl.program_id(2)
    seq = k_ref.shape[1]
    nb = seq // MB_BLOCK
    hd = MB_HEAD_DIM
    qscale = (hd ** -0.5) * LOG2E

    @pl.when(step == 0)
    def _prepare():
        kb_ref[0:MB_BLOCK, :] = jnp.zeros((MB_BLOCK, LANES), BF16)
        kb_ref[MB_BLOCK + seq:, :] = jnp.zeros((PAD_END, LANES), BF16)
        sel_ref[:, nb:, :] = jnp.full((2, PAD_BLOCKS, MB_QTILE), NEG, F32)
        for hh in range(2):
            vt_ref[hh, :, 0:MB_BLOCK] = jnp.zeros((hd + 16, MB_BLOCK), BF16)
            vt_ref[hh, :, MB_BLOCK + seq:] = jnp.zeros((hd + 16, PAD_END), BF16)
            vt_ref[hh, hd:hd + 16, MB_BLOCK:MB_BLOCK + seq] = jnp.ones((16, seq), BF16)

        def blk(j, carry):
            r0 = pl.multiple_of(j * MB_BLOCK, MB_BLOCK)
            r1 = pl.multiple_of(r0 + MB_BLOCK, MB_BLOCK)
            kblk = k_ref[0, pl.ds(r0, MB_BLOCK), :]
            kb_ref[pl.ds(r1, MB_BLOCK), :] = kblk.astype(BF16)
            kmean_ref[pl.ds(j, 1), :] = jnp.mean(kblk, axis=0, keepdims=True)
            vt = v_ref[0, pl.ds(r0, MB_BLOCK), :].T.astype(BF16)
            vt_ref[0, 0:hd, pl.ds(r1, MB_BLOCK)] = vt[0:hd]
            vt_ref[1, 0:hd, pl.ds(r1, MB_BLOCK)] = vt[hd:2 * hd]
            return carry

        lax.fori_loop(0, nb, blk, 0)

        kk = lax.broadcasted_iota(jnp.int32, (MB_BLOCK, MB_BLOCK), 0)
        qq = lax.broadcasted_iota(jnp.int32, (MB_BLOCK, MB_BLOCK), 1)
        d_own = qq - kk
        d_prev = d_own + MB_BLOCK
        for hh in range(2):
            head = pair * 2 + hh
            b_own = jnp.full((MB_BLOCK, MB_BLOCK), rel_ref[0, head], F32)
            b_prev = jnp.full((MB_BLOCK, MB_BLOCK), rel_ref[0, head], F32)
            for b in range(1, REL_BUCKETS):
                val = rel_ref[b, head]
                b_own = jnp.where(d_own >= starts[b], val, b_own)
                b_prev = jnp.where(d_prev >= starts[b], val, b_prev)
            bown_ref[hh] = jnp.where(d_own >= 0, b_own * LOG2E, NEG)
            bprev_ref[hh] = b_prev * LOG2E

    nq = MB_QTILE // MB_BLOCK
    qb0 = step * nq
    c_far = [rel_ref[REL_BUCKETS - 1, pair * 2 + hh] * LOG2E for hh in range(2)]

    km_hi, km_mid, km_lo = _split3(kmean_ref[...])
    qt = q_ref[0].T
    row_head = lax.broadcasted_iota(jnp.int32, (2 * hd, MB_QTILE), 0) // hd
    blk_row = lax.broadcasted_iota(jnp.int32, (nb, MB_QTILE), 0)
    own_blk = qb0 + lax.broadcasted_iota(jnp.int32, (nb, MB_QTILE), 1) // MB_BLOCK
    for hh in range(2):
        qm = jnp.where(row_head == hh, qt, 0.0)
        qs = (qm * qscale).astype(BF16)
        for qi in range(nq):
            c0 = (2 * qi + hh) * MB_BLOCK
            qs_ref[:, c0:c0 + MB_BLOCK] = qs[:, qi * MB_BLOCK:(qi + 1) * MB_BLOCK]
        q_hi, q_mid, q_lo = _split3(qm)
        gate = (_dot(km_hi, q_hi) + (_dot(km_hi, q_mid) + _dot(km_mid, q_hi))
                + (_dot(km_hi, q_lo) + _dot(km_mid, q_mid) + _dot(km_lo, q_hi)))
        gate = jnp.where(blk_row < own_blk, gate, -jnp.inf)
        sel = jnp.full((nb, MB_QTILE), NEG, F32)
        for _ in range(MB_TOPK):
            mx = jnp.max(gate, axis=0, keepdims=True)
            cand = jnp.where((gate == mx) & (mx > -jnp.inf), blk_row, nb)
            pick = blk_row == jnp.min(cand, axis=0, keepdims=True)
            sel = jnp.where(pick, 0.0, sel)
            gate = jnp.where(pick, -jnp.inf, gate)
        sel_ref[hh, 0:nb, :] = sel

    def score(qi, kpos, biases, sb, mb):
        s_all = _dot(kb_ref[kpos, :], qs_ref[:, 2 * qi * MB_BLOCK:2 * (qi + 1) * MB_BLOCK])
        for hh in range(2):
            c = 2 * qi + hh
            s = jnp.concatenate(
                [s_all[u * MB_BLOCK:(u + 1) * MB_BLOCK, hh * MB_BLOCK:(hh + 1) * MB_BLOCK] + bias
                 for u, bias in enumerate(biases[hh])], axis=0)
            mb[c:c + 1, :] = jnp.max(s, axis=0, keepdims=True).astype(BF16).astype(F32)
            sb[0:s.shape[0], c * MB_BLOCK:(c + 1) * MB_BLOCK] = s.astype(BF16)

    def consume(qi, kpos, sb, mb, first):
        for hh in range(2):
            c = 2 * qi + hh
            m_blk = mb[c:c + 1, :]
            m_old = None if first else m_ref[c:c + 1, :]
            m_new = m_blk if first else jnp.maximum(m_old, m_blk)
            p = jnp.exp2(sb[0:kpos.size, c * MB_BLOCK:(c + 1) * MB_BLOCK] - m_new.astype(BF16))
            pv = _dot(vt_ref[hh, :, kpos], p)
            acc_ref[c] = pv if first else acc_ref[c] * jnp.exp2(m_old - m_new) + pv
            m_ref[c:c + 1, :] = m_new

    def far_row(hh, qi, j, limit):
        row = sel_ref[hh, pl.ds(jnp.maximum(j, 0), 1), qi * MB_BLOCK:(qi + 1) * MB_BLOCK]
        return jnp.where((j >= 0) & (j < limit), row + c_far[hh], NEG)

    def near_kpos(qi):
        return pl.ds(pl.multiple_of(qb0 * MB_BLOCK, MB_BLOCK), (qi + 2) * MB_BLOCK)

    def score_near(qi):
        qb = qb0 + qi
        biases = []
        for hh in range(2):
            far = [far_row(hh, qi, qb0 - 1 + u, qb - 1) for u in range(qi)]
            prev = sel_ref[hh, pl.ds(jnp.maximum(qb - 1, 0), 1), qi * MB_BLOCK:(qi + 1) * MB_BLOCK]
            prev = bprev_ref[hh] + jnp.where(qb >= 1, prev, NEG)
            biases.append(far + [prev, bown_ref[hh]])
        score(qi, near_kpos(qi), biases, sn_ref, mbn_ref)

    nfar = qb0 - 1
    chains = [(qi, hh) for qi in range(nq) for hh in range(2)]

    def far_kpos(g):
        return pl.ds(pl.multiple_of((g * FAR_GROUP + 1) * MB_BLOCK, MB_BLOCK),
                     FAR_GROUP * MB_BLOCK)

    def score_far(g, qi, sb, mb):
        biases = [[far_row(hh, qi, g * FAR_GROUP + u, nfar) for u in range(FAR_GROUP)]
                  for hh in range(2)]
        score(qi, far_kpos(g), biases, sb, mb)

    score_near(0)
    for qi in range(1, nq):
        score_near(qi)
        consume(qi - 1, near_kpos(qi - 1), sn_ref, mbn_ref, True)
    score_far(0, 0, sb0_ref, mb0_ref)
    consume(nq - 1, near_kpos(nq - 1), sn_ref, mbn_ref, True)
    for qi in range(1, nq):
        score_far(0, qi, sb0_ref, mb0_ref)

    def two_groups(t, carry):
        for qi in range(nq):
            consume(qi, far_kpos(2 * t), sb0_ref, mb0_ref, False)
            score_far(2 * t + 1, qi, sb1_ref, mb1_ref)
        for qi in range(nq):
            consume(qi, far_kpos(2 * t + 1), sb1_ref, mb1_ref, False)
            score_far(2 * t + 2, qi, sb0_ref, mb0_ref)
        return carry

    lax.fori_loop(0, (jnp.maximum(nfar, 0) + 2 * FAR_GROUP - 1) // (2 * FAR_GROUP),
                  two_groups, 0)

    for qi in range(nq):
        outs = [acc_ref[2 * qi + hh, 0:hd, :] / acc_ref[2 * qi + hh, hd:hd + 1, :]
                for hh in range(2)]
        o_ref[0, qi * MB_BLOCK:(qi + 1) * MB_BLOCK, :] = jnp.concatenate(outs, axis=0).T


def _moba(q, k, v, rel_table):
    bsz, seq, _ = q.shape
    assert seq % MB_QTILE == 0
    nb = seq // MB_BLOCK
    npair = MB_HEADS // 2
    padded = MB_BLOCK + seq + PAD_END
    return pl.pallas_call(
        functools.partial(_moba_kernel, _bucket_starts()),
        grid=(bsz, npair, seq // MB_QTILE),
        in_specs=[
            pl.BlockSpec(memory_space=pltpu.SMEM),
            pl.BlockSpec((1, MB_QTILE, LANES), lambda b, p, t: (b, t, p)),
            pl.BlockSpec((1, seq, LANES), lambda b, p, t: (b, 0, p)),
            pl.BlockSpec((1, seq, LANES), lambda b, p, t: (b, 0, p)),
        ],
        out_specs=pl.BlockSpec((1, MB_QTILE, LANES), lambda b, p, t: (b, t, p)),
        out_shape=jax.ShapeDtypeStruct((bsz, seq, MB_WIDTH), F32),
        scratch_shapes=[
            pltpu.VMEM((padded, LANES), BF16),
            pltpu.VMEM((2, MB_HEAD_DIM + 16, padded), BF16),
            pltpu.VMEM((nb, LANES), F32),
            pltpu.VMEM((2, MB_BLOCK, MB_BLOCK), F32),
            pltpu.VMEM((2, MB_BLOCK, MB_BLOCK), F32),
            pltpu.VMEM((2, nb + PAD_BLOCKS, MB_QTILE), F32),
            pltpu.VMEM((LANES, 2 * MB_QTILE), BF16),
            pltpu.VMEM((2 * MB_QTILE // MB_BLOCK, MB_BLOCK), F32),
            pltpu.VMEM((2 * MB_QTILE // MB_BLOCK, MB_HEAD_DIM + 16, MB_BLOCK), F32),
            pltpu.VMEM((FAR_GROUP * MB_BLOCK, 2 * MB_QTILE), BF16),
            pltpu.VMEM((FAR_GROUP * MB_BLOCK, 2 * MB_QTILE), BF16),
            pltpu.VMEM((2 * MB_QTILE // MB_BLOCK, MB_BLOCK), F32),
            pltpu.VMEM((2 * MB_QTILE // MB_BLOCK, MB_BLOCK), F32),
            pltpu.VMEM((MB_QTILE + MB_BLOCK, 2 * MB_QTILE), BF16),
            pltpu.VMEM((2 * MB_QTILE // MB_BLOCK, MB_BLOCK), F32),
        ],
        compiler_params=_params(("parallel", "parallel", "arbitrary")),
        name="moba",
    )(rel_table, q, k, v)


def _segment_ref(b, n):
    c = b.shape[0]
    if 2 * n >= SUBLANES:
        b3 = b.reshape(c // (2 * n), 2 * n, b.shape[1])
        return jnp.broadcast_to(b3[:, n - 1:n, :], b3.shape).reshape(b.shape)
    b3 = b.reshape(c // SUBLANES, SUBLANES, b.shape[1])
    seg = lax.broadcasted_iota(jnp.int32, b3.shape, 1) // (2 * n)
    out = jnp.broadcast_to(b3[:, n - 1:n, :], b3.shape)
    for s in range(1, SUBLANES // (2 * n)):
        r = s * 2 * n + n - 1
        out = jnp.where(seg == s, jnp.broadcast_to(b3[:, r:r + 1, :], b3.shape), out)
    return out.reshape(b.shape)


def _hgrn_kernel(q_ref, f_ref, v_ref, g_ref, nw_ref, o_ref, state_ref):
    c = HG_CHUNK

    @pl.when(pl.program_id(1) == 0)
    def _reset():
        state_ref[...] = jnp.zeros_like(state_ref)

    half = c // 2
    row = lax.broadcasted_iota(jnp.int32, (c, c), 0)
    col = lax.broadcasted_iota(jnp.int32, (c, c), 1)
    tril = jnp.where(row >= col, 1.0, 0.0).astype(BF16)
    rh = lax.broadcasted_iota(jnp.int32, (half, half), 0)
    ch = lax.broadcasted_iota(jnp.int32, (half, half), 1)
    split = jnp.where(rh > ch, rh ^ ch, 0)
    levels = [1 << i for i in range(int(math.log2(half)))]

    for hh in range(HG_HEADS):
        sl = slice(hh * HG_DK, (hh + 1) * HG_DK)
        f = f_ref[0, :, sl]
        logf = jnp.log(f)
        kk = 1.0 - f
        q = q_ref[0, :, sl]
        v = v_ref[0, :, sl]
        vb = v.astype(BF16)
        qb = q.astype(BF16)
        kb = kk.astype(BF16)

        l_hi, l_mid, l_lo = _split3(logf * LOG2E)
        b = _dot(tril, l_hi) + (_dot(tril, l_mid) + _dot(tril, l_lo))

        def level_operands(n):
            if n == 1:
                return qb * f.astype(BF16), kb
            e_l = jnp.exp2(-jnp.abs(b - _segment_ref(b, n))).astype(BF16)
            return qb * e_l, kb * e_l

        d_lo = jnp.zeros((half, half), F32)
        d_hi = jnp.zeros((half, half), F32)
        for n in levels:
            q_l, k_l = level_operands(n)
            d_lo = jnp.where(split >= n, _dot_nt(q_l[:half], k_l[:half]), d_lo)
            d_hi = jnp.where(split >= n, _dot_nt(q_l[half:], k_l[half:]), d_hi)
        q_l, k_l = level_operands(half)
        cross = _dot_nt(q_l[half:], k_l[:half])

        st = state_ref[hh]
        intra = jnp.concatenate(
            [_dot(d_lo.astype(BF16), vb[:half]),
             _dot(jnp.concatenate([cross, d_hi], axis=1).astype(BF16), vb)], axis=0)
        o = (intra + jnp.sum(q * kk, axis=-1, keepdims=True) * v
             + _dot_nt((q * jnp.exp2(b)).astype(BF16), st.astype(BF16)))
        b_last = b[c - 1:c, :]
        k_dec = (kk * jnp.exp2(b_last - b)).astype(BF16)
        state_ref[hh] = st * jnp.exp2(b_last) + lax.dot_general(
            vb, k_dec, (((0,), (0,)), ((), ())), preferred_element_type=F32)

        o_ref[0, :, sl] = _rms(o, nw_ref[:, :]) * g_ref[0, :, sl]


def _hgrn(q, f, v, g, norm_w):
    bsz, seq, _ = q.shape
    assert seq % HG_CHUNK == 0
    blk = pl.BlockSpec((1, HG_CHUNK, HG_WIDTH), lambda b, t: (b, t, 0))
    return pl.pallas_call(
        _hgrn_kernel,
        grid=(bsz, seq // HG_CHUNK),
        in_specs=[blk, blk, blk, blk, pl.BlockSpec((1, HG_DV), lambda b, t: (0, 0))],
        out_specs=blk,
        out_shape=jax.ShapeDtypeStruct((bsz, seq, HG_WIDTH), F32),
        scratch_shapes=[pltpu.VMEM((HG_HEADS, HG_DV, HG_DK), F32)],
        compiler_params=_params(("parallel", "arbitrary")),
        name="hgrn",
    )(q, f, v, g, norm_w)


def kernel(x, p, w_ffn1_gu, w_ffn1_down, w_in, w_branch_a, w_branch_b, w_out,
           w_ffn2_gu, w_ffn2_down, w_ple, w_ple_gate, norm_gains, hg_norm_w,
           lb_param, rel_table):
    bsz, seq, _ = x.shape
    t = bsz * seq
    assert t % TOKEN_TILE == 0
    h = x.reshape(t, D_MODEL)
    for i in range(p.shape[0]):
        g = norm_gains[i]
        h = _ffn(h, g, w_ffn1_gu[i].astype(BF16), w_ffn1_down[i].astype(BF16), 0, 1)
        mq, mk, mv, hq, hf, hi, hg, gates = _inproj(h, g, lb_param, w_in[i].astype(BF16), i)
        r3 = lambda a: a.reshape(bsz, seq, a.shape[-1])
        o_a = _moba(r3(mq), r3(mk), r3(mv), rel_table)
        o_b = _hgrn(r3(hq), r3(hf), r3(hi), r3(hg), hg_norm_w[i:i + 1])
        h = _post(h, o_a.reshape(t, MB_WIDTH), o_b.reshape(t, HG_WIDTH), gates,
                  p[i].reshape(t, PLE_DIM), g,
                  w_branch_a[i].astype(BF16), w_branch_b[i].astype(BF16), w_out[i].astype(BF16),
                  w_ffn2_gu[i].astype(BF16), w_ffn2_down[i].astype(BF16),
                  w_ple[i].astype(BF16), w_ple_gate[i].astype(BF16))
    return h.reshape(bsz, seq, D_MODEL)
```

```python
import functools
import math

import numpy as np
import jax
import jax.numpy as jnp
from jax import lax
from jax.experimental import pallas as pl
from jax.experimental.pallas import tpu as pltpu

F32 = jnp.float32
BF16 = jnp.bfloat16

D_MODEL = 1024
PLE_DIM = 256
D_FF = 2816
MB_HEADS = 8
MB_HEAD_DIM = 64
MB_WIDTH = MB_HEADS * MB_HEAD_DIM
MB_BLOCK = 256
MB_TOPK = 3
HG_HEADS = 4
HG_DK = 128
HG_DV = 128
HG_WIDTH = HG_HEADS * HG_DV
REL_BUCKETS = 32
REL_MAX_EXACT = REL_BUCKETS // 2
REL_MAX_DIST = 128
N_NORMS = 7
EPS = 1e-6

LANES = 128
SUBLANES = 8
VMEM_LIMIT = 56 * 1024 * 1024
TOKEN_TILE = 512
FF_CHUNK = 256
HG_CHUNK = 256
MB_QTILE = 1024
FAR_GROUP = 2
PAD_BLOCKS = 3 * FAR_GROUP
PAD_END = PAD_BLOCKS * 256
NEG = -1e30
LOG2E = 1.4426950408889634


def _rms(x, g):
    return x * lax.rsqrt(jnp.mean(x * x, axis=-1, keepdims=True) + EPS) * g


def _sigmoid(x):
    return 1.0 / (1.0 + jnp.exp(-x))


def _dot(a, b):
    return jnp.dot(a, b, preferred_element_type=F32)


def _dot_nt(a, b):
    return lax.dot_general(a, b, (((1,), (1,)), ((), ())), preferred_element_type=F32)


def _split3(x):
    def top(y):
        bits = pltpu.bitcast(y, jnp.uint32) & jnp.uint32(0xFFFF0000)
        return pltpu.bitcast(bits, F32)
    hi = top(x)
    r1 = x - hi
    mid = top(r1)
    return hi.astype(BF16), mid.astype(BF16), (r1 - mid).astype(BF16)


def _const_spec(shape):
    nd = len(shape)
    return pl.BlockSpec(shape, lambda *_: (0,) * nd, pipeline_mode=pl.Buffered(1))


def _params(sem, flags=None):
    return pltpu.CompilerParams(dimension_semantics=sem, vmem_limit_bytes=VMEM_LIMIT, flags=flags)


def _ffn_step(x, g_ref, pre, post, wgu_ref, wd_ref, act_ref):
    u = _rms(x, g_ref[pre:pre + 1, :]).astype(BF16)
    for c in range(D_FF // FF_CHUNK):
        lo = c * FF_CHUNK
        g = _dot(u, wgu_ref[:, lo:lo + FF_CHUNK])
        v = _dot(u, wgu_ref[:, D_FF + lo:D_FF + lo + FF_CHUNK])
        act_ref[:, lo:lo + FF_CHUNK] = (g * _sigmoid(g) * v).astype(BF16)
    y = _dot(act_ref[...], wd_ref[...])
    return x + 0.5 * _rms(y, g_ref[post:post + 1, :])


def _ffn_kernel(pre, post, x_ref, g_ref, wgu_ref, wd_ref, o_ref, act_ref):
    o_ref[...] = _ffn_step(x_ref[...], g_ref, pre, post, wgu_ref, wd_ref, act_ref)


def _ffn(x, gains, w_gu, w_down, pre, post):
    t = x.shape[0]
    return pl.pallas_call(
        functools.partial(_ffn_kernel, pre, post),
        grid=(t // TOKEN_TILE,),
        in_specs=[
            pl.BlockSpec((TOKEN_TILE, D_MODEL), lambda i: (i, 0)),
            _const_spec((N_NORMS, D_MODEL)),
            _const_spec((D_MODEL, 2 * D_FF)),
            _const_spec((D_FF, D_MODEL)),
        ],
        out_specs=pl.BlockSpec((TOKEN_TILE, D_MODEL), lambda i: (i, 0)),
        out_shape=jax.ShapeDtypeStruct((t, D_MODEL), F32),
        scratch_shapes=[pltpu.VMEM((TOKEN_TILE, D_FF), BF16)],
        compiler_params=_params(("parallel",)),
        name="ffn",
    )(x, gains, w_gu, w_down)


IN_SPLITS = (MB_WIDTH, MB_WIDTH, MB_WIDTH, HG_WIDTH, HG_WIDTH, HG_WIDTH, HG_WIDTH, 2 * D_MODEL)
IN_COLS = sum(IN_SPLITS)


HG_Q, HG_F, HG_G = 3, 4, 6


def _inproj_kernel(layer, h_ref, g_ref, lbp_ref, w_ref, *out_refs):
    u = _rms(h_ref[...], g_ref[2:3, :]).astype(BF16)
    lp = lbp_ref[...]
    e = jnp.exp(lp - jnp.max(lp, axis=0, keepdims=True))
    sm = e / jnp.sum(e, axis=0, keepdims=True)
    lb = sm[0:1, :]
    for r in range(1, layer + 1):
        lb = lb + sm[r:r + 1, :]
    off = 0
    for idx, ref in enumerate(out_refs):
        n = ref.shape[-1]
        for c in range(0, n, 512):
            z = _dot(u, w_ref[:, off + c:off + c + 512])
            if idx in (HG_Q, HG_G):
                z = z * _sigmoid(z)
            elif idx == HG_F:
                z = lb + (1.0 - lb) * _sigmoid(z)
            ref[:, c:c + 512] = z
        off += n


def _inproj(h, gains, lb_param, w_in, layer):
    t = h.shape[0]
    return pl.pallas_call(
        functools.partial(_inproj_kernel, layer),
        grid=(t // TOKEN_TILE,),
        in_specs=[
            pl.BlockSpec((TOKEN_TILE, D_MODEL), lambda i: (i, 0)),
            _const_spec((N_NORMS, D_MODEL)),
            _const_spec(lb_param.shape),
            _const_spec((D_MODEL, IN_COLS)),
        ],
        out_specs=[pl.BlockSpec((TOKEN_TILE, n), lambda i: (i, 0)) for n in IN_SPLITS],
        out_shape=[jax.ShapeDtypeStruct((t, n), F32) for n in IN_SPLITS],
        compiler_params=_params(("parallel",)),
        name="in_proj",
    )(h, gains, lb_param, w_in)


def _post_kernel(h_ref, oa_ref, ob_ref, gg_ref, p_ref, g_ref, wa_ref, wb_ref, wo_ref,
                 wgu_ref, wd_ref, wp_ref, wg_ref, o_ref, act_ref):
    a = _dot(oa_ref[...].astype(BF16), wa_ref[...])
    b = _dot(ob_ref[...].astype(BF16), wb_ref[...])
    merged = _sigmoid(gg_ref[:, :D_MODEL]) * a + _sigmoid(gg_ref[:, D_MODEL:]) * b
    h = h_ref[...] + _rms(_dot(merged.astype(BF16), wo_ref[...]), g_ref[3:4, :])
    h = _ffn_step(h, g_ref, 4, 5, wgu_ref, wd_ref, act_ref)
    e = _dot(p_ref[...].astype(BF16), wp_ref[...])
    t = _dot(h.astype(BF16), wg_ref[...])
    o_ref[...] = h + _rms(_sigmoid(t) * e, g_ref[6:7, :])


def _post(h, o_a, o_b, gates, p, gains, w_a, w_b, w_out, w_gu, w_down, w_ple, w_ple_gate):
    t = h.shape[0]
    row = lambda n: pl.BlockSpec((TOKEN_TILE, n), lambda i: (i, 0))
    return pl.pallas_call(
        _post_kernel,
        grid=(t // TOKEN_TILE,),
        in_specs=[
            row(D_MODEL), row(MB_WIDTH), row(HG_WIDTH), row(2 * D_MODEL), row(PLE_DIM),
            _const_spec((N_NORMS, D_MODEL)),
            _const_spec((MB_WIDTH, D_MODEL)),
            _const_spec((HG_WIDTH, D_MODEL)),
            _const_spec((D_MODEL, D_MODEL)),
            _const_spec((D_MODEL, 2 * D_FF)),
            _const_spec((D_FF, D_MODEL)),
            _const_spec((PLE_DIM, D_MODEL)),
            _const_spec((D_MODEL, D_MODEL)),
        ],
        out_specs=row(D_MODEL),
        out_shape=jax.ShapeDtypeStruct((t, D_MODEL), F32),
        scratch_shapes=[pltpu.VMEM((TOKEN_TILE, D_FF), BF16)],
        compiler_params=_params(("parallel",)),
        name="post",
    )(h, o_a, o_b, gates, p, gains, w_a, w_b, w_out, w_gu, w_down, w_ple, w_ple_gate)


def _bucket_starts():
    n = np.arange(0, 2 * MB_BLOCK, dtype=np.int32)
    nf = np.maximum(n, 1).astype(np.float32)
    large = REL_MAX_EXACT + (
        np.log(nf / np.float32(REL_MAX_EXACT)) / np.float32(math.log(REL_MAX_DIST / REL_MAX_EXACT))
        * np.float32(REL_BUCKETS - REL_MAX_EXACT)).astype(np.int32)
    large = np.minimum(large, REL_BUCKETS - 1)
    bucket = np.where(n < REL_MAX_EXACT, n, large)
    assert np.all(np.diff(bucket) >= 0) and bucket[-1] == REL_BUCKETS - 1
    return [int(np.argmax(bucket >= b)) for b in range(REL_BUCKETS)]


def _moba_kernel(starts, rel_ref, q_ref, k_ref, v_ref, o_ref,
                 kb_ref, vt_ref, kmean_ref, bown_ref, bprev_ref, sel_ref, qs_ref,
                 m_ref, acc_ref, sb0_ref, sb1_ref, mb0_ref, mb1_ref, sn_ref, mbn_ref):
    pair = pl.program_id(1)
    step = pl.program_id(2)
    seq = k_ref.shape[1]
    nb = seq // MB_BLOCK
    hd = MB_HEAD_DIM
    qscale = (hd ** -0.5) * LOG2E

    @pl.when(step == 0)
    def _prepare():
        kb_ref[0:MB_BLOCK, :] = jnp.zeros((MB_BLOCK, LANES), BF16)
        kb_ref[MB_BLOCK + seq:, :] = jnp.zeros((PAD_END, LANES), BF16)
        sel_ref[:, nb:, :] = jnp.full((2, PAD_BLOCKS, MB_QTILE), NEG, F32)
        for hh in range(2):
            vt_ref[hh, :, 0:MB_BLOCK] = jnp.zeros((hd + 16, MB_BLOCK), BF16)
            vt_ref[hh, :, MB_BLOCK + seq:] = jnp.zeros((hd + 16, PAD_END), BF16)
            vt_ref[hh, hd:hd + 16, MB_BLOCK:MB_BLOCK + seq] = jnp.ones((16, seq), BF16)

        def blk(j, carry):
            r0 = pl.multiple_of(j * MB_BLOCK, MB_BLOCK)
            r1 = pl.multiple_of(r0 + MB_BLOCK, MB_BLOCK)
            kblk = k_ref[0, pl.ds(r0, MB_BLOCK), :]
            kb_ref[pl.ds(r1, MB_BLOCK), :] = kblk.astype(BF16)
            kmean_ref[pl.ds(j, 1), :] = jnp.mean(kblk, axis=0, keepdims=True)
            vt = v_ref[0, pl.ds(r0, MB_BLOCK), :].T.astype(BF16)
            vt_ref[0, 0:hd, pl.ds(r1, MB_BLOCK)] = vt[0:hd]
            vt_ref[1, 0:hd, pl.ds(r1, MB_BLOCK)] = vt[hd:2 * hd]
            return carry

        lax.fori_loop(0, nb, blk, 0)

        kk = lax.broadcasted_iota(jnp.int32, (MB_BLOCK, MB_BLOCK), 0)
        qq = lax.broadcasted_iota(jnp.int32, (MB_BLOCK, MB_BLOCK), 1)
        d_own = qq - kk
        d_prev = d_own + MB_BLOCK
        for hh in range(2):
            head = pair * 2 + hh
            b_own = jnp.full((MB_BLOCK, MB_BLOCK), rel_ref[0, head], F32)
            b_prev = jnp.full((MB_BLOCK, MB_BLOCK), rel_ref[0, head], F32)
            for b in range(1, REL_BUCKETS):
                val = rel_ref[b, head]
                b_own = jnp.where(d_own >= starts[b], val, b_own)
                b_prev = jnp.where(d_prev >= starts[b], val, b_prev)
            bown_ref[hh] = jnp.where(d_own >= 0, b_own * LOG2E, NEG)
            bprev_ref[hh] = b_prev * LOG2E

    nq = MB_QTILE // MB_BLOCK
    qb0 = step * nq
    c_far = [rel_ref[REL_BUCKETS - 1, pair * 2 + hh] * LOG2E for hh in range(2)]

    km_hi, km_mid, km_lo = _split3(kmean_ref[...])
    qt = q_ref[0].T
    row_head = lax.broadcasted_iota(jnp.int32, (2 * hd, MB_QTILE), 0) // hd
    blk_row = lax.broadcasted_iota(jnp.int32, (nb, MB_QTILE), 0)
    own_blk = qb0 + lax.broadcasted_iota(jnp.int32, (nb, MB_QTILE), 1) // MB_BLOCK
    for hh in range(2):
        qm = jnp.where(row_head == hh, qt, 0.0)
        qs = (qm * qscale).astype(BF16)
        for qi in range(nq):
            c0 = (2 * qi + hh) * MB_BLOCK
            qs_ref[:, c0:c0 + MB_BLOCK] = qs[:, qi * MB_BLOCK:(qi + 1) * MB_BLOCK]
        q_hi, q_mid, q_lo = _split3(qm)
        gate = (_dot(km_hi, q_hi) + (_dot(km_hi, q_mid) + _dot(km_mid, q_hi))
                + (_dot(km_hi, q_lo) + _dot(km_mid, q_mid) + _dot(km_lo, q_hi)))
        gate = jnp.where(blk_row < own_blk, gate, -jnp.inf)
        sel = jnp.full((nb, MB_QTILE), NEG, F32)
        for _ in range(MB_TOPK):
            mx = jnp.max(gate, axis=0, keepdims=True)
            cand = jnp.where((gate == mx) & (mx > -jnp.inf), blk_row, nb)
            pick = blk_row == jnp.min(cand, axis=0, keepdims=True)
            sel = jnp.where(pick, 0.0, sel)
            gate = jnp.where(pick, -jnp.inf, gate)
        sel_ref[hh, 0:nb, :] = sel

    def score(qi, kpos, biases, sb, mb):
        s_all = _dot(kb_ref[kpos, :], qs_ref[:, 2 * qi * MB_BLOCK:2 * (qi + 1) * MB_BLOCK])
        for hh in range(2):
            c = 2 * qi + hh
            s = jnp.concatenate(
                [s_all[u * MB_BLOCK:(u + 1) * MB_BLOCK, hh * MB_BLOCK:(hh + 1) * MB_BLOCK] + bias
                 for u, bias in enumerate(biases[hh])], axis=0)
            mb[c:c + 1, :] = jnp.max(s, axis=0, keepdims=True)
            sb[0:s.shape[0], c * MB_BLOCK:(c + 1) * MB_BLOCK] = s

    def consume(qi, kpos, sb, mb, first):
        for hh in range(2):
            c = 2 * qi + hh
            m_blk = mb[c:c + 1, :]
            m_old = None if first else m_ref[c:c + 1, :]
            m_new = m_blk if first else jnp.maximum(m_old, m_blk)
            p = jnp.exp2(sb[0:kpos.size, c * MB_BLOCK:(c + 1) * MB_BLOCK] - m_new)
            pv = _dot(vt_ref[hh, :, kpos], p.astype(BF16))
            acc_ref[c] = pv if first else acc_ref[c] * jnp.exp2(m_old - m_new) + pv
            m_ref[c:c + 1, :] = m_new

    def far_row(hh, qi, j, limit):
        row = sel_ref[hh, pl.ds(jnp.maximum(j, 0), 1), qi * MB_BLOCK:(qi + 1) * MB_BLOCK]
        return jnp.where((j >= 0) & (j < limit), row + c_far[hh], NEG)

    def near_kpos(qi):
        return pl.ds(pl.multiple_of(qb0 * MB_BLOCK, MB_BLOCK), (qi + 2) * MB_BLOCK)

    def score_near(qi):
        qb = qb0 + qi
        biases = []
        for hh in range(2):
            far = [far_row(hh, qi, qb0 - 1 + u, qb - 1) for u in range(qi)]
            prev = sel_ref[hh, pl.ds(jnp.maximum(qb - 1, 0), 1), qi * MB_BLOCK:(qi + 1) * MB_BLOCK]
            prev = bprev_ref[hh] + jnp.where(qb >= 1, prev, NEG)
            biases.append(far + [prev, bown_ref[hh]])
        score(qi, near_kpos(qi), biases, sn_ref, mbn_ref)

    nfar = qb0 - 1
    chains = [(qi, hh) for qi in range(nq) for hh in range(2)]

    def far_kpos(g):
        return pl.ds(pl.multiple_of((g * FAR_GROUP + 1) * MB_BLOCK, MB_BLOCK),
                     FAR_GROUP * MB_BLOCK)

    def score_far(g, qi, sb, mb):
        biases = [[far_row(hh, qi, g * FAR_GROUP + u, nfar) for u in range(FAR_GROUP)]
                  for hh in range(2)]
        score(qi, far_kpos(g), biases, sb, mb)

    score_near(0)
    for qi in range(1, nq):
        score_near(qi)
        consume(qi - 1, near_kpos(qi - 1), sn_ref, mbn_ref, True)
    score_far(0, 0, sb0_ref, mb0_ref)
    consume(nq - 1, near_kpos(nq - 1), sn_ref, mbn_ref, True)
    for qi in range(1, nq):
        score_far(0, qi, sb0_ref, mb0_ref)

    def two_groups(t, carry):
        for qi in range(nq):
            consume(qi, far_kpos(2 * t), sb0_ref, mb0_ref, False)
            score_far(2 * t + 1, qi, sb1_ref, mb1_ref)
        for qi in range(nq):
            consume(qi, far_kpos(2 * t + 1), sb1_ref, mb1_ref, False)
            score_far(2 * t + 2, qi, sb0_ref, mb0_ref)
        return carry

    lax.fori_loop(0, (jnp.maximum(nfar, 0) + 2 * FAR_GROUP - 1) // (2 * FAR_GROUP),
                  two_groups, 0)

    for qi in range(nq):
        outs = [acc_ref[2 * qi + hh, 0:hd, :] / acc_ref[2 * qi + hh, hd:hd + 1, :]
                for hh in range(2)]
        o_ref[0, qi * MB_BLOCK:(qi + 1) * MB_BLOCK, :] = jnp.concatenate(outs, axis=0).T


def _moba(q, k, v, rel_table):
    bsz, seq, _ = q.shape
    assert seq % MB_QTILE == 0
    nb = seq // MB_BLOCK
    npair = MB_HEADS // 2
    padded = MB_BLOCK + seq + PAD_END
    return pl.pallas_call(
        functools.partial(_moba_kernel, _bucket_starts()),
        grid=(bsz, npair, seq // MB_QTILE),
        in_specs=[
            pl.BlockSpec(memory_space=pltpu.SMEM),
            pl.BlockSpec((1, MB_QTILE, LANES), lambda b, p, t: (b, t, p)),
            pl.BlockSpec((1, seq, LANES), lambda b, p, t: (b, 0, p)),
            pl.BlockSpec((1, seq, LANES), lambda b, p, t: (b, 0, p)),
        ],
        out_specs=pl.BlockSpec((1, MB_QTILE, LANES), lambda b, p, t: (b, t, p)),
        out_shape=jax.ShapeDtypeStruct((bsz, seq, MB_WIDTH), F32),
        scratch_shapes=[
            pltpu.VMEM((padded, LANES), BF16),
            pltpu.VMEM((2, MB_HEAD_DIM + 16, padded), BF16),
            pltpu.VMEM((nb, LANES), F32),
            pltpu.VMEM((2, MB_BLOCK, MB_BLOCK), F32),
            pltpu.VMEM((2, MB_BLOCK, MB_BLOCK), F32),
            pltpu.VMEM((2, nb + PAD_BLOCKS, MB_QTILE), F32),
            pltpu.VMEM((LANES, 2 * MB_QTILE), BF16),
            pltpu.VMEM((2 * MB_QTILE // MB_BLOCK, MB_BLOCK), F32),
            pltpu.VMEM((2 * MB_QTILE // MB_BLOCK, MB_HEAD_DIM + 16, MB_BLOCK), F32),
            pltpu.VMEM((FAR_GROUP * MB_BLOCK, 2 * MB_QTILE), F32),
            pltpu.VMEM((FAR_GROUP * MB_BLOCK, 2 * MB_QTILE), F32),
            pltpu.VMEM((2 * MB_QTILE // MB_BLOCK, MB_BLOCK), F32),
            pltpu.VMEM((2 * MB_QTILE // MB_BLOCK, MB_BLOCK), F32),
            pltpu.VMEM((MB_QTILE + MB_BLOCK, 2 * MB_QTILE), F32),
            pltpu.VMEM((2 * MB_QTILE // MB_BLOCK, MB_BLOCK), F32),
        ],
        compiler_params=_params(("parallel", "parallel", "arbitrary")),
        name="moba",
    )(rel_table, q, k, v)


def _segment_ref(b, n):
    c = b.shape[0]
    if 2 * n >= SUBLANES:
        b3 = b.reshape(c // (2 * n), 2 * n, b.shape[1])
        return jnp.broadcast_to(b3[:, n - 1:n, :], b3.shape).reshape(b.shape)
    b3 = b.reshape(c // SUBLANES, SUBLANES, b.shape[1])
    seg = lax.broadcasted_iota(jnp.int32, b3.shape, 1) // (2 * n)
    out = jnp.broadcast_to(b3[:, n - 1:n, :], b3.shape)
    for s in range(1, SUBLANES // (2 * n)):
        r = s * 2 * n + n - 1
        out = jnp.where(seg == s, jnp.broadcast_to(b3[:, r:r + 1, :], b3.shape), out)
    return out.reshape(b.shape)


def _hgrn_kernel(q_ref, f_ref, v_ref, g_ref, nw_ref, o_ref, state_ref):
    c = HG_CHUNK

    @pl.when(pl.program_id(1) == 0)
    def _reset():
        state_ref[...] = jnp.zeros_like(state_ref)

    half = c // 2
    row = lax.broadcasted_iota(jnp.int32, (c, c), 0)
    col = lax.broadcasted_iota(jnp.int32, (c, c), 1)
    tril = jnp.where(row >= col, 1.0, 0.0).astype(BF16)
    rh = lax.broadcasted_iota(jnp.int32, (half, half), 0)
    ch = lax.broadcasted_iota(jnp.int32, (half, half), 1)
    split = jnp.where(rh > ch, rh ^ ch, 0)
    levels = [1 << i for i in range(int(math.log2(half)))]

    for hh in range(HG_HEADS):
        sl = slice(hh * HG_DK, (hh + 1) * HG_DK)
        f = f_ref[0, :, sl]
        logf = jnp.log(f)
        kk = 1.0 - f
        q = q_ref[0, :, sl]
        v = v_ref[0, :, sl]
        vb = v.astype(BF16)
        qb = q.astype(BF16)
        kb = kk.astype(BF16)

        l_hi, l_mid, l_lo = _split3(logf * LOG2E)
        b = _dot(tril, l_hi) + (_dot(tril, l_mid) + _dot(tril, l_lo))

        def level_operands(n):
            if n == 1:
                return qb * f.astype(BF16), kb
            e_l = jnp.exp2(-jnp.abs(b - _segment_ref(b, n))).astype(BF16)
            return qb * e_l, kb * e_l

        d_lo = jnp.zeros((half, half), F32)
        d_hi = jnp.zeros((half, half), F32)
        for n in levels:
            q_l, k_l = level_operands(n)
            d_lo = jnp.where(split >= n, _dot_nt(q_l[:half], k_l[:half]), d_lo)
            d_hi = jnp.where(split >= n, _dot_nt(q_l[half:], k_l[half:]), d_hi)
        q_l, k_l = level_operands(half)
        cross = _dot_nt(q_l[half:], k_l[:half])

        st = state_ref[hh]
        intra = jnp.concatenate(
            [_dot(d_lo.astype(BF16), vb[:half]),
             _dot(jnp.concatenate([cross, d_hi], axis=1).astype(BF16), vb)], axis=0)
        o = (intra + jnp.sum(q * kk, axis=-1, keepdims=True) * v
             + _dot_nt((q * jnp.exp2(b)).astype(BF16), st.astype(BF16)))
        b_last = b[c - 1:c, :]
        k_dec = (kk * jnp.exp2(b_last - b)).astype(BF16)
        state_ref[hh] = st * jnp.exp2(b_last) + lax.dot_general(
            vb, k_dec, (((0,), (0,)), ((), ())), preferred_element_type=F32)

        o_ref[0, :, sl] = _rms(o, nw_ref[:, :]) * g_ref[0, :, sl]


def _hgrn(q, f, v, g, norm_w):
    bsz, seq, _ = q.shape
    assert seq % HG_CHUNK == 0
    blk = pl.BlockSpec((1, HG_CHUNK, HG_WIDTH), lambda b, t: (b, t, 0))
    return pl.pallas_call(
        _hgrn_kernel,
        grid=(bsz, seq // HG_CHUNK),
        in_specs=[blk, blk, blk, blk, pl.BlockSpec((1, HG_DV), lambda b, t: (0, 0))],
        out_specs=blk,
        out_shape=jax.ShapeDtypeStruct((bsz, seq, HG_WIDTH), F32),
        scratch_shapes=[pltpu.VMEM((HG_HEADS, HG_DV, HG_DK), F32)],
        compiler_params=_params(("parallel", "arbitrary")),
        name="hgrn",
    )(q, f, v, g, norm_w)


def kernel(x, p, w_ffn1_gu, w_ffn1_down, w_in, w_branch_a, w_branch_b, w_out,
           w_ffn2_gu, w_ffn2_down, w_ple, w_ple_gate, norm_gains, hg_norm_w,
           lb_param, rel_table):
    bsz, seq, _ = x.shape
    t = bsz * seq
    assert t % TOKEN_TILE == 0
    h = x.reshape(t, D_MODEL)
    for i in range(p.shape[0]):
        g = norm_gains[i]
        h = _ffn(h, g, w_ffn1_gu[i].astype(BF16), w_ffn1_down[i].astype(BF16), 0, 1)
        mq, mk, mv, hq, hf, hi, hg, gates = _inproj(h, g, lb_param, w_in[i].astype(BF16), i)
        r3 = lambda a: a.reshape(bsz, seq, a.shape[-1])
        o_a = _moba(r3(mq), r3(mk), r3(mv), rel_table)
        o_b = _hgrn(r3(hq), r3(hf), r3(hi), r3(hg), hg_norm_w[i:i + 1])
        h = _post(h, o_a.reshape(t, MB_WIDTH), o_b.reshape(t, HG_WIDTH), gates,
                  p[i].reshape(t, PLE_DIM), g,
                  w_branch_a[i].astype(BF16), w_branch_b[i].astype(BF16), w_out[i].astype(BF16),
                  w_ffn2_gu[i].astype(BF16), w_ffn2_down[i].astype(BF16),
                  w_ple[i].astype(BF16), w_ple_gate[i].astype(BF16))
    return h.reshape(bsz, seq, D_MODEL)
```

```python
import functools
import math

import numpy as np
import jax
import jax.numpy as jnp
from jax import lax
from jax.experimental import pallas as pl
from jax.experimental.pallas import tpu as pltpu

F32 = jnp.float32
BF16 = jnp.bfloat16

D_MODEL = 1024
PLE_DIM = 256
D_FF = 2816
MB_HEADS = 8
MB_HEAD_DIM = 64
MB_WIDTH = MB_HEADS * MB_HEAD_DIM
MB_BLOCK = 256
MB_TOPK = 3
HG_HEADS = 4
HG_DK = 128
HG_DV = 128
HG_WIDTH = HG_HEADS * HG_DV
REL_BUCKETS = 32
REL_MAX_EXACT = REL_BUCKETS // 2
REL_MAX_DIST = 128
N_NORMS = 7
EPS = 1e-6

LANES = 128
SUBLANES = 8
VMEM_LIMIT = 56 * 1024 * 1024
TOKEN_TILE = 512
FF_CHUNK = 256
HG_CHUNK = 256
HG_STEP_CHUNKS = 4
MB_QTILE = 1024
FAR_GROUP = 2
PAD_BLOCKS = 3 * FAR_GROUP
PAD_END = PAD_BLOCKS * 256
NEG = -1e30
LOG2E = 1.4426950408889634


def _rms(x, g):
    return x * lax.rsqrt(jnp.mean(x * x, axis=-1, keepdims=True) + EPS) * g


def _sigmoid(x):
    return 1.0 / (1.0 + jnp.exp(-x))


def _dot(a, b):
    return jnp.dot(a, b, preferred_element_type=F32)


def _dot_nt(a, b):
    return lax.dot_general(a, b, (((1,), (1,)), ((), ())), preferred_element_type=F32)


def _split3(x):
    def top(y):
        bits = pltpu.bitcast(y, jnp.uint32) & jnp.uint32(0xFFFF0000)
        return pltpu.bitcast(bits, F32)
    hi = top(x)
    r1 = x - hi
    mid = top(r1)
    return hi.astype(BF16), mid.astype(BF16), (r1 - mid).astype(BF16)


def _const_spec(shape):
    nd = len(shape)
    return pl.BlockSpec(shape, lambda *_: (0,) * nd, pipeline_mode=pl.Buffered(1))


def _params(sem, flags=None):
    return pltpu.CompilerParams(dimension_semantics=sem, vmem_limit_bytes=VMEM_LIMIT, flags=flags)


def _ffn_step(x, g_ref, pre, post, wgu_ref, wd_ref, act_ref):
    u = _rms(x, g_ref[pre:pre + 1, :]).astype(BF16)
    for c in range(D_FF // FF_CHUNK):
        lo = c * FF_CHUNK
        g = _dot(u, wgu_ref[:, lo:lo + FF_CHUNK])
        v = _dot(u, wgu_ref[:, D_FF + lo:D_FF + lo + FF_CHUNK])
        act_ref[:, lo:lo + FF_CHUNK] = (g * _sigmoid(g) * v).astype(BF16)
    y = _dot(act_ref[...], wd_ref[...])
    return x + 0.5 * _rms(y, g_ref[post:post + 1, :])


def _ffn_kernel(pre, post, x_ref, g_ref, wgu_ref, wd_ref, o_ref, act_ref):
    o_ref[...] = _ffn_step(x_ref[...], g_ref, pre, post, wgu_ref, wd_ref, act_ref)


def _ffn(x, gains, w_gu, w_down, pre, post):
    t = x.shape[0]
    return pl.pallas_call(
        functools.partial(_ffn_kernel, pre, post),
        grid=(t // TOKEN_TILE,),
        in_specs=[
            pl.BlockSpec((TOKEN_TILE, D_MODEL), lambda i: (i, 0)),
            _const_spec((N_NORMS, D_MODEL)),
            _const_spec((D_MODEL, 2 * D_FF)),
            _const_spec((D_FF, D_MODEL)),
        ],
        out_specs=pl.BlockSpec((TOKEN_TILE, D_MODEL), lambda i: (i, 0)),
        out_shape=jax.ShapeDtypeStruct((t, D_MODEL), F32),
        scratch_shapes=[pltpu.VMEM((TOKEN_TILE, D_FF), BF16)],
        compiler_params=_params(("parallel",)),
        name="ffn",
    )(x, gains, w_gu, w_down)


IN_SPLITS = (MB_WIDTH, MB_WIDTH, MB_WIDTH, HG_WIDTH, HG_WIDTH, HG_WIDTH, HG_WIDTH, 2 * D_MODEL)
IN_COLS = sum(IN_SPLITS)


HG_Q, HG_F, HG_G = 3, 4, 6


def _inproj_kernel(layer, h_ref, g_ref, lbp_ref, w_ref, *out_refs):
    u = _rms(h_ref[...], g_ref[2:3, :]).astype(BF16)
    lp = lbp_ref[...]
    e = jnp.exp(lp - jnp.max(lp, axis=0, keepdims=True))
    sm = e / jnp.sum(e, axis=0, keepdims=True)
    lb = sm[0:1, :]
    for r in range(1, layer + 1):
        lb = lb + sm[r:r + 1, :]
    off = 0
    for idx, ref in enumerate(out_refs):
        n = ref.shape[-1]
        for c in range(0, n, 512):
            z = _dot(u, w_ref[:, off + c:off + c + 512])
            if idx in (HG_Q, HG_G):
                z = z * _sigmoid(z)
            elif idx == HG_F:
                z = lb + (1.0 - lb) * _sigmoid(z)
            ref[:, c:c + 512] = z
        off += n


def _inproj(h, gains, lb_param, w_in, layer):
    t = h.shape[0]
    return pl.pallas_call(
        functools.partial(_inproj_kernel, layer),
        grid=(t // TOKEN_TILE,),
        in_specs=[
            pl.BlockSpec((TOKEN_TILE, D_MODEL), lambda i: (i, 0)),
            _const_spec((N_NORMS, D_MODEL)),
            _const_spec(lb_param.shape),
            _const_spec((D_MODEL, IN_COLS)),
        ],
        out_specs=[pl.BlockSpec((TOKEN_TILE, n), lambda i: (i, 0)) for n in IN_SPLITS],
        out_shape=[jax.ShapeDtypeStruct((t, n), F32) for n in IN_SPLITS],
        compiler_params=_params(("parallel",)),
        name="in_proj",
    )(h, gains, lb_param, w_in)


def _post_kernel(h_ref, oa_ref, ob_ref, gg_ref, p_ref, g_ref, wa_ref, wb_ref, wo_ref,
                 wgu_ref, wd_ref, wp_ref, wg_ref, o_ref, act_ref):
    a = _dot(oa_ref[...].astype(BF16), wa_ref[...])
    b = _dot(ob_ref[...].astype(BF16), wb_ref[...])
    merged = _sigmoid(gg_ref[:, :D_MODEL]) * a + _sigmoid(gg_ref[:, D_MODEL:]) * b
    h = h_ref[...] + _rms(_dot(merged.astype(BF16), wo_ref[...]), g_ref[3:4, :])
    h = _ffn_step(h, g_ref, 4, 5, wgu_ref, wd_ref, act_ref)
    e = _dot(p_ref[...].astype(BF16), wp_ref[...])
    t = _dot(h.astype(BF16), wg_ref[...])
    o_ref[...] = h + _rms(_sigmoid(t) * e, g_ref[6:7, :])


def _post(h, o_a, o_b, gates, p, gains, w_a, w_b, w_out, w_gu, w_down, w_ple, w_ple_gate):
    t = h.shape[0]
    row = lambda n: pl.BlockSpec((TOKEN_TILE, n), lambda i: (i, 0))
    return pl.pallas_call(
        _post_kernel,
        grid=(t // TOKEN_TILE,),
        in_specs=[
            row(D_MODEL), row(MB_WIDTH), row(HG_WIDTH), row(2 * D_MODEL), row(PLE_DIM),
            _const_spec((N_NORMS, D_MODEL)),
            _const_spec((MB_WIDTH, D_MODEL)),
            _const_spec((HG_WIDTH, D_MODEL)),
            _const_spec((D_MODEL, D_MODEL)),
            _const_spec((D_MODEL, 2 * D_FF)),
            _const_spec((D_FF, D_MODEL)),
            _const_spec((PLE_DIM, D_MODEL)),
            _const_spec((D_MODEL, D_MODEL)),
        ],
        out_specs=row(D_MODEL),
        out_shape=jax.ShapeDtypeStruct((t, D_MODEL), F32),
        scratch_shapes=[pltpu.VMEM((TOKEN_TILE, D_FF), BF16)],
        compiler_params=_params(("parallel",)),
        name="post",
    )(h, o_a, o_b, gates, p, gains, w_a, w_b, w_out, w_gu, w_down, w_ple, w_ple_gate)


def _bucket_starts():
    n = np.arange(0, 2 * MB_BLOCK, dtype=np.int32)
    nf = np.maximum(n, 1).astype(np.float32)
    large = REL_MAX_EXACT + (
        np.log(nf / np.float32(REL_MAX_EXACT)) / np.float32(math.log(REL_MAX_DIST / REL_MAX_EXACT))
        * np.float32(REL_BUCKETS - REL_MAX_EXACT)).astype(np.int32)
    large = np.minimum(large, REL_BUCKETS - 1)
    bucket = np.where(n < REL_MAX_EXACT, n, large)
    assert np.all(np.diff(bucket) >= 0) and bucket[-1] == REL_BUCKETS - 1
    return [int(np.argmax(bucket >= b)) for b in range(REL_BUCKETS)]


def _moba_kernel(starts, rel_ref, q_ref, k_ref, v_ref, o_ref,
                 kb_ref, vt_ref, kmean_ref, bown_ref, bprev_ref, sel_ref, qs_ref,
                 m_ref, acc_ref, sb0_ref, sb1_ref, mb0_ref, mb1_ref, sn_ref, mbn_ref):
    pair = pl.program_id(1)
    step = pl.program_id(2)
    seq = k_ref.shape[1]
    nb = seq // MB_BLOCK
    hd = MB_HEAD_DIM
    qscale = (hd ** -0.5) * LOG2E

    @pl.when(step == 0)
    def _prepare():
        kb_ref[0:MB_BLOCK, :] = jnp.zeros((MB_BLOCK, LANES), BF16)
        kb_ref[MB_BLOCK + seq:, :] = jnp.zeros((PAD_END, LANES), BF16)
        sel_ref[:, nb:, :] = jnp.full((2, PAD_BLOCKS, MB_QTILE), NEG, F32)
        for hh in range(2):
            vt_ref[hh, :, 0:MB_BLOCK] = jnp.zeros((hd + 16, MB_BLOCK), BF16)
            vt_ref[hh, :, MB_BLOCK + seq:] = jnp.zeros((hd + 16, PAD_END), BF16)
            vt_ref[hh, hd:hd + 16, MB_BLOCK:MB_BLOCK + seq] = jnp.ones((16, seq), BF16)

        def blk(j, carry):
            r0 = pl.multiple_of(j * MB_BLOCK, MB_BLOCK)
            r1 = pl.multiple_of(r0 + MB_BLOCK, MB_BLOCK)
            kblk = k_ref[0, pl.ds(r0, MB_BLOCK), :]
            kb_ref[pl.ds(r1, MB_BLOCK), :] = kblk.astype(BF16)
            kmean_ref[pl.ds(j, 1), :] = jnp.mean(kblk, axis=0, keepdims=True)
            vt = v_ref[0, pl.ds(r0, MB_BLOCK), :].T.astype(BF16)
            vt_ref[0, 0:hd, pl.ds(r1, MB_BLOCK)] = vt[0:hd]
            vt_ref[1, 0:hd, pl.ds(r1, MB_BLOCK)] = vt[hd:2 * hd]
            return carry

        lax.fori_loop(0, nb, blk, 0)

        kk = lax.broadcasted_iota(jnp.int32, (MB_BLOCK, MB_BLOCK), 0)
        qq = lax.broadcasted_iota(jnp.int32, (MB_BLOCK, MB_BLOCK), 1)
        d_own = qq - kk
        d_prev = d_own + MB_BLOCK
        for hh in range(2):
            head = pair * 2 + hh
            b_own = jnp.full((MB_BLOCK, MB_BLOCK), rel_ref[0, head], F32)
            b_prev = jnp.full((MB_BLOCK, MB_BLOCK), rel_ref[0, head], F32)
            for b in range(1, REL_BUCKETS):
                val = rel_ref[b, head]
                b_own = jnp.where(d_own >= starts[b], val, b_own)
                b_prev = jnp.where(d_prev >= starts[b], val, b_prev)
            bown_ref[hh] = jnp.where(d_own >= 0, b_own * LOG2E, NEG)
            bprev_ref[hh] = b_prev * LOG2E

    nq = MB_QTILE // MB_BLOCK
    qb0 = step * nq
    c_far = [rel_ref[REL_BUCKETS - 1, pair * 2 + hh] * LOG2E for hh in range(2)]

    km_hi, km_mid, km_lo = _split3(kmean_ref[...])
    qt = q_ref[0].T
    row_head = lax.broadcasted_iota(jnp.int32, (2 * hd, MB_QTILE), 0) // hd
    blk_row = lax.broadcasted_iota(jnp.int32, (nb, MB_QTILE), 0)
    own_blk = qb0 + lax.broadcasted_iota(jnp.int32, (nb, MB_QTILE), 1) // MB_BLOCK
    for hh in range(2):
        qm = jnp.where(row_head == hh, qt, 0.0)
        qs = (qm * qscale).astype(BF16)
        for qi in range(nq):
            c0 = (2 * qi + hh) * MB_BLOCK
            qs_ref[:, c0:c0 + MB_BLOCK] = qs[:, qi * MB_BLOCK:(qi + 1) * MB_BLOCK]
        q_hi, q_mid, q_lo = _split3(qm)
        gate = (_dot(km_hi, q_hi) + (_dot(km_hi, q_mid) + _dot(km_mid, q_hi))
                + (_dot(km_hi, q_lo) + _dot(km_mid, q_mid) + _dot(km_lo, q_hi)))
        gate = jnp.where(blk_row < own_blk, gate, -jnp.inf)
        sel = jnp.full((nb, MB_QTILE), NEG, F32)
        for _ in range(MB_TOPK):
            mx = jnp.max(gate, axis=0, keepdims=True)
            cand = jnp.where((gate == mx) & (mx > -jnp.inf), blk_row, nb)
            pick = blk_row == jnp.min(cand, axis=0, keepdims=True)
            sel = jnp.where(pick, 0.0, sel)
            gate = jnp.where(pick, -jnp.inf, gate)
        sel_ref[hh, 0:nb, :] = sel

    def key_slice(blocks):
        return pl.ds(pl.multiple_of(blocks[0] * MB_BLOCK, MB_BLOCK), blocks[1] * MB_BLOCK)

    def score(qi, blocks, biases, sb, mb):
        s_all = _dot(kb_ref[key_slice(blocks), :],
                     qs_ref[:, 2 * qi * MB_BLOCK:2 * (qi + 1) * MB_BLOCK])
        for hh in range(2):
            c = 2 * qi + hh
            s = jnp.concatenate(
                [s_all[u * MB_BLOCK:(u + 1) * MB_BLOCK, hh * MB_BLOCK:(hh + 1) * MB_BLOCK] + bias
                 for u, bias in enumerate(biases[hh])], axis=0)
            mb[c:c + 1, :] = jnp.max(s, axis=0, keepdims=True)
            sb[0:s.shape[0], c * MB_BLOCK:(c + 1) * MB_BLOCK] = s

    def consume(qi, blocks, sb, mb, first):
        kpos = key_slice(blocks)
        for hh in range(2):
            c = 2 * qi + hh
            m_blk = mb[c:c + 1, :]
            m_old = None if first else m_ref[c:c + 1, :]
            m_new = m_blk if first else jnp.maximum(m_old, m_blk)
            p = jnp.exp2(sb[0:kpos.size, c * MB_BLOCK:(c + 1) * MB_BLOCK] - m_new)
            pv = _dot(vt_ref[hh, :, kpos], p.astype(BF16))
            acc_ref[c] = pv if first else acc_ref[c] * jnp.exp2(m_old - m_new) + pv
            m_ref[c:c + 1, :] = m_new

    def far_row(hh, qi, j, limit):
        row = sel_ref[hh, pl.ds(jnp.maximum(j, 0), 1), qi * MB_BLOCK:(qi + 1) * MB_BLOCK]
        return jnp.where((j >= 0) & (j < limit), row + c_far[hh], NEG)

    def near_blocks(qi):
        return qb0, qi + 2

    def score_near(qi):
        qb = qb0 + qi
        biases = []
        for hh in range(2):
            far = [far_row(hh, qi, qb0 - 1 + u, qb - 1) for u in range(qi)]
            prev = sel_ref[hh, pl.ds(jnp.maximum(qb - 1, 0), 1), qi * MB_BLOCK:(qi + 1) * MB_BLOCK]
            prev = bprev_ref[hh] + jnp.where(qb >= 1, prev, NEG)
            biases.append(far + [prev, bown_ref[hh]])
        score(qi, near_blocks(qi), biases, sn_ref, mbn_ref)

    nfar = qb0 - 1
    chains = [(qi, hh) for qi in range(nq) for hh in range(2)]

    def far_blocks(g, n=FAR_GROUP):
        return g * FAR_GROUP + 1, n

    def score_far(g, qi, sb, mb, n=FAR_GROUP):
        biases = [[far_row(hh, qi, g * FAR_GROUP + u, nfar) for u in range(n)]
                  for hh in range(2)]
        score(qi, far_blocks(g, n), biases, sb, mb)

    score_near(0)
    for qi in range(1, nq):
        score_near(qi)
        consume(qi - 1, near_blocks(qi - 1), sn_ref, mbn_ref, True)
    score_far(0, 0, sb0_ref, mb0_ref)
    consume(nq - 1, near_blocks(nq - 1), sn_ref, mbn_ref, True)
    for qi in range(1, nq):
        score_far(0, qi, sb0_ref, mb0_ref)

    assert nq % (2 * FAR_GROUP) == 0 and FAR_GROUP > 1

    def two_groups(t, last):
        n_last = FAR_GROUP - 1 if last else FAR_GROUP
        for qi in range(nq):
            consume(qi, far_blocks(2 * t), sb0_ref, mb0_ref, False)
            score_far(2 * t + 1, qi, sb1_ref, mb1_ref, n_last)
        for qi in range(nq):
            consume(qi, far_blocks(2 * t + 1, n_last), sb1_ref, mb1_ref, False)
            if not last:
                score_far(2 * t + 2, qi, sb0_ref, mb0_ref)

    trips = (jnp.maximum(nfar, 0) + 2 * FAR_GROUP - 1) // (2 * FAR_GROUP)
    lax.fori_loop(0, trips - 1, lambda t, carry: (two_groups(t, False), carry)[1], 0)

    @pl.when(trips > 0)
    def _last_trip():
        two_groups(trips - 1, True)

    for qi in range(nq):
        outs = [acc_ref[2 * qi + hh, 0:hd, :] / acc_ref[2 * qi + hh, hd:hd + 1, :]
                for hh in range(2)]
        o_ref[0, qi * MB_BLOCK:(qi + 1) * MB_BLOCK, :] = jnp.concatenate(outs, axis=0).T


def _moba(q, k, v, rel_table):
    bsz, seq, _ = q.shape
    assert seq % MB_QTILE == 0
    nb = seq // MB_BLOCK
    npair = MB_HEADS // 2
    padded = MB_BLOCK + seq + PAD_END
    return pl.pallas_call(
        functools.partial(_moba_kernel, _bucket_starts()),
        grid=(bsz, npair, seq // MB_QTILE),
        in_specs=[
            pl.BlockSpec(memory_space=pltpu.SMEM),
            pl.BlockSpec((1, MB_QTILE, LANES), lambda b, p, t: (b, t, p)),
            pl.BlockSpec((1, seq, LANES), lambda b, p, t: (b, 0, p)),
            pl.BlockSpec((1, seq, LANES), lambda b, p, t: (b, 0, p)),
        ],
        out_specs=pl.BlockSpec((1, MB_QTILE, LANES), lambda b, p, t: (b, t, p)),
        out_shape=jax.ShapeDtypeStruct((bsz, seq, MB_WIDTH), F32),
        scratch_shapes=[
            pltpu.VMEM((padded, LANES), BF16),
            pltpu.VMEM((2, MB_HEAD_DIM + 16, padded), BF16),
            pltpu.VMEM((nb, LANES), F32),
            pltpu.VMEM((2, MB_BLOCK, MB_BLOCK), F32),
            pltpu.VMEM((2, MB_BLOCK, MB_BLOCK), F32),
            pltpu.VMEM((2, nb + PAD_BLOCKS, MB_QTILE), F32),
            pltpu.VMEM((LANES, 2 * MB_QTILE), BF16),
            pltpu.VMEM((2 * MB_QTILE // MB_BLOCK, MB_BLOCK), F32),
            pltpu.VMEM((2 * MB_QTILE // MB_BLOCK, MB_HEAD_DIM + 16, MB_BLOCK), F32),
            pltpu.VMEM((FAR_GROUP * MB_BLOCK, 2 * MB_QTILE), F32),
            pltpu.VMEM((FAR_GROUP * MB_BLOCK, 2 * MB_QTILE), F32),
            pltpu.VMEM((2 * MB_QTILE // MB_BLOCK, MB_BLOCK), F32),
            pltpu.VMEM((2 * MB_QTILE // MB_BLOCK, MB_BLOCK), F32),
            pltpu.VMEM((MB_QTILE + MB_BLOCK, 2 * MB_QTILE), F32),
            pltpu.VMEM((2 * MB_QTILE // MB_BLOCK, MB_BLOCK), F32),
        ],
        compiler_params=_params(("parallel", "parallel", "arbitrary")),
        name="moba",
    )(rel_table, q, k, v)


def _segment_ref(b, n):
    c = b.shape[0]
    if 2 * n >= SUBLANES:
        b3 = b.reshape(c // (2 * n), 2 * n, b.shape[1])
        return jnp.broadcast_to(b3[:, n - 1:n, :], b3.shape).reshape(b.shape)
    b3 = b.reshape(c // SUBLANES, SUBLANES, b.shape[1])
    seg = lax.broadcasted_iota(jnp.int32, b3.shape, 1) // (2 * n)
    out = jnp.broadcast_to(b3[:, n - 1:n, :], b3.shape)
    for s in range(1, SUBLANES // (2 * n)):
        r = s * 2 * n + n - 1
        out = jnp.where(seg == s, jnp.broadcast_to(b3[:, r:r + 1, :], b3.shape), out)
    return out.reshape(b.shape)


def _hgrn_kernel(q_ref, f_ref, v_ref, g_ref, nw_ref, o_ref, state_ref):
    c = HG_CHUNK

    @pl.when(pl.program_id(1) == 0)
    def _reset():
        state_ref[...] = jnp.zeros_like(state_ref)

    half = c // 2
    row = lax.broadcasted_iota(jnp.int32, (c, c), 0)
    col = lax.broadcasted_iota(jnp.int32, (c, c), 1)
    tril = jnp.where(row >= col, 1.0, 0.0).astype(BF16)
    rh = lax.broadcasted_iota(jnp.int32, (half, half), 0)
    ch = lax.broadcasted_iota(jnp.int32, (half, half), 1)
    split = jnp.where(rh > ch, rh ^ ch, 0)
    levels = [1 << i for i in range(int(math.log2(half)))]

    def chunk_head(rows, hh):
        sl = slice(hh * HG_DK, (hh + 1) * HG_DK)
        f = f_ref[0, rows, sl]
        logf = jnp.log(f)
        kk = 1.0 - f
        q = q_ref[0, rows, sl]
        v = v_ref[0, rows, sl]
        vb = v.astype(BF16)
        qb = q.astype(BF16)
        kb = kk.astype(BF16)

        l_hi, l_mid, l_lo = _split3(logf * LOG2E)
        b = _dot(tril, l_hi) + (_dot(tril, l_mid) + _dot(tril, l_lo))

        def level_operands(n):
            if n == 1:
                return qb * f.astype(BF16), kb
            e_l = jnp.exp2(-jnp.abs(b - _segment_ref(b, n))).astype(BF16)
            return qb * e_l, kb * e_l

        d_lo = jnp.zeros((half, half), F32)
        d_hi = jnp.zeros((half, half), F32)
        for n in levels:
            q_l, k_l = level_operands(n)
            d_lo = jnp.where(split >= n, _dot_nt(q_l[:half], k_l[:half]), d_lo)
            d_hi = jnp.where(split >= n, _dot_nt(q_l[half:], k_l[half:]), d_hi)
        q_l, k_l = level_operands(half)
        cross = _dot_nt(q_l[half:], k_l[:half])

        st = state_ref[hh]
        intra = jnp.concatenate(
            [_dot(d_lo.astype(BF16), vb[:half]),
             _dot(jnp.concatenate([cross, d_hi], axis=1).astype(BF16), vb)], axis=0)
        o = (intra + jnp.sum(q * kk, axis=-1, keepdims=True) * v
             + _dot_nt((q * jnp.exp2(b)).astype(BF16), st.astype(BF16)))
        b_last = b[c - 1:c, :]
        k_dec = (kk * jnp.exp2(b_last - b)).astype(BF16)
        state_ref[hh] = st * jnp.exp2(b_last) + lax.dot_general(
            vb, k_dec, (((0,), (0,)), ((), ())), preferred_element_type=F32)

        o_ref[0, rows, sl] = _rms(o, nw_ref[:, :]) * g_ref[0, rows, sl]

    for sub in range(HG_STEP_CHUNKS):
        for hh in range(HG_HEADS):
            chunk_head(slice(sub * c, (sub + 1) * c), hh)


def _hgrn(q, f, v, g, norm_w):
    bsz, seq, _ = q.shape
    step_rows = HG_STEP_CHUNKS * HG_CHUNK
    assert seq % step_rows == 0
    blk = pl.BlockSpec((1, step_rows, HG_WIDTH), lambda b, t: (b, t, 0))
    return pl.pallas_call(
        _hgrn_kernel,
        grid=(bsz, seq // step_rows),
        in_specs=[blk, blk, blk, blk, pl.BlockSpec((1, HG_DV), lambda b, t: (0, 0))],
        out_specs=blk,
        out_shape=jax.ShapeDtypeStruct((bsz, seq, HG_WIDTH), F32),
        scratch_shapes=[pltpu.VMEM((HG_HEADS, HG_DV, HG_DK), F32)],
        compiler_params=_params(("parallel", "arbitrary")),
        name="hgrn",
    )(q, f, v, g, norm_w)


def kernel(x, p, w_ffn1_gu, w_ffn1_down, w_in, w_branch_a, w_branch_b, w_out,
           w_ffn2_gu, w_ffn2_down, w_ple, w_ple_gate, norm_gains, hg_norm_w,
           lb_param, rel_table):
    bsz, seq, _ = x.shape
    t = bsz * seq
    assert t % TOKEN_TILE == 0
    h = x.reshape(t, D_MODEL)
    for i in range(p.shape[0]):
        g = norm_gains[i]
        h = _ffn(h, g, w_ffn1_gu[i].astype(BF16), w_ffn1_down[i].astype(BF16), 0, 1)
        mq, mk, mv, hq, hf, hi, hg, gates = _inproj(h, g, lb_param, w_in[i].astype(BF16), i)
        r3 = lambda a: a.reshape(bsz, seq, a.shape[-1])
        o_a = _moba(r3(mq), r3(mk), r3(mv), rel_table)
        o_b = _hgrn(r3(hq), r3(hf), r3(hi), r3(hg), hg_norm_w[i:i + 1])
        h = _post(h, o_a.reshape(t, MB_WIDTH), o_b.reshape(t, HG_WIDTH), gates,
                  p[i].reshape(t, PLE_DIM), g,
                  w_branch_a[i].astype(BF16), w_branch_b[i].astype(BF16), w_out[i].astype(BF16),
                  w_ffn2_gu[i].astype(BF16), w_ffn2_down[i].astype(BF16),
                  w_ple[i].astype(BF16), w_ple_gate[i].astype(BF16))
    return h.reshape(bsz, seq, D_MODEL)
```

```python
import functools
import math

import numpy as np
import jax
import jax.numpy as jnp
from jax import lax
from jax.experimental import pallas as pl
from jax.experimental.pallas import tpu as pltpu

F32 = jnp.float32
BF16 = jnp.bfloat16

D_MODEL = 1024
PLE_DIM = 256
D_FF = 2816
MB_HEADS = 8
MB_HEAD_DIM = 64
MB_WIDTH = MB_HEADS * MB_HEAD_DIM
MB_BLOCK = 256
MB_TOPK = 3
HG_HEADS = 4
HG_DK = 128
HG_DV = 128
HG_WIDTH = HG_HEADS * HG_DV
REL_BUCKETS = 32
REL_MAX_EXACT = REL_BUCKETS // 2
REL_MAX_DIST = 128
N_NORMS = 7
EPS = 1e-6

LANES = 128
SUBLANES = 8
VMEM_LIMIT = 56 * 1024 * 1024
TOKEN_TILE = 512
FF_CHUNK = 256
HG_CHUNK = 256
HG_STEP_CHUNKS = 4
MB_QTILE = 1024
FAR_GROUP = 2
PAD_BLOCKS = 3 * FAR_GROUP
PAD_END = PAD_BLOCKS * 256
NEG = -1e30
LOG2E = 1.4426950408889634


def _rms(x, g):
    return x * lax.rsqrt(jnp.mean(x * x, axis=-1, keepdims=True) + EPS) * g


def _sigmoid(x):
    return 1.0 / (1.0 + jnp.exp(-x))


def _dot(a, b):
    return jnp.dot(a, b, preferred_element_type=F32)


def _dot_nt(a, b):
    return lax.dot_general(a, b, (((1,), (1,)), ((), ())), preferred_element_type=F32)


def _split3(x):
    def top(y):
        bits = pltpu.bitcast(y, jnp.uint32) & jnp.uint32(0xFFFF0000)
        return pltpu.bitcast(bits, F32)
    hi = top(x)
    r1 = x - hi
    mid = top(r1)
    return hi.astype(BF16), mid.astype(BF16), (r1 - mid).astype(BF16)


def _const_spec(shape):
    nd = len(shape)
    return pl.BlockSpec(shape, lambda *_: (0,) * nd, pipeline_mode=pl.Buffered(1))


def _params(sem, flags=None):
    return pltpu.CompilerParams(dimension_semantics=sem, vmem_limit_bytes=VMEM_LIMIT, flags=flags)


def _ffn_step(x, g_ref, pre, post, wgu_ref, wd_ref, act_ref):
    u = _rms(x, g_ref[pre:pre + 1, :]).astype(BF16)
    for c in range(D_FF // FF_CHUNK):
        lo = c * FF_CHUNK
        g = _dot(u, wgu_ref[:, lo:lo + FF_CHUNK])
        v = _dot(u, wgu_ref[:, D_FF + lo:D_FF + lo + FF_CHUNK])
        act_ref[:, lo:lo + FF_CHUNK] = (g * _sigmoid(g) * v).astype(BF16)
    y = _dot(act_ref[...], wd_ref[...])
    return x + 0.5 * _rms(y, g_ref[post:post + 1, :])


def _ffn_kernel(pre, post, x_ref, g_ref, wgu_ref, wd_ref, o_ref, act_ref):
    o_ref[...] = _ffn_step(x_ref[...], g_ref, pre, post, wgu_ref, wd_ref, act_ref)


def _ffn(x, gains, w_gu, w_down, pre, post):
    t = x.shape[0]
    return pl.pallas_call(
        functools.partial(_ffn_kernel, pre, post),
        grid=(t // TOKEN_TILE,),
        in_specs=[
            pl.BlockSpec((TOKEN_TILE, D_MODEL), lambda i: (i, 0)),
            _const_spec((N_NORMS, D_MODEL)),
            _const_spec((D_MODEL, 2 * D_FF)),
            _const_spec((D_FF, D_MODEL)),
        ],
        out_specs=pl.BlockSpec((TOKEN_TILE, D_MODEL), lambda i: (i, 0)),
        out_shape=jax.ShapeDtypeStruct((t, D_MODEL), F32),
        scratch_shapes=[pltpu.VMEM((TOKEN_TILE, D_FF), BF16)],
        compiler_params=_params(("parallel",)),
        name="ffn",
    )(x, gains, w_gu, w_down)


IN_SPLITS = (MB_WIDTH, MB_WIDTH, MB_WIDTH, HG_WIDTH, HG_WIDTH, HG_WIDTH, HG_WIDTH, 2 * D_MODEL)
IN_COLS = sum(IN_SPLITS)


HG_Q, HG_F, HG_G = 3, 4, 6


def _inproj_kernel(layer, h_ref, g_ref, lbp_ref, w_ref, *out_refs):
    u = _rms(h_ref[...], g_ref[2:3, :]).astype(BF16)
    lp = lbp_ref[...]
    e = jnp.exp(lp - jnp.max(lp, axis=0, keepdims=True))
    sm = e / jnp.sum(e, axis=0, keepdims=True)
    lb = sm[0:1, :]
    for r in range(1, layer + 1):
        lb = lb + sm[r:r + 1, :]
    off = 0
    for idx, ref in enumerate(out_refs):
        n = ref.shape[-1]
        for c in range(0, n, 512):
            z = _dot(u, w_ref[:, off + c:off + c + 512])
            if idx in (HG_Q, HG_G):
                z = z * _sigmoid(z)
            elif idx == HG_F:
                z = lb + (1.0 - lb) * _sigmoid(z)
            ref[:, c:c + 512] = z
        off += n


def _inproj(h, gains, lb_param, w_in, layer):
    t = h.shape[0]
    return pl.pallas_call(
        functools.partial(_inproj_kernel, layer),
        grid=(t // TOKEN_TILE,),
        in_specs=[
            pl.BlockSpec((TOKEN_TILE, D_MODEL), lambda i: (i, 0)),
            _const_spec((N_NORMS, D_MODEL)),
            _const_spec(lb_param.shape),
            _const_spec((D_MODEL, IN_COLS)),
        ],
        out_specs=[pl.BlockSpec((TOKEN_TILE, n), lambda i: (i, 0)) for n in IN_SPLITS],
        out_shape=[jax.ShapeDtypeStruct((t, n), F32) for n in IN_SPLITS],
        compiler_params=_params(("parallel",)),
        name="in_proj",
    )(h, gains, lb_param, w_in)


def _post_kernel(h_ref, oa_ref, ob_ref, gg_ref, p_ref, g_ref, wa_ref, wb_ref, wo_ref,
                 wgu_ref, wd_ref, wp_ref, wg_ref, o_ref, act_ref):
    a = _dot(oa_ref[...].astype(BF16), wa_ref[...])
    b = _dot(ob_ref[...].astype(BF16), wb_ref[...])
    merged = _sigmoid(gg_ref[:, :D_MODEL]) * a + _sigmoid(gg_ref[:, D_MODEL:]) * b
    h = h_ref[...] + _rms(_dot(merged.astype(BF16), wo_ref[...]), g_ref[3:4, :])
    h = _ffn_step(h, g_ref, 4, 5, wgu_ref, wd_ref, act_ref)
    e = _dot(p_ref[...].astype(BF16), wp_ref[...])
    t = _dot(h.astype(BF16), wg_ref[...])
    o_ref[...] = h + _rms(_sigmoid(t) * e, g_ref[6:7, :])


def _post(h, o_a, o_b, gates, p, gains, w_a, w_b, w_out, w_gu, w_down, w_ple, w_ple_gate):
    t = h.shape[0]
    row = lambda n: pl.BlockSpec((TOKEN_TILE, n), lambda i: (i, 0))
    return pl.pallas_call(
        _post_kernel,
        grid=(t // TOKEN_TILE,),
        in_specs=[
            row(D_MODEL), row(MB_WIDTH), row(HG_WIDTH), row(2 * D_MODEL), row(PLE_DIM),
            _const_spec((N_NORMS, D_MODEL)),
            _const_spec((MB_WIDTH, D_MODEL)),
            _const_spec((HG_WIDTH, D_MODEL)),
            _const_spec((D_MODEL, D_MODEL)),
            _const_spec((D_MODEL, 2 * D_FF)),
            _const_spec((D_FF, D_MODEL)),
            _const_spec((PLE_DIM, D_MODEL)),
            _const_spec((D_MODEL, D_MODEL)),
        ],
        out_specs=row(D_MODEL),
        out_shape=jax.ShapeDtypeStruct((t, D_MODEL), F32),
        scratch_shapes=[pltpu.VMEM((TOKEN_TILE, D_FF), BF16)],
        compiler_params=_params(("parallel",)),
        name="post",
    )(h, o_a, o_b, gates, p, gains, w_a, w_b, w_out, w_gu, w_down, w_ple, w_ple_gate)


def _bucket_starts():
    n = np.arange(0, 2 * MB_BLOCK, dtype=np.int32)
    nf = np.maximum(n, 1).astype(np.float32)
    large = REL_MAX_EXACT + (
        np.log(nf / np.float32(REL_MAX_EXACT)) / np.float32(math.log(REL_MAX_DIST / REL_MAX_EXACT))
        * np.float32(REL_BUCKETS - REL_MAX_EXACT)).astype(np.int32)
    large = np.minimum(large, REL_BUCKETS - 1)
    bucket = np.where(n < REL_MAX_EXACT, n, large)
    assert np.all(np.diff(bucket) >= 0) and bucket[-1] == REL_BUCKETS - 1
    return [int(np.argmax(bucket >= b)) for b in range(REL_BUCKETS)]


def _moba_kernel(starts, rel_ref, q_ref, k_ref, v_ref, o_ref,
                 kb_ref, vt_ref, kmean_ref, bown_ref, bprev_ref, sel_ref, qs_ref,
                 m_ref, acc_ref, sb0_ref, sb1_ref, mb0_ref, mb1_ref, sn_ref, mbn_ref):
    pair = pl.program_id(1)
    step = pl.program_id(2)
    seq = k_ref.shape[1]
    nb = seq // MB_BLOCK
    hd = MB_HEAD_DIM
    qscale = (hd ** -0.5) * LOG2E

    @pl.when(step == 0)
    def _prepare():
        kb_ref[0:MB_BLOCK, :] = jnp.zeros((MB_BLOCK, LANES), BF16)
        kb_ref[MB_BLOCK + seq:, :] = jnp.zeros((PAD_END, LANES), BF16)
        sel_ref[:, nb:, :] = jnp.full((2, PAD_BLOCKS, MB_QTILE), NEG, F32)
        for hh in range(2):
            vt_ref[hh, :, 0:MB_BLOCK] = jnp.zeros((hd + 16, MB_BLOCK), BF16)
            vt_ref[hh, :, MB_BLOCK + seq:] = jnp.zeros((hd + 16, PAD_END), BF16)
            vt_ref[hh, hd:hd + 16, MB_BLOCK:MB_BLOCK + seq] = jnp.ones((16, seq), BF16)

        def blk(j, carry):
            r0 = pl.multiple_of(j * MB_BLOCK, MB_BLOCK)
            r1 = pl.multiple_of(r0 + MB_BLOCK, MB_BLOCK)
            kblk = k_ref[0, pl.ds(r0, MB_BLOCK), :]
            kb_ref[pl.ds(r1, MB_BLOCK), :] = kblk.astype(BF16)
            kmean_ref[pl.ds(j, 1), :] = jnp.mean(kblk, axis=0, keepdims=True)
            vt = v_ref[0, pl.ds(r0, MB_BLOCK), :].T.astype(BF16)
            vt_ref[0, 0:hd, pl.ds(r1, MB_BLOCK)] = vt[0:hd]
            vt_ref[1, 0:hd, pl.ds(r1, MB_BLOCK)] = vt[hd:2 * hd]
            return carry

        lax.fori_loop(0, nb, blk, 0)

        kk = lax.broadcasted_iota(jnp.int32, (MB_BLOCK, MB_BLOCK), 0)
        qq = lax.broadcasted_iota(jnp.int32, (MB_BLOCK, MB_BLOCK), 1)
        d_own = qq - kk
        d_prev = d_own + MB_BLOCK
        for hh in range(2):
            head = pair * 2 + hh
            b_own = jnp.full((MB_BLOCK, MB_BLOCK), rel_ref[0, head], F32)
            b_prev = jnp.full((MB_BLOCK, MB_BLOCK), rel_ref[0, head], F32)
            for b in range(1, REL_BUCKETS):
                val = rel_ref[b, head]
                b_own = jnp.where(d_own >= starts[b], val, b_own)
                b_prev = jnp.where(d_prev >= starts[b], val, b_prev)
            bown_ref[hh] = jnp.where(d_own >= 0, b_own * LOG2E, NEG)
            bprev_ref[hh] = b_prev * LOG2E

    nq = MB_QTILE // MB_BLOCK
    qb0 = step * nq
    c_far = [rel_ref[REL_BUCKETS - 1, pair * 2 + hh] * LOG2E for hh in range(2)]

    km = kmean_ref[...]
    lane_head = lax.broadcasted_iota(jnp.int32, (nb, 2 * hd), 1) // hd
    km_hi, km_mid, km_lo = _split3(jnp.concatenate(
        [jnp.where(lane_head == hh, km, 0.0) for hh in range(2)], axis=0))
    qt = q_ref[0].T
    q_hi, q_mid, q_lo = _split3(qt)
    gates = (_dot(km_hi, q_hi) + (_dot(km_hi, q_mid) + _dot(km_mid, q_hi))
             + (_dot(km_hi, q_lo) + _dot(km_mid, q_mid) + _dot(km_lo, q_hi)))
    row_head = lax.broadcasted_iota(jnp.int32, (2 * hd, MB_QTILE), 0) // hd
    blk_row = lax.broadcasted_iota(jnp.int32, (nb, MB_QTILE), 0)
    own_blk = qb0 + lax.broadcasted_iota(jnp.int32, (nb, MB_QTILE), 1) // MB_BLOCK
    for hh in range(2):
        qs = (jnp.where(row_head == hh, qt, 0.0) * qscale).astype(BF16)
        for qi in range(nq):
            c0 = (2 * qi + hh) * MB_BLOCK
            qs_ref[:, c0:c0 + MB_BLOCK] = qs[:, qi * MB_BLOCK:(qi + 1) * MB_BLOCK]
        gate = gates[hh * nb:(hh + 1) * nb]
        gate = jnp.where(blk_row < own_blk, gate, -jnp.inf)
        sel = jnp.full((nb, MB_QTILE), NEG, F32)
        for _ in range(MB_TOPK):
            mx = jnp.max(gate, axis=0, keepdims=True)
            cand = jnp.where((gate == mx) & (mx > -jnp.inf), blk_row, nb)
            pick = blk_row == jnp.min(cand, axis=0, keepdims=True)
            sel = jnp.where(pick, 0.0, sel)
            gate = jnp.where(pick, -jnp.inf, gate)
        sel_ref[hh, 0:nb, :] = sel

    def key_slice(blocks):
        return pl.ds(pl.multiple_of(blocks[0] * MB_BLOCK, MB_BLOCK), blocks[1] * MB_BLOCK)

    def score(qi, blocks, mats, sb, mb):
        s_all = _dot(kb_ref[key_slice(blocks), :],
                     qs_ref[:, 2 * qi * MB_BLOCK:2 * (qi + 1) * MB_BLOCK])
        slots = mb.shape[0] // (2 * nq)
        for hh in range(2):
            c = 2 * qi + hh
            for u in range(blocks[1]):
                piece = s_all[u * MB_BLOCK:(u + 1) * MB_BLOCK, hh * MB_BLOCK:(hh + 1) * MB_BLOCK]
                if mats[hh][u] is not None:
                    piece = piece + mats[hh][u]
                mb[c * slots + u:c * slots + u + 1, :] = jnp.max(piece, axis=0, keepdims=True)
                sb[u * MB_BLOCK:(u + 1) * MB_BLOCK, c * MB_BLOCK:(c + 1) * MB_BLOCK] = piece

    def consume(qi, blocks, rows, sb, mb, first):
        kpos = key_slice(blocks)
        slots = mb.shape[0] // (2 * nq)
        for hh in range(2):
            c = 2 * qi + hh
            m_blk = None
            for u in range(blocks[1]):
                m_u = mb[c * slots + u:c * slots + u + 1, :]
                if rows[hh][u] is not None:
                    m_u = m_u + rows[hh][u]
                m_blk = m_u if m_blk is None else jnp.maximum(m_blk, m_u)
            m_old = None if first else m_ref[c:c + 1, :]
            m_new = m_blk if first else jnp.maximum(m_old, m_blk)
            p = jnp.concatenate(
                [jnp.exp2(sb[u * MB_BLOCK:(u + 1) * MB_BLOCK, c * MB_BLOCK:(c + 1) * MB_BLOCK]
                          + ((-m_new) if rows[hh][u] is None else (rows[hh][u] - m_new)))
                 for u in range(blocks[1])], axis=0)
            pv = _dot(vt_ref[hh, :, kpos], p.astype(BF16))
            acc_ref[c] = pv if first else acc_ref[c] * jnp.exp2(m_old - m_new) + pv
            m_ref[c:c + 1, :] = m_new

    def far_row(hh, qi, j, limit):
        row = sel_ref[hh, pl.ds(jnp.maximum(j, 0), 1), qi * MB_BLOCK:(qi + 1) * MB_BLOCK]
        return jnp.where((j >= 0) & (j < limit), row + c_far[hh], NEG)

    def near_blocks(qi):
        return qb0, qi + 2

    def score_near(qi):
        mats = [[None] * qi + [bprev_ref[hh], bown_ref[hh]] for hh in range(2)]
        score(qi, near_blocks(qi), mats, sn_ref, mbn_ref)

    def consume_near(qi):
        qb = qb0 + qi
        rows = []
        for hh in range(2):
            far = [far_row(hh, qi, qb0 - 1 + u, qb - 1) for u in range(qi)]
            prev = sel_ref[hh, pl.ds(jnp.maximum(qb - 1, 0), 1), qi * MB_BLOCK:(qi + 1) * MB_BLOCK]
            rows.append(far + [jnp.where(qb >= 1, prev, NEG), None])
        consume(qi, near_blocks(qi), rows, sn_ref, mbn_ref, True)

    nfar = qb0 - 1
    chains = [(qi, hh) for qi in range(nq) for hh in range(2)]

    def far_blocks(g, n=FAR_GROUP):
        return g * FAR_GROUP + 1, n

    def score_far(g, qi, sb, mb, n=FAR_GROUP):
        score(qi, far_blocks(g, n), [[None] * n] * 2, sb, mb)

    def consume_far(g, qi, sb, mb, n=FAR_GROUP):
        rows = [[far_row(hh, qi, g * FAR_GROUP + u, nfar) for u in range(n)]
                for hh in range(2)]
        consume(qi, far_blocks(g, n), rows, sb, mb, False)

    score_near(0)
    for qi in range(1, nq):
        score_near(qi)
        consume_near(qi - 1)
    score_far(0, 0, sb0_ref, mb0_ref)
    consume_near(nq - 1)
    for qi in range(1, nq):
        score_far(0, qi, sb0_ref, mb0_ref)

    assert nq % (2 * FAR_GROUP) == 0 and FAR_GROUP > 1

    def two_groups(t, last):
        n_last = FAR_GROUP - 1 if last else FAR_GROUP
        for qi in range(nq):
            consume_far(2 * t, qi, sb0_ref, mb0_ref)
            score_far(2 * t + 1, qi, sb1_ref, mb1_ref, n_last)
        for qi in range(nq):
            consume_far(2 * t + 1, qi, sb1_ref, mb1_ref, n_last)
            if not last:
                score_far(2 * t + 2, qi, sb0_ref, mb0_ref)

    trips = (jnp.maximum(nfar, 0) + 2 * FAR_GROUP - 1) // (2 * FAR_GROUP)
    lax.fori_loop(0, trips - 1, lambda t, carry: (two_groups(t, False), carry)[1], 0)

    @pl.when(trips > 0)
    def _last_trip():
        two_groups(trips - 1, True)

    for qi in range(nq):
        outs = [acc_ref[2 * qi + hh, 0:hd, :] / acc_ref[2 * qi + hh, hd:hd + 1, :]
                for hh in range(2)]
        o_ref[0, qi * MB_BLOCK:(qi + 1) * MB_BLOCK, :] = jnp.concatenate(outs, axis=0).T


def _moba(q, k, v, rel_table):
    bsz, seq, _ = q.shape
    assert seq % MB_QTILE == 0
    nb = seq // MB_BLOCK
    npair = MB_HEADS // 2
    padded = MB_BLOCK + seq + PAD_END
    chains = 2 * MB_QTILE // MB_BLOCK
    return pl.pallas_call(
        functools.partial(_moba_kernel, _bucket_starts()),
        grid=(bsz, npair, seq // MB_QTILE),
        in_specs=[
            pl.BlockSpec(memory_space=pltpu.SMEM),
            pl.BlockSpec((1, MB_QTILE, LANES), lambda b, p, t: (b, t, p)),
            pl.BlockSpec((1, seq, LANES), lambda b, p, t: (b, 0, p)),
            pl.BlockSpec((1, seq, LANES), lambda b, p, t: (b, 0, p)),
        ],
        out_specs=pl.BlockSpec((1, MB_QTILE, LANES), lambda b, p, t: (b, t, p)),
        out_shape=jax.ShapeDtypeStruct((bsz, seq, MB_WIDTH), F32),
        scratch_shapes=[
            pltpu.VMEM((padded, LANES), BF16),
            pltpu.VMEM((2, MB_HEAD_DIM + 16, padded), BF16),
            pltpu.VMEM((nb, LANES), F32),
            pltpu.VMEM((2, MB_BLOCK, MB_BLOCK), F32),
            pltpu.VMEM((2, MB_BLOCK, MB_BLOCK), F32),
            pltpu.VMEM((2, nb + PAD_BLOCKS, MB_QTILE), F32),
            pltpu.VMEM((LANES, 2 * MB_QTILE), BF16),
            pltpu.VMEM((2 * MB_QTILE // MB_BLOCK, MB_BLOCK), F32),
            pltpu.VMEM((2 * MB_QTILE // MB_BLOCK, MB_HEAD_DIM + 16, MB_BLOCK), F32),
            pltpu.VMEM((FAR_GROUP * MB_BLOCK, 2 * MB_QTILE), F32),
            pltpu.VMEM((FAR_GROUP * MB_BLOCK, 2 * MB_QTILE), F32),
            pltpu.VMEM((chains * FAR_GROUP, MB_BLOCK), F32),
            pltpu.VMEM((chains * FAR_GROUP, MB_BLOCK), F32),
            pltpu.VMEM((MB_QTILE + MB_BLOCK, 2 * MB_QTILE), F32),
            pltpu.VMEM((chains * (chains // 2 + 1), MB_BLOCK), F32),
        ],
        compiler_params=_params(("parallel", "parallel", "arbitrary")),
        name="moba",
    )(rel_table, q, k, v)


def _segment_ref(b, n):
    c = b.shape[0]
    if 2 * n >= SUBLANES:
        b3 = b.reshape(c // (2 * n), 2 * n, b.shape[1])
        return jnp.broadcast_to(b3[:, n - 1:n, :], b3.shape).reshape(b.shape)
    b3 = b.reshape(c // SUBLANES, SUBLANES, b.shape[1])
    seg = lax.broadcasted_iota(jnp.int32, b3.shape, 1) // (2 * n)
    out = jnp.broadcast_to(b3[:, n - 1:n, :], b3.shape)
    for s in range(1, SUBLANES // (2 * n)):
        r = s * 2 * n + n - 1
        out = jnp.where(seg == s, jnp.broadcast_to(b3[:, r:r + 1, :], b3.shape), out)
    return out.reshape(b.shape)


def _hgrn_kernel(q_ref, f_ref, v_ref, g_ref, nw_ref, o_ref, state_ref):
    c = HG_CHUNK

    @pl.when(pl.program_id(1) == 0)
    def _reset():
        state_ref[...] = jnp.zeros_like(state_ref)

    half = c // 2
    row = lax.broadcasted_iota(jnp.int32, (c, c), 0)
    col = lax.broadcasted_iota(jnp.int32, (c, c), 1)
    tril = jnp.where(row >= col, 1.0, 0.0).astype(BF16)
    rh = lax.broadcasted_iota(jnp.int32, (half, half), 0)
    ch = lax.broadcasted_iota(jnp.int32, (half, half), 1)
    split = jnp.where(rh > ch, rh ^ ch, 0)
    levels = [1 << i for i in range(int(math.log2(half)))]

    def chunk_head(rows, hh):
        sl = slice(hh * HG_DK, (hh + 1) * HG_DK)
        f = f_ref[0, rows, sl]
        logf = jnp.log(f)
        kk = 1.0 - f
        q = q_ref[0, rows, sl]
        v = v_ref[0, rows, sl]
        vb = v.astype(BF16)
        qb = q.astype(BF16)
        kb = kk.astype(BF16)

        l_hi, l_mid, l_lo = _split3(logf * LOG2E)
        b = _dot(tril, l_hi) + (_dot(tril, l_mid) + _dot(tril, l_lo))

        def level_operands(n):
            if n == 1:
                return qb * f.astype(BF16), kb
            e_l = jnp.exp2(-jnp.abs(b - _segment_ref(b, n))).astype(BF16)
            return qb * e_l, kb * e_l

        d_lo = jnp.zeros((half, half), F32)
        d_hi = jnp.zeros((half, half), F32)
        for n in levels:
            q_l, k_l = level_operands(n)
            d_lo = jnp.where(split >= n, _dot_nt(q_l[:half], k_l[:half]), d_lo)
            d_hi = jnp.where(split >= n, _dot_nt(q_l[half:], k_l[half:]), d_hi)
        q_l, k_l = level_operands(half)
        cross = _dot_nt(q_l[half:], k_l[:half])

        st = state_ref[hh]
        intra = jnp.concatenate(
            [_dot(d_lo.astype(BF16), vb[:half]),
             _dot(jnp.concatenate([cross, d_hi], axis=1).astype(BF16), vb)], axis=0)
        o = (intra + jnp.sum(q * kk, axis=-1, keepdims=True) * v
             + _dot_nt((q * jnp.exp2(b)).astype(BF16), st.astype(BF16)))
        b_last = b[c - 1:c, :]
        k_dec = (kk * jnp.exp2(b_last - b)).astype(BF16)
        state_ref[hh] = st * jnp.exp2(b_last) + lax.dot_general(
            vb, k_dec, (((0,), (0,)), ((), ())), preferred_element_type=F32)

        o_ref[0, rows, sl] = _rms(o, nw_ref[:, :]) * g_ref[0, rows, sl]

    for sub in range(HG_STEP_CHUNKS):
        for hh in range(HG_HEADS):
            chunk_head(slice(sub * c, (sub + 1) * c), hh)


def _hgrn(q, f, v, g, norm_w):
    bsz, seq, _ = q.shape
    step_rows = HG_STEP_CHUNKS * HG_CHUNK
    assert seq % step_rows == 0
    blk = pl.BlockSpec((1, step_rows, HG_WIDTH), lambda b, t: (b, t, 0))
    return pl.pallas_call(
        _hgrn_kernel,
        grid=(bsz, seq // step_rows),
        in_specs=[blk, blk, blk, blk, pl.BlockSpec((1, HG_DV), lambda b, t: (0, 0))],
        out_specs=blk,
        out_shape=jax.ShapeDtypeStruct((bsz, seq, HG_WIDTH), F32),
        scratch_shapes=[pltpu.VMEM((HG_HEADS, HG_DV, HG_DK), F32)],
        compiler_params=_params(("parallel", "arbitrary")),
        name="hgrn",
    )(q, f, v, g, norm_w)


def kernel(x, p, w_ffn1_gu, w_ffn1_down, w_in, w_branch_a, w_branch_b, w_out,
           w_ffn2_gu, w_ffn2_down, w_ple, w_ple_gate, norm_gains, hg_norm_w,
           lb_param, rel_table):
    bsz, seq, _ = x.shape
    t = bsz * seq
    assert t % TOKEN_TILE == 0
    h = x.reshape(t, D_MODEL)
    for i in range(p.shape[0]):
        g = norm_gains[i]
        h = _ffn(h, g, w_ffn1_gu[i].astype(BF16), w_ffn1_down[i].astype(BF16), 0, 1)
        mq, mk, mv, hq, hf, hi, hg, gates = _inproj(h, g, lb_param, w_in[i].astype(BF16), i)
        r3 = lambda a: a.reshape(bsz, seq, a.shape[-1])
        o_a = _moba(r3(mq), r3(mk), r3(mv), rel_table)
        o_b = _hgrn(r3(hq), r3(hf), r3(hi), r3(hg), hg_norm_w[i:i + 1])
        h = _post(h, o_a.reshape(t, MB_WIDTH), o_b.reshape(t, HG_WIDTH), gates,
                  p[i].reshape(t, PLE_DIM), g,
                  w_branch_a[i].astype(BF16), w_branch_b[i].astype(BF16), w_out[i].astype(BF16),
                  w_ffn2_gu[i].astype(BF16), w_ffn2_down[i].astype(BF16),
                  w_ple[i].astype(BF16), w_ple_gate[i].astype(BF16))
    return h.reshape(bsz, seq, D_MODEL)
```

```python
import functools
import math

import numpy as np
import jax
import jax.numpy as jnp
from jax import lax
from jax.experimental import pallas as pl
from jax.experimental.pallas import tpu as pltpu

F32 = jnp.float32
BF16 = jnp.bfloat16

D_MODEL = 1024
PLE_DIM = 256
D_FF = 2816
MB_HEADS = 8
MB_HEAD_DIM = 64
MB_WIDTH = MB_HEADS * MB_HEAD_DIM
MB_BLOCK = 256
MB_TOPK = 3
HG_HEADS = 4
HG_DK = 128
HG_DV = 128
HG_WIDTH = HG_HEADS * HG_DV
REL_BUCKETS = 32
REL_MAX_EXACT = REL_BUCKETS // 2
REL_MAX_DIST = 128
N_NORMS = 7
EPS = 1e-6

LANES = 128
SUBLANES = 8
VMEM_LIMIT = 56 * 1024 * 1024
TOKEN_TILE = 512
FFN_TILE = 1024
FF_CHUNK = 256
HG_CHUNK = 256
HG_STEP_CHUNKS = 4
MB_QTILE = 1024
FAR_GROUP = 2
PAD_BLOCKS = 3 * FAR_GROUP
PAD_END = PAD_BLOCKS * 256
NEG = -1e30
LOG2E = 1.4426950408889634


def _rms(x, g):
    return x * lax.rsqrt(jnp.mean(x * x, axis=-1, keepdims=True) + EPS) * g


def _sigmoid(x):
    return 1.0 / (1.0 + jnp.exp(-x))


def _dot(a, b):
    return jnp.dot(a, b, preferred_element_type=F32)


def _dot_nt(a, b):
    return lax.dot_general(a, b, (((1,), (1,)), ((), ())), preferred_element_type=F32)


def _split3(x):
    def top(y):
        bits = pltpu.bitcast(y, jnp.uint32) & jnp.uint32(0xFFFF0000)
        return pltpu.bitcast(bits, F32)
    hi = top(x)
    r1 = x - hi
    mid = top(r1)
    return hi.astype(BF16), mid.astype(BF16), (r1 - mid).astype(BF16)


def _const_spec(shape):
    nd = len(shape)
    return pl.BlockSpec(shape, lambda *_: (0,) * nd, pipeline_mode=pl.Buffered(1))


def _params(sem, flags=None):
    return pltpu.CompilerParams(dimension_semantics=sem, vmem_limit_bytes=VMEM_LIMIT, flags=flags)


def _ffn_step(x, g_ref, pre, post, wgu_ref, wd_ref, act_ref):
    u = _rms(x, g_ref[pre:pre + 1, :]).astype(BF16)
    for c in range(D_FF // FF_CHUNK):
        lo = c * FF_CHUNK
        g = _dot(u, wgu_ref[:, lo:lo + FF_CHUNK])
        v = _dot(u, wgu_ref[:, D_FF + lo:D_FF + lo + FF_CHUNK])
        act_ref[:, lo:lo + FF_CHUNK] = (g * _sigmoid(g) * v).astype(BF16)
    y = _dot(act_ref[...], wd_ref[...])
    return x + 0.5 * _rms(y, g_ref[post:post + 1, :])


def _ffn_kernel(pre, post, x_ref, g_ref, wgu_ref, wd_ref, o_ref, act_ref):
    o_ref[...] = _ffn_step(x_ref[...], g_ref, pre, post, wgu_ref, wd_ref, act_ref)


def _ffn(x, gains, w_gu, w_down, pre, post):
    t = x.shape[0]
    return pl.pallas_call(
        functools.partial(_ffn_kernel, pre, post),
        grid=(t // FFN_TILE,),
        in_specs=[
            pl.BlockSpec((FFN_TILE, D_MODEL), lambda i: (i, 0)),
            _const_spec((N_NORMS, D_MODEL)),
            _const_spec((D_MODEL, 2 * D_FF)),
            _const_spec((D_FF, D_MODEL)),
        ],
        out_specs=pl.BlockSpec((FFN_TILE, D_MODEL), lambda i: (i, 0)),
        out_shape=jax.ShapeDtypeStruct((t, D_MODEL), F32),
        scratch_shapes=[pltpu.VMEM((FFN_TILE, D_FF), BF16)],
        compiler_params=_params(("parallel",)),
        name="ffn",
    )(x, gains, w_gu, w_down)


IN_SPLITS = (MB_WIDTH, MB_WIDTH, MB_WIDTH, HG_WIDTH, HG_WIDTH, HG_WIDTH, HG_WIDTH, 2 * D_MODEL)
IN_COLS = sum(IN_SPLITS)


HG_Q, HG_F, HG_G = 3, 4, 6


def _inproj_kernel(layer, h_ref, g_ref, lbp_ref, w_ref, *out_refs):
    u = _rms(h_ref[...], g_ref[2:3, :]).astype(BF16)
    lp = lbp_ref[...]
    e = jnp.exp(lp - jnp.max(lp, axis=0, keepdims=True))
    sm = e / jnp.sum(e, axis=0, keepdims=True)
    lb = sm[0:1, :]
    for r in range(1, layer + 1):
        lb = lb + sm[r:r + 1, :]
    off = 0
    for idx, ref in enumerate(out_refs):
        n = ref.shape[-1]
        for c in range(0, n, 512):
            z = _dot(u, w_ref[:, off + c:off + c + 512])
            if idx in (HG_Q, HG_G):
                z = z * _sigmoid(z)
            elif idx == HG_F:
                z = lb + (1.0 - lb) * _sigmoid(z)
            ref[:, c:c + 512] = z
        off += n


def _inproj(h, gains, lb_param, w_in, layer):
    t = h.shape[0]
    return pl.pallas_call(
        functools.partial(_inproj_kernel, layer),
        grid=(t // TOKEN_TILE,),
        in_specs=[
            pl.BlockSpec((TOKEN_TILE, D_MODEL), lambda i: (i, 0)),
            _const_spec((N_NORMS, D_MODEL)),
            _const_spec(lb_param.shape),
            _const_spec((D_MODEL, IN_COLS)),
        ],
        out_specs=[pl.BlockSpec((TOKEN_TILE, n), lambda i: (i, 0)) for n in IN_SPLITS],
        out_shape=[jax.ShapeDtypeStruct((t, n), F32) for n in IN_SPLITS],
        compiler_params=_params(("parallel",)),
        name="in_proj",
    )(h, gains, lb_param, w_in)


def _post_kernel(h_ref, oa_ref, ob_ref, gg_ref, p_ref, g_ref, wa_ref, wb_ref, wo_ref,
                 wgu_ref, wd_ref, wp_ref, wg_ref, o_ref, act_ref):
    a = _dot(oa_ref[...].astype(BF16), wa_ref[...])
    b = _dot(ob_ref[...].astype(BF16), wb_ref[...])
    merged = _sigmoid(gg_ref[:, :D_MODEL]) * a + _sigmoid(gg_ref[:, D_MODEL:]) * b
    h = h_ref[...] + _rms(_dot(merged.astype(BF16), wo_ref[...]), g_ref[3:4, :])
    h = _ffn_step(h, g_ref, 4, 5, wgu_ref, wd_ref, act_ref)
    e = _dot(p_ref[...].astype(BF16), wp_ref[...])
    t = _dot(h.astype(BF16), wg_ref[...])
    o_ref[...] = h + _rms(_sigmoid(t) * e, g_ref[6:7, :])


def _post(h, o_a, o_b, gates, p, gains, w_a, w_b, w_out, w_gu, w_down, w_ple, w_ple_gate):
    t = h.shape[0]
    row = lambda n: pl.BlockSpec((TOKEN_TILE, n), lambda i: (i, 0))
    return pl.pallas_call(
        _post_kernel,
        grid=(t // TOKEN_TILE,),
        in_specs=[
            row(D_MODEL), row(MB_WIDTH), row(HG_WIDTH), row(2 * D_MODEL), row(PLE_DIM),
            _const_spec((N_NORMS, D_MODEL)),
            _const_spec((MB_WIDTH, D_MODEL)),
            _const_spec((HG_WIDTH, D_MODEL)),
            _const_spec((D_MODEL, D_MODEL)),
            _const_spec((D_MODEL, 2 * D_FF)),
            _const_spec((D_FF, D_MODEL)),
            _const_spec((PLE_DIM, D_MODEL)),
            _const_spec((D_MODEL, D_MODEL)),
        ],
        out_specs=row(D_MODEL),
        out_shape=jax.ShapeDtypeStruct((t, D_MODEL), F32),
        scratch_shapes=[pltpu.VMEM((TOKEN_TILE, D_FF), BF16)],
        compiler_params=_params(("parallel",)),
        name="post",
    )(h, o_a, o_b, gates, p, gains, w_a, w_b, w_out, w_gu, w_down, w_ple, w_ple_gate)


def _bucket_starts():
    n = np.arange(0, 2 * MB_BLOCK, dtype=np.int32)
    nf = np.maximum(n, 1).astype(np.float32)
    large = REL_MAX_EXACT + (
        np.log(nf / np.float32(REL_MAX_EXACT)) / np.float32(math.log(REL_MAX_DIST / REL_MAX_EXACT))
        * np.float32(REL_BUCKETS - REL_MAX_EXACT)).astype(np.int32)
    large = np.minimum(large, REL_BUCKETS - 1)
    bucket = np.where(n < REL_MAX_EXACT, n, large)
    assert np.all(np.diff(bucket) >= 0) and bucket[-1] == REL_BUCKETS - 1
    return [int(np.argmax(bucket >= b)) for b in range(REL_BUCKETS)]


def _moba_kernel(starts, rel_ref, q_ref, k_ref, v_ref, o_ref,
                 kb_ref, vt_ref, kmean_ref, bown_ref, bprev_ref, sel_ref, qs_ref,
                 m_ref, acc_ref, sb0_ref, sb1_ref, mb0_ref, mb1_ref, sn_ref, mbn_ref):
    pair = pl.program_id(1)
    step = pl.program_id(2)
    seq = k_ref.shape[1]
    nb = seq // MB_BLOCK
    hd = MB_HEAD_DIM
    qscale = (hd ** -0.5) * LOG2E

    @pl.when(step == 0)
    def _prepare():
        kb_ref[0:MB_BLOCK, :] = jnp.zeros((MB_BLOCK, LANES), BF16)
        kb_ref[MB_BLOCK + seq:, :] = jnp.zeros((PAD_END, LANES), BF16)
        sel_ref[:, nb:, :] = jnp.full((2, PAD_BLOCKS, MB_QTILE), NEG, F32)
        for hh in range(2):
            vt_ref[hh, :, 0:MB_BLOCK] = jnp.zeros((hd + 16, MB_BLOCK), BF16)
            vt_ref[hh, :, MB_BLOCK + seq:] = jnp.zeros((hd + 16, PAD_END), BF16)
            vt_ref[hh, hd:hd + 16, MB_BLOCK:MB_BLOCK + seq] = jnp.ones((16, seq), BF16)

        def blk(j, carry):
            r0 = pl.multiple_of(j * MB_BLOCK, MB_BLOCK)
            r1 = pl.multiple_of(r0 + MB_BLOCK, MB_BLOCK)
            kblk = k_ref[0, pl.ds(r0, MB_BLOCK), :]
            kb_ref[pl.ds(r1, MB_BLOCK), :] = kblk.astype(BF16)
            kmean_ref[pl.ds(j, 1), :] = jnp.mean(kblk, axis=0, keepdims=True)
            vt = v_ref[0, pl.ds(r0, MB_BLOCK), :].T.astype(BF16)
            vt_ref[0, 0:hd, pl.ds(r1, MB_BLOCK)] = vt[0:hd]
            vt_ref[1, 0:hd, pl.ds(r1, MB_BLOCK)] = vt[hd:2 * hd]
            return carry

        lax.fori_loop(0, nb, blk, 0, unroll=4)

        kk = lax.broadcasted_iota(jnp.int32, (MB_BLOCK, MB_BLOCK), 0)
        qq = lax.broadcasted_iota(jnp.int32, (MB_BLOCK, MB_BLOCK), 1)
        d_own = qq - kk
        d_prev = d_own + MB_BLOCK
        for hh in range(2):
            head = pair * 2 + hh
            b_own = jnp.full((MB_BLOCK, MB_BLOCK), rel_ref[0, head], F32)
            b_prev = jnp.full((MB_BLOCK, MB_BLOCK), rel_ref[0, head], F32)
            for b in range(1, REL_BUCKETS):
                val = rel_ref[b, head]
                b_own = jnp.where(d_own >= starts[b], val, b_own)
                b_prev = jnp.where(d_prev >= starts[b], val, b_prev)
            bown_ref[hh] = jnp.where(d_own >= 0, b_own * LOG2E, NEG)
            bprev_ref[hh] = b_prev * LOG2E

    nq = MB_QTILE // MB_BLOCK
    qb0 = step * nq
    c_far = [rel_ref[REL_BUCKETS - 1, pair * 2 + hh] * LOG2E for hh in range(2)]

    km = kmean_ref[...]
    lane_head = lax.broadcasted_iota(jnp.int32, (nb, 2 * hd), 1) // hd
    km_hi, km_mid, km_lo = _split3(jnp.concatenate(
        [jnp.where(lane_head == hh, km, 0.0) for hh in range(2)], axis=0))
    qt = q_ref[0].T
    q_hi, q_mid, q_lo = _split3(qt)
    gates = (_dot(km_hi, q_hi) + (_dot(km_hi, q_mid) + _dot(km_mid, q_hi))
             + (_dot(km_hi, q_lo) + _dot(km_mid, q_mid) + _dot(km_lo, q_hi)))
    row_head = lax.broadcasted_iota(jnp.int32, (2 * hd, MB_QTILE), 0) // hd
    blk_row = lax.broadcasted_iota(jnp.int32, (nb, MB_QTILE), 0)
    own_blk = qb0 + lax.broadcasted_iota(jnp.int32, (nb, MB_QTILE), 1) // MB_BLOCK
    for hh in range(2):
        qs = (jnp.where(row_head == hh, qt, 0.0) * qscale).astype(BF16)
        for qi in range(nq):
            c0 = (2 * qi + hh) * MB_BLOCK
            qs_ref[:, c0:c0 + MB_BLOCK] = qs[:, qi * MB_BLOCK:(qi + 1) * MB_BLOCK]
        gate = gates[hh * nb:(hh + 1) * nb]
        gate = jnp.where(blk_row < own_blk, gate, -jnp.inf)
        sel = jnp.full((nb, MB_QTILE), NEG, F32)
        for _ in range(MB_TOPK):
            mx = jnp.max(gate, axis=0, keepdims=True)
            cand = jnp.where((gate == mx) & (mx > -jnp.inf), blk_row, nb)
            pick = blk_row == jnp.min(cand, axis=0, keepdims=True)
            sel = jnp.where(pick, 0.0, sel)
            gate = jnp.where(pick, -jnp.inf, gate)
        sel_ref[hh, 0:nb, :] = sel

    def key_slice(blocks):
        return pl.ds(pl.multiple_of(blocks[0] * MB_BLOCK, MB_BLOCK), blocks[1] * MB_BLOCK)

    def score(qi, blocks, mats, sb, mb):
        s_all = _dot(kb_ref[key_slice(blocks), :],
                     qs_ref[:, 2 * qi * MB_BLOCK:2 * (qi + 1) * MB_BLOCK])
        slots = mb.shape[0] // (2 * nq)
        for hh in range(2):
            c = 2 * qi + hh
            for u in range(blocks[1]):
                piece = s_all[u * MB_BLOCK:(u + 1) * MB_BLOCK, hh * MB_BLOCK:(hh + 1) * MB_BLOCK]
                if mats[hh][u] is not None:
                    piece = piece + mats[hh][u]
                mb[c * slots + u:c * slots + u + 1, :] = jnp.max(piece, axis=0, keepdims=True)
                sb[u * MB_BLOCK:(u + 1) * MB_BLOCK, c * MB_BLOCK:(c + 1) * MB_BLOCK] = piece

    def consume(qi, blocks, rows, sb, mb, first):
        kpos = key_slice(blocks)
        slots = mb.shape[0] // (2 * nq)
        for hh in range(2):
            c = 2 * qi + hh
            m_blk = None
            for u in range(blocks[1]):
                m_u = mb[c * slots + u:c * slots + u + 1, :]
                if rows[hh][u] is not None:
                    m_u = m_u + rows[hh][u]
                m_blk = m_u if m_blk is None else jnp.maximum(m_blk, m_u)
            m_old = None if first else m_ref[c:c + 1, :]
            m_new = m_blk if first else jnp.maximum(m_old, m_blk)
            p = jnp.concatenate(
                [jnp.exp2(sb[u * MB_BLOCK:(u + 1) * MB_BLOCK, c * MB_BLOCK:(c + 1) * MB_BLOCK]
                          + ((-m_new) if rows[hh][u] is None else (rows[hh][u] - m_new)))
                 for u in range(blocks[1])], axis=0)
            pv = _dot(vt_ref[hh, :, kpos], p.astype(BF16))
            acc_ref[c] = pv if first else acc_ref[c] * jnp.exp2(m_old - m_new) + pv
            m_ref[c:c + 1, :] = m_new

    def far_row(hh, qi, j, limit):
        row = sel_ref[hh, pl.ds(jnp.maximum(j, 0), 1), qi * MB_BLOCK:(qi + 1) * MB_BLOCK]
        return jnp.where((j >= 0) & (j < limit), row + c_far[hh], NEG)

    def near_blocks(qi):
        return qb0, qi + 2

    def score_near(qi):
        mats = [[None] * qi + [bprev_ref[hh], bown_ref[hh]] for hh in range(2)]
        score(qi, near_blocks(qi), mats, sn_ref, mbn_ref)

    def consume_near(qi):
        qb = qb0 + qi
        rows = []
        for hh in range(2):
            far = [far_row(hh, qi, qb0 - 1 + u, qb - 1) for u in range(qi)]
            prev = sel_ref[hh, pl.ds(jnp.maximum(qb - 1, 0), 1), qi * MB_BLOCK:(qi + 1) * MB_BLOCK]
            rows.append(far + [jnp.where(qb >= 1, prev, NEG), None])
        consume(qi, near_blocks(qi), rows, sn_ref, mbn_ref, True)

    nfar = qb0 - 1
    chains = [(qi, hh) for qi in range(nq) for hh in range(2)]

    def far_blocks(g, n=FAR_GROUP):
        return g * FAR_GROUP + 1, n

    def score_far(g, qi, sb, mb, n=FAR_GROUP):
        score(qi, far_blocks(g, n), [[None] * n] * 2, sb, mb)

    def consume_far(g, qi, sb, mb, n=FAR_GROUP):
        rows = [[far_row(hh, qi, g * FAR_GROUP + u, nfar) for u in range(n)]
                for hh in range(2)]
        consume(qi, far_blocks(g, n), rows, sb, mb, False)

    score_near(0)
    for qi in range(1, nq):
        score_near(qi)
        consume_near(qi - 1)
    score_far(0, 0, sb0_ref, mb0_ref)
    consume_near(nq - 1)
    for qi in range(1, nq):
        score_far(0, qi, sb0_ref, mb0_ref)

    assert nq % (2 * FAR_GROUP) == 0 and FAR_GROUP > 1

    def two_groups(t, last):
        n_last = FAR_GROUP - 1 if last else FAR_GROUP
        for qi in range(nq):
            consume_far(2 * t, qi, sb0_ref, mb0_ref)
            score_far(2 * t + 1, qi, sb1_ref, mb1_ref, n_last)
        for qi in range(nq):
            consume_far(2 * t + 1, qi, sb1_ref, mb1_ref, n_last)
            if not last:
                score_far(2 * t + 2, qi, sb0_ref, mb0_ref)

    trips = (jnp.maximum(nfar, 0) + 2 * FAR_GROUP - 1) // (2 * FAR_GROUP)
    lax.fori_loop(0, trips - 1, lambda t, carry: (two_groups(t, False), carry)[1], 0)

    @pl.when(trips > 0)
    def _last_trip():
        two_groups(trips - 1, True)

    for qi in range(nq):
        outs = [acc_ref[2 * qi + hh, 0:hd, :] / acc_ref[2 * qi + hh, hd:hd + 1, :]
                for hh in range(2)]
        o_ref[0, qi * MB_BLOCK:(qi + 1) * MB_BLOCK, :] = jnp.concatenate(outs, axis=0).T


def _moba(q, k, v, rel_table):
    bsz, seq, _ = q.shape
    assert seq % MB_QTILE == 0
    nb = seq // MB_BLOCK
    npair = MB_HEADS // 2
    padded = MB_BLOCK + seq + PAD_END
    chains = 2 * MB_QTILE // MB_BLOCK
    return pl.pallas_call(
        functools.partial(_moba_kernel, _bucket_starts()),
        grid=(bsz, npair, seq // MB_QTILE),
        in_specs=[
            pl.BlockSpec(memory_space=pltpu.SMEM),
            pl.BlockSpec((1, MB_QTILE, LANES), lambda b, p, t: (b, t, p)),
            pl.BlockSpec((1, seq, LANES), lambda b, p, t: (b, 0, p)),
            pl.BlockSpec((1, seq, LANES), lambda b, p, t: (b, 0, p)),
        ],
        out_specs=pl.BlockSpec((1, MB_QTILE, LANES), lambda b, p, t: (b, t, p)),
        out_shape=jax.ShapeDtypeStruct((bsz, seq, MB_WIDTH), F32),
        scratch_shapes=[
            pltpu.VMEM((padded, LANES), BF16),
            pltpu.VMEM((2, MB_HEAD_DIM + 16, padded), BF16),
            pltpu.VMEM((nb, LANES), F32),
            pltpu.VMEM((2, MB_BLOCK, MB_BLOCK), F32),
            pltpu.VMEM((2, MB_BLOCK, MB_BLOCK), F32),
            pltpu.VMEM((2, nb + PAD_BLOCKS, MB_QTILE), F32),
            pltpu.VMEM((LANES, 2 * MB_QTILE), BF16),
            pltpu.VMEM((2 * MB_QTILE // MB_BLOCK, MB_BLOCK), F32),
            pltpu.VMEM((2 * MB_QTILE // MB_BLOCK, MB_HEAD_DIM + 16, MB_BLOCK), F32),
            pltpu.VMEM((FAR_GROUP * MB_BLOCK, 2 * MB_QTILE), F32),
            pltpu.VMEM((FAR_GROUP * MB_BLOCK, 2 * MB_QTILE), F32),
            pltpu.VMEM((chains * FAR_GROUP, MB_BLOCK), F32),
            pltpu.VMEM((chains * FAR_GROUP, MB_BLOCK), F32),
            pltpu.VMEM((MB_QTILE + MB_BLOCK, 2 * MB_QTILE), F32),
            pltpu.VMEM((chains * (chains // 2 + 1), MB_BLOCK), F32),
        ],
        compiler_params=_params(("parallel", "parallel", "arbitrary")),
        name="moba",
    )(rel_table, q, k, v)


def _segment_ref(b, n):
    c = b.shape[0]
    if 2 * n >= SUBLANES:
        b3 = b.reshape(c // (2 * n), 2 * n, b.shape[1])
        return jnp.broadcast_to(b3[:, n - 1:n, :], b3.shape).reshape(b.shape)
    b3 = b.reshape(c // SUBLANES, SUBLANES, b.shape[1])
    seg = lax.broadcasted_iota(jnp.int32, b3.shape, 1) // (2 * n)
    out = jnp.broadcast_to(b3[:, n - 1:n, :], b3.shape)
    for s in range(1, SUBLANES // (2 * n)):
        r = s * 2 * n + n - 1
        out = jnp.where(seg == s, jnp.broadcast_to(b3[:, r:r + 1, :], b3.shape), out)
    return out.reshape(b.shape)


def _hgrn_kernel(q_ref, f_ref, v_ref, g_ref, nw_ref, o_ref, state_ref):
    c = HG_CHUNK

    @pl.when(pl.program_id(1) == 0)
    def _reset():
        state_ref[...] = jnp.zeros_like(state_ref)

    half = c // 2
    row = lax.broadcasted_iota(jnp.int32, (c, c), 0)
    col = lax.broadcasted_iota(jnp.int32, (c, c), 1)
    tril = jnp.where(row >= col, 1.0, 0.0).astype(BF16)
    rh = lax.broadcasted_iota(jnp.int32, (half, half), 0)
    ch = lax.broadcasted_iota(jnp.int32, (half, half), 1)
    split = jnp.where(rh > ch, rh ^ ch, 0)
    levels = [1 << i for i in range(int(math.log2(half)))]

    def chunk_head(rows, hh):
        sl = slice(hh * HG_DK, (hh + 1) * HG_DK)
        f = f_ref[0, rows, sl]
        logf = jnp.log(f)
        kk = 1.0 - f
        q = q_ref[0, rows, sl]
        v = v_ref[0, rows, sl]
        vb = v.astype(BF16)
        qb = q.astype(BF16)
        kb = kk.astype(BF16)

        l_hi, l_mid, l_lo = _split3(logf * LOG2E)
        b = _dot(tril, l_hi) + (_dot(tril, l_mid) + _dot(tril, l_lo))

        def level_operands(n):
            if n == 1:
                return qb * f.astype(BF16), kb
            e_l = jnp.exp2(-jnp.abs(b - _segment_ref(b, n))).astype(BF16)
            return qb * e_l, kb * e_l

        d_lo = jnp.zeros((half, half), F32)
        d_hi = jnp.zeros((half, half), F32)
        for n in levels:
            q_l, k_l = level_operands(n)
            d_lo = jnp.where(split >= n, _dot_nt(q_l[:half], k_l[:half]), d_lo)
            d_hi = jnp.where(split >= n, _dot_nt(q_l[half:], k_l[half:]), d_hi)
        q_l, k_l = level_operands(half)
        cross = _dot_nt(q_l[half:], k_l[:half])

        st = state_ref[hh]
        intra = jnp.concatenate(
            [_dot(d_lo.astype(BF16), vb[:half]),
             _dot(jnp.concatenate([cross, d_hi], axis=1).astype(BF16), vb)], axis=0)
        o = (intra + jnp.sum(q * kk, axis=-1, keepdims=True) * v
             + _dot_nt((q * jnp.exp2(b)).astype(BF16), st.astype(BF16)))
        b_last = b[c - 1:c, :]
        k_dec = (kk * jnp.exp2(b_last - b)).astype(BF16)
        state_ref[hh] = st * jnp.exp2(b_last) + lax.dot_general(
            vb, k_dec, (((0,), (0,)), ((), ())), preferred_element_type=F32)

        o_ref[0, rows, sl] = _rms(o, nw_ref[:, :]) * g_ref[0, rows, sl]

    for sub in range(HG_STEP_CHUNKS):
        for hh in range(HG_HEADS):
            chunk_head(slice(sub * c, (sub + 1) * c), hh)


def _hgrn(q, f, v, g, norm_w):
    bsz, seq, _ = q.shape
    step_rows = HG_STEP_CHUNKS * HG_CHUNK
    assert seq % step_rows == 0
    blk = pl.BlockSpec((1, step_rows, HG_WIDTH), lambda b, t: (b, t, 0))
    return pl.pallas_call(
        _hgrn_kernel,
        grid=(bsz, seq // step_rows),
        in_specs=[blk, blk, blk, blk, pl.BlockSpec((1, HG_DV), lambda b, t: (0, 0))],
        out_specs=blk,
        out_shape=jax.ShapeDtypeStruct((bsz, seq, HG_WIDTH), F32),
        scratch_shapes=[pltpu.VMEM((HG_HEADS, HG_DV, HG_DK), F32)],
        compiler_params=_params(("parallel", "arbitrary")),
        name="hgrn",
    )(q, f, v, g, norm_w)


def kernel(x, p, w_ffn1_gu, w_ffn1_down, w_in, w_branch_a, w_branch_b, w_out,
           w_ffn2_gu, w_ffn2_down, w_ple, w_ple_gate, norm_gains, hg_norm_w,
           lb_param, rel_table):
    bsz, seq, _ = x.shape
    t = bsz * seq
    assert t % TOKEN_TILE == 0
    h = x.reshape(t, D_MODEL)
    for i in range(p.shape[0]):
        g = norm_gains[i]
        h = _ffn(h, g, w_ffn1_gu[i].astype(BF16), w_ffn1_down[i].astype(BF16), 0, 1)
        mq, mk, mv, hq, hf, hi, hg, gates = _inproj(h, g, lb_param, w_in[i].astype(BF16), i)
        r3 = lambda a: a.reshape(bsz, seq, a.shape[-1])
        o_a = _moba(r3(mq), r3(mk), r3(mv), rel_table)
        o_b = _hgrn(r3(hq), r3(hf), r3(hi), r3(hg), hg_norm_w[i:i + 1])
        h = _post(h, o_a.reshape(t, MB_WIDTH), o_b.reshape(t, HG_WIDTH), gates,
                  p[i].reshape(t, PLE_DIM), g,
                  w_branch_a[i].astype(BF16), w_branch_b[i].astype(BF16), w_out[i].astype(BF16),
                  w_ffn2_gu[i].astype(BF16), w_ffn2_down[i].astype(BF16),
                  w_ple[i].astype(BF16), w_ple_gate[i].astype(BF16))
    return h.reshape(bsz, seq, D_MODEL)
```

```python
import functools
import math

import numpy as np
import jax
import jax.numpy as jnp
from jax import lax
from jax.experimental import pallas as pl
from jax.experimental.pallas import tpu as pltpu

F32 = jnp.float32
BF16 = jnp.bfloat16

D_MODEL = 1024
PLE_DIM = 256
D_FF = 2816
MB_HEADS = 8
MB_HEAD_DIM = 64
MB_WIDTH = MB_HEADS * MB_HEAD_DIM
MB_BLOCK = 256
MB_TOPK = 3
HG_HEADS = 4
HG_DK = 128
HG_DV = 128
HG_WIDTH = HG_HEADS * HG_DV
REL_BUCKETS = 32
REL_MAX_EXACT = REL_BUCKETS // 2
REL_MAX_DIST = 128
N_NORMS = 7
EPS = 1e-6

LANES = 128
SUBLANES = 8
VMEM_LIMIT = 56 * 1024 * 1024
TOKEN_TILE = 512
FFN_TILE = 1024
FF_CHUNK = 256
HG_CHUNK = 256
HG_STEP_CHUNKS = 4
MB_QTILE = 1024
FAR_GROUP = 2
PAD_BLOCKS = 3 * FAR_GROUP
PAD_END = PAD_BLOCKS * 256
PEELED = 3
NEG = -1e30
LOG2E = 1.4426950408889634


def _rms(x, g):
    return x * lax.rsqrt(jnp.mean(x * x, axis=-1, keepdims=True) + EPS) * g


def _sigmoid(x):
    return 1.0 / (1.0 + jnp.exp(-x))


def _dot(a, b):
    return jnp.dot(a, b, preferred_element_type=F32)


def _dot_nt(a, b):
    return lax.dot_general(a, b, (((1,), (1,)), ((), ())), preferred_element_type=F32)


def _split3(x):
    def top(y):
        bits = pltpu.bitcast(y, jnp.uint32) & jnp.uint32(0xFFFF0000)
        return pltpu.bitcast(bits, F32)
    hi = top(x)
    r1 = x - hi
    mid = top(r1)
    return hi.astype(BF16), mid.astype(BF16), (r1 - mid).astype(BF16)


def _const_spec(shape):
    nd = len(shape)
    return pl.BlockSpec(shape, lambda *_: (0,) * nd, pipeline_mode=pl.Buffered(1))


def _params(sem, flags=None):
    return pltpu.CompilerParams(dimension_semantics=sem, vmem_limit_bytes=VMEM_LIMIT, flags=flags)


def _ffn_step(x, g_ref, pre, post, wgu_ref, wd_ref, act_ref):
    u = _rms(x, g_ref[pre:pre + 1, :]).astype(BF16)
    for c in range(D_FF // FF_CHUNK):
        lo = c * FF_CHUNK
        g = _dot(u, wgu_ref[:, lo:lo + FF_CHUNK])
        v = _dot(u, wgu_ref[:, D_FF + lo:D_FF + lo + FF_CHUNK])
        act_ref[:, lo:lo + FF_CHUNK] = (g * _sigmoid(g) * v).astype(BF16)
    y = _dot(act_ref[...], wd_ref[...])
    return x + 0.5 * _rms(y, g_ref[post:post + 1, :])


def _ffn_kernel(pre, post, x_ref, g_ref, wgu_ref, wd_ref, o_ref, act_ref):
    o_ref[...] = _ffn_step(x_ref[...], g_ref, pre, post, wgu_ref, wd_ref, act_ref)


def _ffn(x, gains, w_gu, w_down, pre, post):
    t = x.shape[0]
    return pl.pallas_call(
        functools.partial(_ffn_kernel, pre, post),
        grid=(t // FFN_TILE,),
        in_specs=[
            pl.BlockSpec((FFN_TILE, D_MODEL), lambda i: (i, 0)),
            _const_spec((N_NORMS, D_MODEL)),
            _const_spec((D_MODEL, 2 * D_FF)),
            _const_spec((D_FF, D_MODEL)),
        ],
        out_specs=pl.BlockSpec((FFN_TILE, D_MODEL), lambda i: (i, 0)),
        out_shape=jax.ShapeDtypeStruct((t, D_MODEL), F32),
        scratch_shapes=[pltpu.VMEM((FFN_TILE, D_FF), BF16)],
        compiler_params=_params(("parallel",)),
        name="ffn",
    )(x, gains, w_gu, w_down)


IN_SPLITS = (MB_WIDTH, MB_WIDTH, MB_WIDTH, HG_WIDTH, HG_WIDTH, HG_WIDTH, HG_WIDTH, 2 * D_MODEL)
IN_COLS = sum(IN_SPLITS)


HG_Q, HG_F, HG_G = 3, 4, 6


def _inproj_kernel(layer, h_ref, g_ref, lbp_ref, w_ref, *out_refs):
    u = _rms(h_ref[...], g_ref[2:3, :]).astype(BF16)
    lp = lbp_ref[...]
    e = jnp.exp(lp - jnp.max(lp, axis=0, keepdims=True))
    sm = e / jnp.sum(e, axis=0, keepdims=True)
    lb = sm[0:1, :]
    for r in range(1, layer + 1):
        lb = lb + sm[r:r + 1, :]
    off = 0
    for idx, ref in enumerate(out_refs):
        n = ref.shape[-1]
        for c in range(0, n, 512):
            z = _dot(u, w_ref[:, off + c:off + c + 512])
            if idx in (HG_Q, HG_G):
                z = z * _sigmoid(z)
            elif idx == HG_F:
                z = lb + (1.0 - lb) * _sigmoid(z)
            ref[:, c:c + 512] = z
        off += n


def _inproj(h, gains, lb_param, w_in, layer):
    t = h.shape[0]
    return pl.pallas_call(
        functools.partial(_inproj_kernel, layer),
        grid=(t // TOKEN_TILE,),
        in_specs=[
            pl.BlockSpec((TOKEN_TILE, D_MODEL), lambda i: (i, 0)),
            _const_spec((N_NORMS, D_MODEL)),
            _const_spec(lb_param.shape),
            _const_spec((D_MODEL, IN_COLS)),
        ],
        out_specs=[pl.BlockSpec((TOKEN_TILE, n), lambda i: (i, 0)) for n in IN_SPLITS],
        out_shape=[jax.ShapeDtypeStruct((t, n), F32) for n in IN_SPLITS],
        compiler_params=_params(("parallel",)),
        name="in_proj",
    )(h, gains, lb_param, w_in)


def _post_kernel(h_ref, oa_ref, ob_ref, gg_ref, p_ref, g_ref, wa_ref, wb_ref, wo_ref,
                 wgu_ref, wd_ref, wp_ref, wg_ref, o_ref, act_ref):
    a = _dot(oa_ref[...].astype(BF16), wa_ref[...])
    b = _dot(ob_ref[...].astype(BF16), wb_ref[...])
    merged = _sigmoid(gg_ref[:, :D_MODEL]) * a + _sigmoid(gg_ref[:, D_MODEL:]) * b
    h = h_ref[...] + _rms(_dot(merged.astype(BF16), wo_ref[...]), g_ref[3:4, :])
    h = _ffn_step(h, g_ref, 4, 5, wgu_ref, wd_ref, act_ref)
    e = _dot(p_ref[...].astype(BF16), wp_ref[...])
    t = _dot(h.astype(BF16), wg_ref[...])
    o_ref[...] = h + _rms(_sigmoid(t) * e, g_ref[6:7, :])


def _post(h, o_a, o_b, gates, p, gains, w_a, w_b, w_out, w_gu, w_down, w_ple, w_ple_gate):
    t = h.shape[0]
    row = lambda n: pl.BlockSpec((TOKEN_TILE, n), lambda i: (i, 0))
    return pl.pallas_call(
        _post_kernel,
        grid=(t // TOKEN_TILE,),
        in_specs=[
            row(D_MODEL), row(MB_WIDTH), row(HG_WIDTH), row(2 * D_MODEL), row(PLE_DIM),
            _const_spec((N_NORMS, D_MODEL)),
            _const_spec((MB_WIDTH, D_MODEL)),
            _const_spec((HG_WIDTH, D_MODEL)),
            _const_spec((D_MODEL, D_MODEL)),
            _const_spec((D_MODEL, 2 * D_FF)),
            _const_spec((D_FF, D_MODEL)),
            _const_spec((PLE_DIM, D_MODEL)),
            _const_spec((D_MODEL, D_MODEL)),
        ],
        out_specs=row(D_MODEL),
        out_shape=jax.ShapeDtypeStruct((t, D_MODEL), F32),
        scratch_shapes=[pltpu.VMEM((TOKEN_TILE, D_FF), BF16)],
        compiler_params=_params(("parallel",)),
        name="post",
    )(h, o_a, o_b, gates, p, gains, w_a, w_b, w_out, w_gu, w_down, w_ple, w_ple_gate)


def _bucket_starts():
    n = np.arange(0, 2 * MB_BLOCK, dtype=np.int32)
    nf = np.maximum(n, 1).astype(np.float32)
    large = REL_MAX_EXACT + (
        np.log(nf / np.float32(REL_MAX_EXACT)) / np.float32(math.log(REL_MAX_DIST / REL_MAX_EXACT))
        * np.float32(REL_BUCKETS - REL_MAX_EXACT)).astype(np.int32)
    large = np.minimum(large, REL_BUCKETS - 1)
    bucket = np.where(n < REL_MAX_EXACT, n, large)
    assert np.all(np.diff(bucket) >= 0) and bucket[-1] == REL_BUCKETS - 1
    return [int(np.argmax(bucket >= b)) for b in range(REL_BUCKETS)]


def _moba_kernel(starts, rel_ref, q_ref, k_ref, v_ref, o_ref,
                 kb_ref, vt_ref, kmean_ref, bown_ref, bprev_ref, sel_ref, qs_ref,
                 m_ref, acc_ref, sb0_ref, sb1_ref, mb0_ref, mb1_ref, sn_ref, mbn_ref):
    pair = pl.program_id(1)
    step = pl.program_id(2)
    seq = k_ref.shape[1]
    nb = seq // MB_BLOCK
    hd = MB_HEAD_DIM
    qscale = (hd ** -0.5) * LOG2E

    @pl.when(step == 0)
    def _prepare():
        kb_ref[0:MB_BLOCK, :] = jnp.zeros((MB_BLOCK, LANES), BF16)
        kb_ref[MB_BLOCK + seq:, :] = jnp.zeros((PAD_END, LANES), BF16)
        sel_ref[:, nb:, :] = jnp.full((2, PAD_BLOCKS, MB_QTILE), NEG, F32)
        for hh in range(2):
            vt_ref[hh, :, 0:MB_BLOCK] = jnp.zeros((hd + 16, MB_BLOCK), BF16)
            vt_ref[hh, :, MB_BLOCK + seq:] = jnp.zeros((hd + 16, PAD_END), BF16)
            vt_ref[hh, hd:hd + 16, MB_BLOCK:MB_BLOCK + seq] = jnp.ones((16, seq), BF16)

        def blk(j, carry):
            r0 = pl.multiple_of(j * MB_BLOCK, MB_BLOCK)
            r1 = pl.multiple_of(r0 + MB_BLOCK, MB_BLOCK)
            kblk = k_ref[0, pl.ds(r0, MB_BLOCK), :]
            kb_ref[pl.ds(r1, MB_BLOCK), :] = kblk.astype(BF16)
            kmean_ref[pl.ds(j, 1), :] = jnp.mean(kblk, axis=0, keepdims=True)
            vt = v_ref[0, pl.ds(r0, MB_BLOCK), :].T.astype(BF16)
            vt_ref[0, 0:hd, pl.ds(r1, MB_BLOCK)] = vt[0:hd]
            vt_ref[1, 0:hd, pl.ds(r1, MB_BLOCK)] = vt[hd:2 * hd]
            return carry

        lax.fori_loop(0, nb, blk, 0, unroll=4)

        kk = lax.broadcasted_iota(jnp.int32, (MB_BLOCK, MB_BLOCK), 0)
        qq = lax.broadcasted_iota(jnp.int32, (MB_BLOCK, MB_BLOCK), 1)
        d_own = qq - kk
        d_prev = d_own + MB_BLOCK
        for hh in range(2):
            head = pair * 2 + hh
            b_own = jnp.full((MB_BLOCK, MB_BLOCK), rel_ref[0, head], F32)
            b_prev = jnp.full((MB_BLOCK, MB_BLOCK), rel_ref[0, head], F32)
            for b in range(1, REL_BUCKETS):
                val = rel_ref[b, head]
                b_own = jnp.where(d_own >= starts[b], val, b_own)
                b_prev = jnp.where(d_prev >= starts[b], val, b_prev)
            bown_ref[hh] = jnp.where(d_own >= 0, b_own * LOG2E, NEG)
            bprev_ref[hh] = b_prev * LOG2E

    nq = MB_QTILE // MB_BLOCK
    qb0 = step * nq
    c_far = [rel_ref[REL_BUCKETS - 1, pair * 2 + hh] * LOG2E for hh in range(2)]

    km = kmean_ref[...]
    lane_head = lax.broadcasted_iota(jnp.int32, (nb, 2 * hd), 1) // hd
    km_hi, km_mid, km_lo = _split3(jnp.concatenate(
        [jnp.where(lane_head == hh, km, 0.0) for hh in range(2)], axis=0))
    qt = q_ref[0].T
    q_hi, q_mid, q_lo = _split3(qt)
    gates = (_dot(km_hi, q_hi) + (_dot(km_hi, q_mid) + _dot(km_mid, q_hi))
             + (_dot(km_hi, q_lo) + _dot(km_mid, q_mid) + _dot(km_lo, q_hi)))
    row_head = lax.broadcasted_iota(jnp.int32, (2 * hd, MB_QTILE), 0) // hd
    blk_row = lax.broadcasted_iota(jnp.int32, (nb, MB_QTILE), 0)
    own_blk = qb0 + lax.broadcasted_iota(jnp.int32, (nb, MB_QTILE), 1) // MB_BLOCK
    for hh in range(2):
        qs = (jnp.where(row_head == hh, qt, 0.0) * qscale).astype(BF16)
        for qi in range(nq):
            c0 = (2 * qi + hh) * MB_BLOCK
            qs_ref[:, c0:c0 + MB_BLOCK] = qs[:, qi * MB_BLOCK:(qi + 1) * MB_BLOCK]
        gate = gates[hh * nb:(hh + 1) * nb]
        gate = jnp.where(blk_row < own_blk, gate, -jnp.inf)
        sel = jnp.full((nb, MB_QTILE), NEG, F32)
        for _ in range(MB_TOPK):
            mx = jnp.max(gate, axis=0, keepdims=True)
            cand = jnp.where((gate == mx) & (mx > -jnp.inf), blk_row, nb)
            pick = blk_row == jnp.min(cand, axis=0, keepdims=True)
            sel = jnp.where(pick, 0.0, sel)
            gate = jnp.where(pick, -jnp.inf, gate)
        sel_ref[hh, 0:nb, :] = sel

    def key_slice(blocks):
        return pl.ds(pl.multiple_of(blocks[0] * MB_BLOCK, MB_BLOCK), blocks[1] * MB_BLOCK)

    def score(qi, blocks, mats, sb, mb):
        s_all = _dot(kb_ref[key_slice(blocks), :],
                     qs_ref[:, 2 * qi * MB_BLOCK:2 * (qi + 1) * MB_BLOCK])
        slots = mb.shape[0] // (2 * nq)
        for hh in range(2):
            c = 2 * qi + hh
            for u in range(blocks[1]):
                piece = s_all[u * MB_BLOCK:(u + 1) * MB_BLOCK, hh * MB_BLOCK:(hh + 1) * MB_BLOCK]
                if mats[hh][u] is not None:
                    piece = piece + mats[hh][u]
                mb[c * slots + u:c * slots + u + 1, :] = jnp.max(piece, axis=0, keepdims=True)
                sb[u * MB_BLOCK:(u + 1) * MB_BLOCK, c * MB_BLOCK:(c + 1) * MB_BLOCK] = piece

    def consume(qi, blocks, rows, sb, mb, first):
        kpos = key_slice(blocks)
        slots = mb.shape[0] // (2 * nq)
        for hh in range(2):
            c = 2 * qi + hh
            m_blk = None
            for u in range(blocks[1]):
                m_u = mb[c * slots + u:c * slots + u + 1, :]
                if rows[hh][u] is not None:
                    m_u = m_u + rows[hh][u]
                m_blk = m_u if m_blk is None else jnp.maximum(m_blk, m_u)
            m_old = None if first else m_ref[c:c + 1, :]
            m_new = m_blk if first else jnp.maximum(m_old, m_blk)
            p = jnp.concatenate(
                [jnp.exp2(sb[u * MB_BLOCK:(u + 1) * MB_BLOCK, c * MB_BLOCK:(c + 1) * MB_BLOCK]
                          + ((-m_new) if rows[hh][u] is None else (rows[hh][u] - m_new)))
                 for u in range(blocks[1])], axis=0)
            pv = _dot(vt_ref[hh, :, kpos], p.astype(BF16))
            acc_ref[c] = pv if first else acc_ref[c] * jnp.exp2(m_old - m_new) + pv
            m_ref[c:c + 1, :] = m_new

    def far_row(hh, qi, j, limit):
        row = sel_ref[hh, pl.ds(jnp.maximum(j, 0), 1), qi * MB_BLOCK:(qi + 1) * MB_BLOCK]
        return jnp.where((j >= 0) & (j < limit), row + c_far[hh], NEG)

    def near_blocks(qi):
        return qb0, qi + 2

    def score_near(qi):
        mats = [[None] * qi + [bprev_ref[hh], bown_ref[hh]] for hh in range(2)]
        score(qi, near_blocks(qi), mats, sn_ref, mbn_ref)

    def consume_near(qi):
        qb = qb0 + qi
        rows = []
        for hh in range(2):
            far = [far_row(hh, qi, qb0 - 1 + u, qb - 1) for u in range(qi)]
            prev = sel_ref[hh, pl.ds(jnp.maximum(qb - 1, 0), 1), qi * MB_BLOCK:(qi + 1) * MB_BLOCK]
            rows.append(far + [jnp.where(qb >= 1, prev, NEG), None])
        consume(qi, near_blocks(qi), rows, sn_ref, mbn_ref, True)

    nfar = qb0 - 1
    chains = [(qi, hh) for qi in range(nq) for hh in range(2)]

    def far_blocks(g, n=FAR_GROUP):
        return g * FAR_GROUP + 1, n

    def score_far(g, qi, sb, mb, n=FAR_GROUP):
        score(qi, far_blocks(g, n), [[None] * n] * 2, sb, mb)

    def consume_far(g, qi, sb, mb, n=FAR_GROUP):
        rows = [[far_row(hh, qi, g * FAR_GROUP + u, nfar) for u in range(n)]
                for hh in range(2)]
        consume(qi, far_blocks(g, n), rows, sb, mb, False)

    score_near(0)
    for qi in range(1, nq):
        score_near(qi)
        consume_near(qi - 1)
    score_far(0, 0, sb0_ref, mb0_ref)
    consume_near(nq - 1)
    for qi in range(1, nq):
        score_far(0, qi, sb0_ref, mb0_ref)

    assert nq % (2 * FAR_GROUP) == 0 and FAR_GROUP > 1

    def two_groups(t, last):
        n_last = FAR_GROUP - 1 if last else FAR_GROUP
        for qi in range(nq):
            consume_far(2 * t, qi, sb0_ref, mb0_ref)
            score_far(2 * t + 1, qi, sb1_ref, mb1_ref, n_last)
        for qi in range(nq):
            consume_far(2 * t + 1, qi, sb1_ref, mb1_ref, n_last)
            if not last:
                score_far(2 * t + 2, qi, sb0_ref, mb0_ref)

    trips = (jnp.maximum(nfar, 0) + 2 * FAR_GROUP - 1) // (2 * FAR_GROUP)
    lax.fori_loop(0, trips - PEELED, lambda t, carry: (two_groups(t, False), carry)[1], 0)
    for n in range(1, PEELED + 1):
        @pl.when((trips == n) if n < PEELED else (trips >= n))
        def _tail(n=n):
            for i in range(n, 0, -1):
                two_groups(trips - i, i == 1)

    for qi in range(nq):
        outs = [acc_ref[2 * qi + hh, 0:hd, :] / acc_ref[2 * qi + hh, hd:hd + 1, :]
                for hh in range(2)]
        o_ref[0, qi * MB_BLOCK:(qi + 1) * MB_BLOCK, :] = jnp.concatenate(outs, axis=0).T


def _moba(q, k, v, rel_table):
    bsz, seq, _ = q.shape
    assert seq % MB_QTILE == 0
    nb = seq // MB_BLOCK
    npair = MB_HEADS // 2
    padded = MB_BLOCK + seq + PAD_END
    chains = 2 * MB_QTILE // MB_BLOCK
    return pl.pallas_call(
        functools.partial(_moba_kernel, _bucket_starts()),
        grid=(bsz, npair, seq // MB_QTILE),
        in_specs=[
            pl.BlockSpec(memory_space=pltpu.SMEM),
            pl.BlockSpec((1, MB_QTILE, LANES), lambda b, p, t: (b, t, p)),
            pl.BlockSpec((1, seq, LANES), lambda b, p, t: (b, 0, p)),
            pl.BlockSpec((1, seq, LANES), lambda b, p, t: (b, 0, p)),
        ],
        out_specs=pl.BlockSpec((1, MB_QTILE, LANES), lambda b, p, t: (b, t, p)),
        out_shape=jax.ShapeDtypeStruct((bsz, seq, MB_WIDTH), F32),
        scratch_shapes=[
            pltpu.VMEM((padded, LANES), BF16),
            pltpu.VMEM((2, MB_HEAD_DIM + 16, padded), BF16),
            pltpu.VMEM((nb, LANES), F32),
            pltpu.VMEM((2, MB_BLOCK, MB_BLOCK), F32),
            pltpu.VMEM((2, MB_BLOCK, MB_BLOCK), F32),
            pltpu.VMEM((2, nb + PAD_BLOCKS, MB_QTILE), F32),
            pltpu.VMEM((LANES, 2 * MB_QTILE), BF16),
            pltpu.VMEM((2 * MB_QTILE // MB_BLOCK, MB_BLOCK), F32),
            pltpu.VMEM((2 * MB_QTILE // MB_BLOCK, MB_HEAD_DIM + 16, MB_BLOCK), F32),
            pltpu.VMEM((FAR_GROUP * MB_BLOCK, 2 * MB_QTILE), F32),
            pltpu.VMEM((FAR_GROUP * MB_BLOCK, 2 * MB_QTILE), F32),
            pltpu.VMEM((chains * FAR_GROUP, MB_BLOCK), F32),
            pltpu.VMEM((chains * FAR_GROUP, MB_BLOCK), F32),
            pltpu.VMEM((MB_QTILE + MB_BLOCK, 2 * MB_QTILE), F32),
            pltpu.VMEM((chains * (chains // 2 + 1), MB_BLOCK), F32),
        ],
        compiler_params=_params(("parallel", "parallel", "arbitrary")),
        name="moba",
    )(rel_table, q, k, v)


def _segment_ref(b, n):
    c = b.shape[0]
    if 2 * n >= SUBLANES:
        b3 = b.reshape(c // (2 * n), 2 * n, b.shape[1])
        return jnp.broadcast_to(b3[:, n - 1:n, :], b3.shape).reshape(b.shape)
    b3 = b.reshape(c // SUBLANES, SUBLANES, b.shape[1])
    seg = lax.broadcasted_iota(jnp.int32, b3.shape, 1) // (2 * n)
    out = jnp.broadcast_to(b3[:, n - 1:n, :], b3.shape)
    for s in range(1, SUBLANES // (2 * n)):
        r = s * 2 * n + n - 1
        out = jnp.where(seg == s, jnp.broadcast_to(b3[:, r:r + 1, :], b3.shape), out)
    return out.reshape(b.shape)


def _hgrn_kernel(q_ref, f_ref, v_ref, g_ref, nw_ref, o_ref, state_ref):
    c = HG_CHUNK

    @pl.when(pl.program_id(1) == 0)
    def _reset():
        state_ref[...] = jnp.zeros_like(state_ref)

    half = c // 2
    row = lax.broadcasted_iota(jnp.int32, (c, c), 0)
    col = lax.broadcasted_iota(jnp.int32, (c, c), 1)
    tril = jnp.where(row >= col, 1.0, 0.0).astype(BF16)
    rh = lax.broadcasted_iota(jnp.int32, (half, half), 0)
    ch = lax.broadcasted_iota(jnp.int32, (half, half), 1)
    split = jnp.where(rh > ch, rh ^ ch, 0)
    levels = [1 << i for i in range(int(math.log2(half)))]

    def chunk_head(rows, hh):
        sl = slice(hh * HG_DK, (hh + 1) * HG_DK)
        f = f_ref[0, rows, sl]
        logf = jnp.log(f)
        kk = 1.0 - f
        q = q_ref[0, rows, sl]
        v = v_ref[0, rows, sl]
        vb = v.astype(BF16)
        qb = q.astype(BF16)
        kb = kk.astype(BF16)

        l_hi, l_mid, l_lo = _split3(logf * LOG2E)
        b = _dot(tril, l_hi) + (_dot(tril, l_mid) + _dot(tril, l_lo))

        def level_operands(n):
            if n == 1:
                return qb * f.astype(BF16), kb
            e_l = jnp.exp2(-jnp.abs(b - _segment_ref(b, n))).astype(BF16)
            return qb * e_l, kb * e_l

        d_lo = jnp.zeros((half, half), F32)
        d_hi = jnp.zeros((half, half), F32)
        for n in levels:
            q_l, k_l = level_operands(n)
            d_lo = jnp.where(split >= n, _dot_nt(q_l[:half], k_l[:half]), d_lo)
            d_hi = jnp.where(split >= n, _dot_nt(q_l[half:], k_l[half:]), d_hi)
        q_l, k_l = level_operands(half)
        cross = _dot_nt(q_l[half:], k_l[:half])

        st = state_ref[hh]
        intra = jnp.concatenate(
            [_dot(d_lo.astype(BF16), vb[:half]),
             _dot(jnp.concatenate([cross, d_hi], axis=1).astype(BF16), vb)], axis=0)
        o = (intra + jnp.sum(q * kk, axis=-1, keepdims=True) * v
             + _dot_nt((q * jnp.exp2(b)).astype(BF16), st.astype(BF16)))
        b_last = b[c - 1:c, :]
        k_dec = (kk * jnp.exp2(b_last - b)).astype(BF16)
        state_ref[hh] = st * jnp.exp2(b_last) + lax.dot_general(
            vb, k_dec, (((0,), (0,)), ((), ())), preferred_element_type=F32)

        o_ref[0, rows, sl] = _rms(o, nw_ref[:, :]) * g_ref[0, rows, sl]

    for sub in range(HG_STEP_CHUNKS):
        for hh in range(HG_HEADS):
            chunk_head(slice(sub * c, (sub + 1) * c), hh)


def _hgrn(q, f, v, g, norm_w):
    bsz, seq, _ = q.shape
    step_rows = HG_STEP_CHUNKS * HG_CHUNK
    assert seq % step_rows == 0
    blk = pl.BlockSpec((1, step_rows, HG_WIDTH), lambda b, t: (b, t, 0))
    return pl.pallas_call(
        _hgrn_kernel,
        grid=(bsz, seq // step_rows),
        in_specs=[blk, blk, blk, blk, pl.BlockSpec((1, HG_DV), lambda b, t: (0, 0))],
        out_specs=blk,
        out_shape=jax.ShapeDtypeStruct((bsz, seq, HG_WIDTH), F32),
        scratch_shapes=[pltpu.VMEM((HG_HEADS, HG_DV, HG_DK), F32)],
        compiler_params=_params(("parallel", "arbitrary")),
        name="hgrn",
    )(q, f, v, g, norm_w)


def kernel(x, p, w_ffn1_gu, w_ffn1_down, w_in, w_branch_a, w_branch_b, w_out,
           w_ffn2_gu, w_ffn2_down, w_ple, w_ple_gate, norm_gains, hg_norm_w,
           lb_param, rel_table):
    bsz, seq, _ = x.shape
    t = bsz * seq
    assert t % TOKEN_TILE == 0
    h = x.reshape(t, D_MODEL)
    for i in range(p.shape[0]):
        g = norm_gains[i]
        h = _ffn(h, g, w_ffn1_gu[i].astype(BF16), w_ffn1_down[i].astype(BF16), 0, 1)
        mq, mk, mv, hq, hf, hi, hg, gates = _inproj(h, g, lb_param, w_in[i].astype(BF16), i)
        r3 = lambda a: a.reshape(bsz, seq, a.shape[-1])
        o_a = _moba(r3(mq), r3(mk), r3(mv), rel_table)
        o_b = _hgrn(r3(hq), r3(hf), r3(hi), r3(hg), hg_norm_w[i:i + 1])
        h = _post(h, o_a.reshape(t, MB_WIDTH), o_b.reshape(t, HG_WIDTH), gates,
                  p[i].reshape(t, PLE_DIM), g,
                  w_branch_a[i].astype(BF16), w_branch_b[i].astype(BF16), w_out[i].astype(BF16),
                  w_ffn2_gu[i].astype(BF16), w_ffn2_down[i].astype(BF16),
                  w_ple[i].astype(BF16), w_ple_gate[i].astype(BF16))
    return h.reshape(bsz, seq, D_MODEL)
```

```python
import functools
import math

import numpy as np
import jax
import jax.numpy as jnp
from jax import lax
from jax.experimental import pallas as pl
from jax.experimental.pallas import tpu as pltpu

F32 = jnp.float32
BF16 = jnp.bfloat16

D_MODEL = 1024
PLE_DIM = 256
D_FF = 2816
MB_HEADS = 8
MB_HEAD_DIM = 64
MB_WIDTH = MB_HEADS * MB_HEAD_DIM
MB_BLOCK = 256
MB_TOPK = 3
HG_HEADS = 4
HG_DK = 128
HG_DV = 128
HG_WIDTH = HG_HEADS * HG_DV
REL_BUCKETS = 32
REL_MAX_EXACT = REL_BUCKETS // 2
REL_MAX_DIST = 128
N_NORMS = 7
EPS = 1e-6

LANES = 128
SUBLANES = 8
VMEM_LIMIT = 56 * 1024 * 1024
TOKEN_TILE = 512
FFN_TILE = 1024
FF_CHUNK = 256
HG_CHUNK = 256
HG_STEP_CHUNKS = 4
MB_QTILE = 1024
FAR_GROUP = 2
PAD_BLOCKS = 3 * FAR_GROUP
PAD_END = PAD_BLOCKS * 256
PEELED = 3
NEG = -1e30
LOG2E = 1.4426950408889634


def _rms(x, g):
    return x * lax.rsqrt(jnp.mean(x * x, axis=-1, keepdims=True) + EPS) * g


def _sigmoid(x):
    return 1.0 / (1.0 + jnp.exp(-x))


def _dot(a, b):
    return jnp.dot(a, b, preferred_element_type=F32)


def _dot_nt(a, b):
    return lax.dot_general(a, b, (((1,), (1,)), ((), ())), preferred_element_type=F32)


def _split3(x):
    def top(y):
        bits = pltpu.bitcast(y, jnp.uint32) & jnp.uint32(0xFFFF0000)
        return pltpu.bitcast(bits, F32)
    hi = top(x)
    r1 = x - hi
    mid = top(r1)
    return hi.astype(BF16), mid.astype(BF16), (r1 - mid).astype(BF16)


def _const_spec(shape):
    nd = len(shape)
    return pl.BlockSpec(shape, lambda *_: (0,) * nd, pipeline_mode=pl.Buffered(1))


def _params(sem, flags=None):
    return pltpu.CompilerParams(dimension_semantics=sem, vmem_limit_bytes=VMEM_LIMIT, flags=flags)


def _ffn_step(x, g_ref, pre, post, wgu_ref, wd_ref, act_ref):
    u = _rms(x, g_ref[pre:pre + 1, :]).astype(BF16)
    for c in range(D_FF // FF_CHUNK):
        lo = c * FF_CHUNK
        g = _dot(u, wgu_ref[:, lo:lo + FF_CHUNK])
        v = _dot(u, wgu_ref[:, D_FF + lo:D_FF + lo + FF_CHUNK])
        act_ref[:, lo:lo + FF_CHUNK] = (g * _sigmoid(g) * v).astype(BF16)
    y = _dot(act_ref[...], wd_ref[...])
    return x + 0.5 * _rms(y, g_ref[post:post + 1, :])


def _ffn_kernel(pre, post, x_ref, g_ref, wgu_ref, wd_ref, o_ref, act_ref):
    o_ref[...] = _ffn_step(x_ref[...], g_ref, pre, post, wgu_ref, wd_ref, act_ref)


def _ffn(x, gains, w_gu, w_down, pre, post):
    t = x.shape[0]
    return pl.pallas_call(
        functools.partial(_ffn_kernel, pre, post),
        grid=(t // FFN_TILE,),
        in_specs=[
            pl.BlockSpec((FFN_TILE, D_MODEL), lambda i: (i, 0)),
            _const_spec((N_NORMS, D_MODEL)),
            _const_spec((D_MODEL, 2 * D_FF)),
            _const_spec((D_FF, D_MODEL)),
        ],
        out_specs=pl.BlockSpec((FFN_TILE, D_MODEL), lambda i: (i, 0)),
        out_shape=jax.ShapeDtypeStruct((t, D_MODEL), F32),
        scratch_shapes=[pltpu.VMEM((FFN_TILE, D_FF), BF16)],
        compiler_params=_params(("parallel",)),
        name="ffn",
    )(x, gains, w_gu, w_down)


IN_SPLITS = (MB_WIDTH, MB_WIDTH, MB_WIDTH, HG_WIDTH, HG_WIDTH, HG_WIDTH, HG_WIDTH, 2 * D_MODEL)
IN_COLS = sum(IN_SPLITS)


HG_Q, HG_F, HG_G = 3, 4, 6


def _inproj_kernel(layer, h_ref, g_ref, lbp_ref, w_ref, *out_refs):
    u = _rms(h_ref[...], g_ref[2:3, :]).astype(BF16)
    lp = lbp_ref[...]
    e = jnp.exp(lp - jnp.max(lp, axis=0, keepdims=True))
    sm = e / jnp.sum(e, axis=0, keepdims=True)
    lb = sm[0:1, :]
    for r in range(1, layer + 1):
        lb = lb + sm[r:r + 1, :]
    off = 0
    for idx, ref in enumerate(out_refs):
        n = ref.shape[-1]
        for c in range(0, n, 512):
            z = _dot(u, w_ref[:, off + c:off + c + 512])
            if idx in (HG_Q, HG_G):
                z = z * _sigmoid(z)
            elif idx == HG_F:
                z = lb + (1.0 - lb) * _sigmoid(z)
            ref[:, c:c + 512] = z
        off += n


def _inproj(h, gains, lb_param, w_in, layer):
    t = h.shape[0]
    return pl.pallas_call(
        functools.partial(_inproj_kernel, layer),
        grid=(t // TOKEN_TILE,),
        in_specs=[
            pl.BlockSpec((TOKEN_TILE, D_MODEL), lambda i: (i, 0)),
            _const_spec((N_NORMS, D_MODEL)),
            _const_spec(lb_param.shape),
            _const_spec((D_MODEL, IN_COLS)),
        ],
        out_specs=[pl.BlockSpec((TOKEN_TILE, n), lambda i: (i, 0)) for n in IN_SPLITS],
        out_shape=[jax.ShapeDtypeStruct((t, n), F32) for n in IN_SPLITS],
        compiler_params=_params(("parallel",)),
        name="in_proj",
    )(h, gains, lb_param, w_in)


def _post_kernel(h_ref, oa_ref, ob_ref, gg_ref, p_ref, g_ref, wa_ref, wb_ref, wo_ref,
                 wgu_ref, wd_ref, wp_ref, wg_ref, o_ref, act_ref):
    a = _dot(oa_ref[...].astype(BF16), wa_ref[...])
    b = _dot(ob_ref[...].astype(BF16), wb_ref[...])
    merged = _sigmoid(gg_ref[:, :D_MODEL]) * a + _sigmoid(gg_ref[:, D_MODEL:]) * b
    h = h_ref[...] + _rms(_dot(merged.astype(BF16), wo_ref[...]), g_ref[3:4, :])
    h = _ffn_step(h, g_ref, 4, 5, wgu_ref, wd_ref, act_ref)
    e = _dot(p_ref[...].astype(BF16), wp_ref[...])
    t = _dot(h.astype(BF16), wg_ref[...])
    o_ref[...] = h + _rms(_sigmoid(t) * e, g_ref[6:7, :])


def _post(h, o_a, o_b, gates, p, gains, w_a, w_b, w_out, w_gu, w_down, w_ple, w_ple_gate):
    t = h.shape[0]
    row = lambda n: pl.BlockSpec((TOKEN_TILE, n), lambda i: (i, 0))
    return pl.pallas_call(
        _post_kernel,
        grid=(t // TOKEN_TILE,),
        in_specs=[
            row(D_MODEL), row(MB_WIDTH), row(HG_WIDTH), row(2 * D_MODEL), row(PLE_DIM),
            _const_spec((N_NORMS, D_MODEL)),
            _const_spec((MB_WIDTH, D_MODEL)),
            _const_spec((HG_WIDTH, D_MODEL)),
            _const_spec((D_MODEL, D_MODEL)),
            _const_spec((D_MODEL, 2 * D_FF)),
            _const_spec((D_FF, D_MODEL)),
            _const_spec((PLE_DIM, D_MODEL)),
            _const_spec((D_MODEL, D_MODEL)),
        ],
        out_specs=row(D_MODEL),
        out_shape=jax.ShapeDtypeStruct((t, D_MODEL), F32),
        scratch_shapes=[pltpu.VMEM((TOKEN_TILE, D_FF), BF16)],
        compiler_params=_params(("parallel",)),
        name="post",
    )(h, o_a, o_b, gates, p, gains, w_a, w_b, w_out, w_gu, w_down, w_ple, w_ple_gate)


def _bucket_starts():
    n = np.arange(0, 2 * MB_BLOCK, dtype=np.int32)
    nf = np.maximum(n, 1).astype(np.float32)
    large = REL_MAX_EXACT + (
        np.log(nf / np.float32(REL_MAX_EXACT)) / np.float32(math.log(REL_MAX_DIST / REL_MAX_EXACT))
        * np.float32(REL_BUCKETS - REL_MAX_EXACT)).astype(np.int32)
    large = np.minimum(large, REL_BUCKETS - 1)
    bucket = np.where(n < REL_MAX_EXACT, n, large)
    assert np.all(np.diff(bucket) >= 0) and bucket[-1] == REL_BUCKETS - 1
    return [int(np.argmax(bucket >= b)) for b in range(REL_BUCKETS)]


def _moba_kernel(starts, rel_ref, q_ref, k_ref, v_ref, o_ref,
                 kb_ref, vt_ref, kmean_ref, bown_ref, bprev_ref, sel_ref, qs_ref,
                 m_ref, acc_ref, sb0_ref, sb1_ref, mb0_ref, mb1_ref, sn_ref, mbn_ref):
    pair = pl.program_id(1)
    step = pl.program_id(2)
    seq = k_ref.shape[1]
    nb = seq // MB_BLOCK
    hd = MB_HEAD_DIM
    qscale = (hd ** -0.5) * LOG2E

    @pl.when(step == 0)
    def _prepare():
        kb_ref[0:MB_BLOCK, :] = jnp.zeros((MB_BLOCK, LANES), BF16)
        kb_ref[MB_BLOCK + seq:, :] = jnp.zeros((PAD_END, LANES), BF16)
        sel_ref[:, nb:, :] = jnp.full((2, PAD_BLOCKS, MB_QTILE), NEG, F32)
        for hh in range(2):
            vt_ref[hh, :, 0:MB_BLOCK] = jnp.zeros((hd + 16, MB_BLOCK), BF16)
            vt_ref[hh, :, MB_BLOCK + seq:] = jnp.zeros((hd + 16, PAD_END), BF16)
            vt_ref[hh, hd:hd + 16, MB_BLOCK:MB_BLOCK + seq] = jnp.ones((16, seq), BF16)

        def blk(j, carry):
            r0 = pl.multiple_of(j * MB_BLOCK, MB_BLOCK)
            r1 = pl.multiple_of(r0 + MB_BLOCK, MB_BLOCK)
            kblk = k_ref[0, pl.ds(r0, MB_BLOCK), :]
            kb_ref[pl.ds(r1, MB_BLOCK), :] = kblk.astype(BF16)
            kmean_ref[pl.ds(j, 1), :] = jnp.mean(kblk, axis=0, keepdims=True)
            vt = v_ref[0, pl.ds(r0, MB_BLOCK), :].T.astype(BF16)
            vt_ref[0, 0:hd, pl.ds(r1, MB_BLOCK)] = vt[0:hd]
            vt_ref[1, 0:hd, pl.ds(r1, MB_BLOCK)] = vt[hd:2 * hd]
            return carry

        lax.fori_loop(0, nb, blk, 0, unroll=4)

        kk = lax.broadcasted_iota(jnp.int32, (MB_BLOCK, MB_BLOCK), 0)
        qq = lax.broadcasted_iota(jnp.int32, (MB_BLOCK, MB_BLOCK), 1)
        d_own = qq - kk
        d_prev = d_own + MB_BLOCK
        for hh in range(2):
            head = pair * 2 + hh
            b_own = jnp.full((MB_BLOCK, MB_BLOCK), rel_ref[0, head], F32)
            b_prev = jnp.full((MB_BLOCK, MB_BLOCK), rel_ref[0, head], F32)
            for b in range(1, REL_BUCKETS):
                val = rel_ref[b, head]
                b_own = jnp.where(d_own >= starts[b], val, b_own)
                b_prev = jnp.where(d_prev >= starts[b], val, b_prev)
            bown_ref[hh] = jnp.where(d_own >= 0, b_own * LOG2E, NEG)
            bprev_ref[hh] = b_prev * LOG2E

    nq = MB_QTILE // MB_BLOCK
    qb0 = step * nq
    c_far = [rel_ref[REL_BUCKETS - 1, pair * 2 + hh] * LOG2E for hh in range(2)]

    qt = q_ref[0].T
    row_head = lax.broadcasted_iota(jnp.int32, (2 * hd, MB_QTILE), 0) // hd
    for hh in range(2):
        qs = (jnp.where(row_head == hh, qt, 0.0) * qscale).astype(BF16)
        for qi in range(nq):
            c0 = (2 * qi + hh) * MB_BLOCK
            qs_ref[:, c0:c0 + MB_BLOCK] = qs[:, qi * MB_BLOCK:(qi + 1) * MB_BLOCK]

    def select_blocks():
        km = kmean_ref[...]
        lane_head = lax.broadcasted_iota(jnp.int32, (nb, 2 * hd), 1) // hd
        km_hi, km_mid, km_lo = _split3(jnp.concatenate(
            [jnp.where(lane_head == hh, km, 0.0) for hh in range(2)], axis=0))
        q_hi, q_mid, q_lo = _split3(qt)
        gates = (_dot(km_hi, q_hi) + (_dot(km_hi, q_mid) + _dot(km_mid, q_hi))
                 + (_dot(km_hi, q_lo) + _dot(km_mid, q_mid) + _dot(km_lo, q_hi)))
        blk_row = lax.broadcasted_iota(jnp.int32, (nb, MB_QTILE), 0)
        own_blk = qb0 + lax.broadcasted_iota(jnp.int32, (nb, MB_QTILE), 1) // MB_BLOCK
        for hh in range(2):
            gate = gates[hh * nb:(hh + 1) * nb]
            gate = jnp.where(blk_row < own_blk, gate, -jnp.inf)
            sel = jnp.full((nb, MB_QTILE), NEG, F32)
            for _ in range(MB_TOPK):
                mx = jnp.max(gate, axis=0, keepdims=True)
                cand = jnp.where((gate == mx) & (mx > -jnp.inf), blk_row, nb)
                pick = blk_row == jnp.min(cand, axis=0, keepdims=True)
                sel = jnp.where(pick, 0.0, sel)
                gate = jnp.where(pick, -jnp.inf, gate)
            sel_ref[hh, 0:nb, :] = sel

    def key_slice(blocks):
        return pl.ds(pl.multiple_of(blocks[0] * MB_BLOCK, MB_BLOCK), blocks[1] * MB_BLOCK)

    def score(qi, blocks, mats, sb, mb):
        s_all = _dot(kb_ref[key_slice(blocks), :],
                     qs_ref[:, 2 * qi * MB_BLOCK:2 * (qi + 1) * MB_BLOCK])
        slots = mb.shape[0] // (2 * nq)
        for hh in range(2):
            c = 2 * qi + hh
            for u in range(blocks[1]):
                piece = s_all[u * MB_BLOCK:(u + 1) * MB_BLOCK, hh * MB_BLOCK:(hh + 1) * MB_BLOCK]
                if mats[hh][u] is not None:
                    piece = piece + mats[hh][u]
                mb[c * slots + u:c * slots + u + 1, :] = jnp.max(piece, axis=0, keepdims=True)
                sb[u * MB_BLOCK:(u + 1) * MB_BLOCK, c * MB_BLOCK:(c + 1) * MB_BLOCK] = piece

    def consume(qi, blocks, rows, sb, mb, first):
        kpos = key_slice(blocks)
        slots = mb.shape[0] // (2 * nq)
        for hh in range(2):
            c = 2 * qi + hh
            m_blk = None
            for u in range(blocks[1]):
                m_u = mb[c * slots + u:c * slots + u + 1, :]
                if rows[hh][u] is not None:
                    m_u = m_u + rows[hh][u]
                m_blk = m_u if m_blk is None else jnp.maximum(m_blk, m_u)
            m_old = None if first else m_ref[c:c + 1, :]
            m_new = m_blk if first else jnp.maximum(m_old, m_blk)
            p = jnp.concatenate(
                [jnp.exp2(sb[u * MB_BLOCK:(u + 1) * MB_BLOCK, c * MB_BLOCK:(c + 1) * MB_BLOCK]
                          + ((-m_new) if rows[hh][u] is None else (rows[hh][u] - m_new)))
                 for u in range(blocks[1])], axis=0)
            pv = _dot(vt_ref[hh, :, kpos], p.astype(BF16))
            acc_ref[c] = pv if first else acc_ref[c] * jnp.exp2(m_old - m_new) + pv
            m_ref[c:c + 1, :] = m_new

    def far_row(hh, qi, j, limit):
        row = sel_ref[hh, pl.ds(jnp.maximum(j, 0), 1), qi * MB_BLOCK:(qi + 1) * MB_BLOCK]
        return jnp.where((j >= 0) & (j < limit), row + c_far[hh], NEG)

    def near_blocks(qi):
        return qb0, qi + 2

    def score_near(qi):
        mats = [[None] * qi + [bprev_ref[hh], bown_ref[hh]] for hh in range(2)]
        score(qi, near_blocks(qi), mats, sn_ref, mbn_ref)

    def consume_near(qi):
        qb = qb0 + qi
        rows = []
        for hh in range(2):
            far = [far_row(hh, qi, qb0 - 1 + u, qb - 1) for u in range(qi)]
            prev = sel_ref[hh, pl.ds(jnp.maximum(qb - 1, 0), 1), qi * MB_BLOCK:(qi + 1) * MB_BLOCK]
            rows.append(far + [jnp.where(qb >= 1, prev, NEG), None])
        consume(qi, near_blocks(qi), rows, sn_ref, mbn_ref, True)

    nfar = qb0 - 1
    chains = [(qi, hh) for qi in range(nq) for hh in range(2)]

    def far_blocks(g, n=FAR_GROUP):
        return g * FAR_GROUP + 1, n

    def score_far(g, qi, sb, mb, n=FAR_GROUP):
        score(qi, far_blocks(g, n), [[None] * n] * 2, sb, mb)

    def consume_far(g, qi, sb, mb, n=FAR_GROUP):
        rows = [[far_row(hh, qi, g * FAR_GROUP + u, nfar) for u in range(n)]
                for hh in range(2)]
        consume(qi, far_blocks(g, n), rows, sb, mb, False)

    score_near(0)
    score_near(1)
    select_blocks()
    consume_near(0)
    for qi in range(2, nq):
        score_near(qi)
        consume_near(qi - 1)
    score_far(0, 0, sb0_ref, mb0_ref)
    consume_near(nq - 1)
    for qi in range(1, nq):
        score_far(0, qi, sb0_ref, mb0_ref)

    assert nq % (2 * FAR_GROUP) == 0 and FAR_GROUP > 1

    def two_groups(t, last):
        n_last = FAR_GROUP - 1 if last else FAR_GROUP
        for qi in range(nq):
            consume_far(2 * t, qi, sb0_ref, mb0_ref)
            score_far(2 * t + 1, qi, sb1_ref, mb1_ref, n_last)
        for qi in range(nq):
            consume_far(2 * t + 1, qi, sb1_ref, mb1_ref, n_last)
            if not last:
                score_far(2 * t + 2, qi, sb0_ref, mb0_ref)

    trips = (jnp.maximum(nfar, 0) + 2 * FAR_GROUP - 1) // (2 * FAR_GROUP)
    lax.fori_loop(0, trips - PEELED, lambda t, carry: (two_groups(t, False), carry)[1], 0)
    for n in range(1, PEELED + 1):
        @pl.when((trips == n) if n < PEELED else (trips >= n))
        def _tail(n=n):
            for i in range(n, 0, -1):
                two_groups(trips - i, i == 1)

    for qi in range(nq):
        outs = [acc_ref[2 * qi + hh, 0:hd, :] / acc_ref[2 * qi + hh, hd:hd + 1, :]
                for hh in range(2)]
        o_ref[0, qi * MB_BLOCK:(qi + 1) * MB_BLOCK, :] = jnp.concatenate(outs, axis=0).T


def _moba(q, k, v, rel_table):
    bsz, seq, _ = q.shape
    assert seq % MB_QTILE == 0
    nb = seq // MB_BLOCK
    npair = MB_HEADS // 2
    padded = MB_BLOCK + seq + PAD_END
    chains = 2 * MB_QTILE // MB_BLOCK
    return pl.pallas_call(
        functools.partial(_moba_kernel, _bucket_starts()),
        grid=(bsz, npair, seq // MB_QTILE),
        in_specs=[
            pl.BlockSpec(memory_space=pltpu.SMEM),
            pl.BlockSpec((1, MB_QTILE, LANES), lambda b, p, t: (b, t, p)),
            pl.BlockSpec((1, seq, LANES), lambda b, p, t: (b, 0, p)),
            pl.BlockSpec((1, seq, LANES), lambda b, p, t: (b, 0, p)),
        ],
        out_specs=pl.BlockSpec((1, MB_QTILE, LANES), lambda b, p, t: (b, t, p)),
        out_shape=jax.ShapeDtypeStruct((bsz, seq, MB_WIDTH), F32),
        scratch_shapes=[
            pltpu.VMEM((padded, LANES), BF16),
            pltpu.VMEM((2, MB_HEAD_DIM + 16, padded), BF16),
            pltpu.VMEM((nb, LANES), F32),
            pltpu.VMEM((2, MB_BLOCK, MB_BLOCK), F32),
            pltpu.VMEM((2, MB_BLOCK, MB_BLOCK), F32),
            pltpu.VMEM((2, nb + PAD_BLOCKS, MB_QTILE), F32),
            pltpu.VMEM((LANES, 2 * MB_QTILE), BF16),
            pltpu.VMEM((2 * MB_QTILE // MB_BLOCK, MB_BLOCK), F32),
            pltpu.VMEM((2 * MB_QTILE // MB_BLOCK, MB_HEAD_DIM + 16, MB_BLOCK), F32),
            pltpu.VMEM((FAR_GROUP * MB_BLOCK, 2 * MB_QTILE), F32),
            pltpu.VMEM((FAR_GROUP * MB_BLOCK, 2 * MB_QTILE), F32),
            pltpu.VMEM((chains * FAR_GROUP, MB_BLOCK), F32),
            pltpu.VMEM((chains * FAR_GROUP, MB_BLOCK), F32),
            pltpu.VMEM((MB_QTILE + MB_BLOCK, 2 * MB_QTILE), F32),
            pltpu.VMEM((chains * (chains // 2 + 1), MB_BLOCK), F32),
        ],
        compiler_params=_params(("parallel", "parallel", "arbitrary")),
        name="moba",
    )(rel_table, q, k, v)


def _segment_ref(b, n):
    c = b.shape[0]
    if 2 * n >= SUBLANES:
        b3 = b.reshape(c // (2 * n), 2 * n, b.shape[1])
        return jnp.broadcast_to(b3[:, n - 1:n, :], b3.shape).reshape(b.shape)
    b3 = b.reshape(c // SUBLANES, SUBLANES, b.shape[1])
    seg = lax.broadcasted_iota(jnp.int32, b3.shape, 1) // (2 * n)
    out = jnp.broadcast_to(b3[:, n - 1:n, :], b3.shape)
    for s in range(1, SUBLANES // (2 * n)):
        r = s * 2 * n + n - 1
        out = jnp.where(seg == s, jnp.broadcast_to(b3[:, r:r + 1, :], b3.shape), out)
    return out.reshape(b.shape)


def _hgrn_kernel(q_ref, f_ref, v_ref, g_ref, nw_ref, o_ref, state_ref):
    c = HG_CHUNK

    @pl.when(pl.program_id(1) == 0)
    def _reset():
        state_ref[...] = jnp.zeros_like(state_ref)

    half = c // 2
    row = lax.broadcasted_iota(jnp.int32, (c, c), 0)
    col = lax.broadcasted_iota(jnp.int32, (c, c), 1)
    tril = jnp.where(row >= col, 1.0, 0.0).astype(BF16)
    rh = lax.broadcasted_iota(jnp.int32, (half, half), 0)
    ch = lax.broadcasted_iota(jnp.int32, (half, half), 1)
    split = jnp.where(rh > ch, rh ^ ch, 0)
    levels = [1 << i for i in range(int(math.log2(half)))]

    def chunk_head(rows, hh):
        sl = slice(hh * HG_DK, (hh + 1) * HG_DK)
        f = f_ref[0, rows, sl]
        logf = jnp.log(f)
        kk = 1.0 - f
        q = q_ref[0, rows, sl]
        v = v_ref[0, rows, sl]
        vb = v.astype(BF16)
        qb = q.astype(BF16)
        kb = kk.astype(BF16)

        parts = _dot(tril, jnp.concatenate(_split3(logf * LOG2E), axis=1))
        b = parts[:, :HG_DK] + (parts[:, HG_DK:2 * HG_DK] + parts[:, 2 * HG_DK:])

        def level_operands(n):
            if n == 1:
                return qb * f.astype(BF16), kb
            e_l = jnp.exp2(-jnp.abs(b - _segment_ref(b, n))).astype(BF16)
            return qb * e_l, kb * e_l

        d_lo = jnp.zeros((half, half), F32)
        d_hi = jnp.zeros((half, half), F32)
        for n in levels:
            q_l, k_l = level_operands(n)
            d_lo = jnp.where(split >= n, _dot_nt(q_l[:half], k_l[:half]), d_lo)
            d_hi = jnp.where(split >= n, _dot_nt(q_l[half:], k_l[half:]), d_hi)
        q_l, k_l = level_operands(half)
        cross = _dot_nt(q_l[half:], k_l[:half])

        st = state_ref[hh]
        intra = jnp.concatenate(
            [_dot(d_lo.astype(BF16), vb[:half]),
             _dot(jnp.concatenate([cross, d_hi], axis=1).astype(BF16), vb)], axis=0)
        o = (intra + jnp.sum(q * kk, axis=-1, keepdims=True) * v
             + _dot_nt((q * jnp.exp2(b)).astype(BF16), st.astype(BF16)))
        b_last = b[c - 1:c, :]
        k_dec = (kk * jnp.exp2(b_last - b)).astype(BF16)
        state_ref[hh] = st * jnp.exp2(b_last) + lax.dot_general(
            vb, k_dec, (((0,), (0,)), ((), ())), preferred_element_type=F32)

        o_ref[0, rows, sl] = _rms(o, nw_ref[:, :]) * g_ref[0, rows, sl]

    for sub in range(HG_STEP_CHUNKS):
        for hh in range(HG_HEADS):
            chunk_head(slice(sub * c, (sub + 1) * c), hh)


def _hgrn(q, f, v, g, norm_w):
    bsz, seq, _ = q.shape
    step_rows = HG_STEP_CHUNKS * HG_CHUNK
    assert seq % step_rows == 0
    blk = pl.BlockSpec((1, step_rows, HG_WIDTH), lambda b, t: (b, t, 0))
    return pl.pallas_call(
        _hgrn_kernel,
        grid=(bsz, seq // step_rows),
        in_specs=[blk, blk, blk, blk, pl.BlockSpec((1, HG_DV), lambda b, t: (0, 0))],
        out_specs=blk,
        out_shape=jax.ShapeDtypeStruct((bsz, seq, HG_WIDTH), F32),
        scratch_shapes=[pltpu.VMEM((HG_HEADS, HG_DV, HG_DK), F32)],
        compiler_params=_params(("parallel", "arbitrary")),
        name="hgrn",
    )(q, f, v, g, norm_w)


def kernel(x, p, w_ffn1_gu, w_ffn1_down, w_in, w_branch_a, w_branch_b, w_out,
           w_ffn2_gu, w_ffn2_down, w_ple, w_ple_gate, norm_gains, hg_norm_w,
           lb_param, rel_table):
    bsz, seq, _ = x.shape
    t = bsz * seq
    assert t % TOKEN_TILE == 0
    h = x.reshape(t, D_MODEL)
    for i in range(p.shape[0]):
        g = norm_gains[i]
        h = _ffn(h, g, w_ffn1_gu[i].astype(BF16), w_ffn1_down[i].astype(BF16), 0, 1)
        mq, mk, mv, hq, hf, hi, hg, gates = _inproj(h, g, lb_param, w_in[i].astype(BF16), i)
        r3 = lambda a: a.reshape(bsz, seq, a.shape[-1])
        o_a = _moba(r3(mq), r3(mk), r3(mv), rel_table)
        o_b = _hgrn(r3(hq), r3(hf), r3(hi), r3(hg), hg_norm_w[i:i + 1])
        h = _post(h, o_a.reshape(t, MB_WIDTH), o_b.reshape(t, HG_WIDTH), gates,
                  p[i].reshape(t, PLE_DIM), g,
                  w_branch_a[i].astype(BF16), w_branch_b[i].astype(BF16), w_out[i].astype(BF16),
                  w_ffn2_gu[i].astype(BF16), w_ffn2_down[i].astype(BF16),
                  w_ple[i].astype(BF16), w_ple_gate[i].astype(BF16))
    return h.reshape(bsz, seq, D_MODEL)
```

```python
import functools
import math

import numpy as np
import jax
import jax.numpy as jnp
from jax import lax
from jax.experimental import pallas as pl
from jax.experimental.pallas import tpu as pltpu

F32 = jnp.float32
BF16 = jnp.bfloat16

D_MODEL = 1024
PLE_DIM = 256
D_FF = 2816
MB_HEADS = 8
MB_HEAD_DIM = 64
MB_WIDTH = MB_HEADS * MB_HEAD_DIM
MB_BLOCK = 256
MB_TOPK = 3
HG_HEADS = 4
HG_DK = 128
HG_DV = 128
HG_WIDTH = HG_HEADS * HG_DV
REL_BUCKETS = 32
REL_MAX_EXACT = REL_BUCKETS // 2
REL_MAX_DIST = 128
N_NORMS = 7
EPS = 1e-6

LANES = 128
SUBLANES = 8
VMEM_LIMIT = 56 * 1024 * 1024
TOKEN_TILE = 512
FFN_TILE = 1024
FF_CHUNK = 256
HG_CHUNK = 256
HG_STEP_CHUNKS = 4
IN_CHUNK = 512
MB_QTILE = 1024
FAR_GROUP = 2
PAD_BLOCKS = 3 * FAR_GROUP
PAD_END = PAD_BLOCKS * MB_BLOCK
PEELED = 3
ONES_ROWS = 16
NEG = -1e30
LOG2E = 1.4426950408889634


def _rms(x, g):
    return x * lax.rsqrt(jnp.mean(x * x, axis=-1, keepdims=True) + EPS) * g


def _sigmoid(x):
    return 1.0 / (1.0 + jnp.exp(-x))


def _dot(a, b):
    return jnp.dot(a, b, preferred_element_type=F32)


def _dot_nt(a, b):
    return lax.dot_general(a, b, (((1,), (1,)), ((), ())), preferred_element_type=F32)


def _split3(x):
    def top(y):
        bits = pltpu.bitcast(y, jnp.uint32) & jnp.uint32(0xFFFF0000)
        return pltpu.bitcast(bits, F32)
    hi = top(x)
    r1 = x - hi
    mid = top(r1)
    return hi.astype(BF16), mid.astype(BF16), (r1 - mid).astype(BF16)


def _const_spec(shape):
    nd = len(shape)
    return pl.BlockSpec(shape, lambda *_: (0,) * nd, pipeline_mode=pl.Buffered(1))


def _params(sem):
    return pltpu.CompilerParams(dimension_semantics=sem, vmem_limit_bytes=VMEM_LIMIT)


def _ffn_step(x, g_ref, pre, post, wgu_ref, wd_ref, act_ref):
    u = _rms(x, g_ref[pre:pre + 1, :]).astype(BF16)
    for c in range(D_FF // FF_CHUNK):
        lo = c * FF_CHUNK
        g = _dot(u, wgu_ref[:, lo:lo + FF_CHUNK])
        v = _dot(u, wgu_ref[:, D_FF + lo:D_FF + lo + FF_CHUNK])
        act_ref[:, lo:lo + FF_CHUNK] = (g * _sigmoid(g) * v).astype(BF16)
    y = _dot(act_ref[...], wd_ref[...])
    return x + 0.5 * _rms(y, g_ref[post:post + 1, :])


def _ffn_kernel(pre, post, x_ref, g_ref, wgu_ref, wd_ref, o_ref, act_ref):
    o_ref[...] = _ffn_step(x_ref[...], g_ref, pre, post, wgu_ref, wd_ref, act_ref)


def _ffn(x, gains, w_gu, w_down, pre, post):
    t = x.shape[0]
    return pl.pallas_call(
        functools.partial(_ffn_kernel, pre, post),
        grid=(t // FFN_TILE,),
        in_specs=[
            pl.BlockSpec((FFN_TILE, D_MODEL), lambda i: (i, 0)),
            _const_spec((N_NORMS, D_MODEL)),
            _const_spec((D_MODEL, 2 * D_FF)),
            _const_spec((D_FF, D_MODEL)),
        ],
        out_specs=pl.BlockSpec((FFN_TILE, D_MODEL), lambda i: (i, 0)),
        out_shape=jax.ShapeDtypeStruct((t, D_MODEL), F32),
        scratch_shapes=[pltpu.VMEM((FFN_TILE, D_FF), BF16)],
        compiler_params=_params(("parallel",)),
        name="ffn",
    )(x, gains, w_gu, w_down)


IN_SPLITS = (MB_WIDTH, MB_WIDTH, MB_WIDTH, HG_WIDTH, HG_WIDTH, HG_WIDTH, HG_WIDTH, 2 * D_MODEL)
IN_COLS = sum(IN_SPLITS)


HG_Q, HG_F, HG_G = 3, 4, 6


def _inproj_kernel(layer, h_ref, g_ref, lbp_ref, w_ref, *out_refs):
    u = _rms(h_ref[...], g_ref[2:3, :]).astype(BF16)
    lp = lbp_ref[...]
    e = jnp.exp(lp - jnp.max(lp, axis=0, keepdims=True))
    sm = e / jnp.sum(e, axis=0, keepdims=True)
    lb = sm[0:1, :]
    for r in range(1, layer + 1):
        lb = lb + sm[r:r + 1, :]
    off = 0
    for idx, ref in enumerate(out_refs):
        n = ref.shape[-1]
        for c in range(0, n, IN_CHUNK):
            z = _dot(u, w_ref[:, off + c:off + c + IN_CHUNK])
            if idx in (HG_Q, HG_G):
                z = z * _sigmoid(z)
            elif idx == HG_F:
                z = lb + (1.0 - lb) * _sigmoid(z)
            ref[:, c:c + IN_CHUNK] = z
        off += n


def _inproj(h, gains, lb_param, w_in, layer):
    t = h.shape[0]
    return pl.pallas_call(
        functools.partial(_inproj_kernel, layer),
        grid=(t // TOKEN_TILE,),
        in_specs=[
            pl.BlockSpec((TOKEN_TILE, D_MODEL), lambda i: (i, 0)),
            _const_spec((N_NORMS, D_MODEL)),
            _const_spec(lb_param.shape),
            _const_spec((D_MODEL, IN_COLS)),
        ],
        out_specs=[pl.BlockSpec((TOKEN_TILE, n), lambda i: (i, 0)) for n in IN_SPLITS],
        out_shape=[jax.ShapeDtypeStruct((t, n), F32) for n in IN_SPLITS],
        compiler_params=_params(("parallel",)),
        name="in_proj",
    )(h, gains, lb_param, w_in)


def _post_kernel(h_ref, oa_ref, ob_ref, gg_ref, p_ref, g_ref, wa_ref, wb_ref, wo_ref,
                 wgu_ref, wd_ref, wp_ref, wg_ref, o_ref, act_ref):
    a = _dot(oa_ref[...].astype(BF16), wa_ref[...])
    b = _dot(ob_ref[...].astype(BF16), wb_ref[...])
    merged = _sigmoid(gg_ref[:, :D_MODEL]) * a + _sigmoid(gg_ref[:, D_MODEL:]) * b
    h = h_ref[...] + _rms(_dot(merged.astype(BF16), wo_ref[...]), g_ref[3:4, :])
    h = _ffn_step(h, g_ref, 4, 5, wgu_ref, wd_ref, act_ref)
    e = _dot(p_ref[...].astype(BF16), wp_ref[...])
    t = _dot(h.astype(BF16), wg_ref[...])
    o_ref[...] = h + _rms(_sigmoid(t) * e, g_ref[6:7, :])


def _post(h, o_a, o_b, gates, p, gains, w_a, w_b, w_out, w_gu, w_down, w_ple, w_ple_gate):
    t = h.shape[0]
    row = lambda n: pl.BlockSpec((TOKEN_TILE, n), lambda i: (i, 0))
    return pl.pallas_call(
        _post_kernel,
        grid=(t // TOKEN_TILE,),
        in_specs=[
            row(D_MODEL), row(MB_WIDTH), row(HG_WIDTH), row(2 * D_MODEL), row(PLE_DIM),
            _const_spec((N_NORMS, D_MODEL)),
            _const_spec((MB_WIDTH, D_MODEL)),
            _const_spec((HG_WIDTH, D_MODEL)),
            _const_spec((D_MODEL, D_MODEL)),
            _const_spec((D_MODEL, 2 * D_FF)),
            _const_spec((D_FF, D_MODEL)),
            _const_spec((PLE_DIM, D_MODEL)),
            _const_spec((D_MODEL, D_MODEL)),
        ],
        out_specs=row(D_MODEL),
        out_shape=jax.ShapeDtypeStruct((t, D_MODEL), F32),
        scratch_shapes=[pltpu.VMEM((TOKEN_TILE, D_FF), BF16)],
        compiler_params=_params(("parallel",)),
        name="post",
    )(h, o_a, o_b, gates, p, gains, w_a, w_b, w_out, w_gu, w_down, w_ple, w_ple_gate)


def _bucket_starts():
    n = np.arange(0, 2 * MB_BLOCK, dtype=np.int32)
    nf = np.maximum(n, 1).astype(np.float32)
    large = REL_MAX_EXACT + (
        np.log(nf / np.float32(REL_MAX_EXACT)) / np.float32(math.log(REL_MAX_DIST / REL_MAX_EXACT))
        * np.float32(REL_BUCKETS - REL_MAX_EXACT)).astype(np.int32)
    large = np.minimum(large, REL_BUCKETS - 1)
    bucket = np.where(n < REL_MAX_EXACT, n, large)
    assert np.all(np.diff(bucket) >= 0) and bucket[-1] == REL_BUCKETS - 1
    starts = [int(np.argmax(bucket >= b)) for b in range(REL_BUCKETS)]
    assert starts[-1] <= MB_BLOCK + 1
    return starts


def _moba_kernel(starts, rel_ref, q_ref, k_ref, v_ref, o_ref,
                 kb_ref, vt_ref, kmean_ref, bown_ref, bprev_ref, sel_ref, qs_ref,
                 m_ref, acc_ref, sb0_ref, sb1_ref, mb0_ref, mb1_ref, sn_ref, mbn_ref):
    pair = pl.program_id(1)
    step = pl.program_id(2)
    seq = k_ref.shape[1]
    nb = seq // MB_BLOCK
    hd = MB_HEAD_DIM
    qscale = (hd ** -0.5) * LOG2E

    @pl.when(step == 0)
    def _prepare():
        kb_ref[0:MB_BLOCK, :] = jnp.zeros((MB_BLOCK, LANES), BF16)
        kb_ref[MB_BLOCK + seq:, :] = jnp.zeros((PAD_END, LANES), BF16)
        sel_ref[:, nb:, :] = jnp.full((2, PAD_BLOCKS, MB_QTILE), NEG, F32)
        for hh in range(2):
            vt_ref[hh, :, 0:MB_BLOCK] = jnp.zeros((hd + ONES_ROWS, MB_BLOCK), BF16)
            vt_ref[hh, :, MB_BLOCK + seq:] = jnp.zeros((hd + ONES_ROWS, PAD_END), BF16)
            vt_ref[hh, hd:, MB_BLOCK:MB_BLOCK + seq] = jnp.ones((ONES_ROWS, seq), BF16)

        def blk(j, carry):
            r0 = pl.multiple_of(j * MB_BLOCK, MB_BLOCK)
            r1 = pl.multiple_of(r0 + MB_BLOCK, MB_BLOCK)
            kblk = k_ref[0, pl.ds(r0, MB_BLOCK), :]
            kb_ref[pl.ds(r1, MB_BLOCK), :] = kblk.astype(BF16)
            kmean_ref[pl.ds(j, 1), :] = jnp.mean(kblk, axis=0, keepdims=True)
            vt = v_ref[0, pl.ds(r0, MB_BLOCK), :].T.astype(BF16)
            vt_ref[0, 0:hd, pl.ds(r1, MB_BLOCK)] = vt[0:hd]
            vt_ref[1, 0:hd, pl.ds(r1, MB_BLOCK)] = vt[hd:2 * hd]
            return carry

        lax.fori_loop(0, nb, blk, 0, unroll=4)

        kk = lax.broadcasted_iota(jnp.int32, (MB_BLOCK, MB_BLOCK), 0)
        qq = lax.broadcasted_iota(jnp.int32, (MB_BLOCK, MB_BLOCK), 1)
        d_own = qq - kk
        d_prev = d_own + MB_BLOCK
        for hh in range(2):
            head = pair * 2 + hh
            b_own = jnp.full((MB_BLOCK, MB_BLOCK), rel_ref[0, head], F32)
            b_prev = jnp.full((MB_BLOCK, MB_BLOCK), rel_ref[0, head], F32)
            for b in range(1, REL_BUCKETS):
                val = rel_ref[b, head]
                b_own = jnp.where(d_own >= starts[b], val, b_own)
                b_prev = jnp.where(d_prev >= starts[b], val, b_prev)
            bown_ref[hh] = jnp.where(d_own >= 0, b_own * LOG2E, NEG)
            bprev_ref[hh] = b_prev * LOG2E

    nq = MB_QTILE // MB_BLOCK
    qb0 = step * nq
    c_far = [rel_ref[REL_BUCKETS - 1, pair * 2 + hh] * LOG2E for hh in range(2)]

    qt = q_ref[0].T
    row_head = lax.broadcasted_iota(jnp.int32, (2 * hd, MB_QTILE), 0) // hd
    for hh in range(2):
        qs = (jnp.where(row_head == hh, qt, 0.0) * qscale).astype(BF16)
        for qi in range(nq):
            c0 = (2 * qi + hh) * MB_BLOCK
            qs_ref[:, c0:c0 + MB_BLOCK] = qs[:, qi * MB_BLOCK:(qi + 1) * MB_BLOCK]

    def select_blocks():
        km = kmean_ref[...]
        lane_head = lax.broadcasted_iota(jnp.int32, (nb, 2 * hd), 1) // hd
        km_hi, km_mid, km_lo = _split3(jnp.concatenate(
            [jnp.where(lane_head == hh, km, 0.0) for hh in range(2)], axis=0))
        q_hi, q_mid, q_lo = _split3(qt)
        gates = (_dot(km_hi, q_hi) + (_dot(km_hi, q_mid) + _dot(km_mid, q_hi))
                 + (_dot(km_hi, q_lo) + _dot(km_mid, q_mid) + _dot(km_lo, q_hi)))
        blk_row = lax.broadcasted_iota(jnp.int32, (nb, MB_QTILE), 0)
        own_blk = qb0 + lax.broadcasted_iota(jnp.int32, (nb, MB_QTILE), 1) // MB_BLOCK
        for hh in range(2):
            gate = gates[hh * nb:(hh + 1) * nb]
            gate = jnp.where(blk_row < own_blk, gate, -jnp.inf)
            sel = jnp.full((nb, MB_QTILE), NEG, F32)
            for _ in range(MB_TOPK):
                mx = jnp.max(gate, axis=0, keepdims=True)
                cand = jnp.where((gate == mx) & (mx > -jnp.inf), blk_row, nb)
                pick = blk_row == jnp.min(cand, axis=0, keepdims=True)
                sel = jnp.where(pick, 0.0, sel)
                gate = jnp.where(pick, -jnp.inf, gate)
            sel_ref[hh, 0:nb, :] = sel

    def key_slice(blocks):
        return pl.ds(pl.multiple_of(blocks[0] * MB_BLOCK, MB_BLOCK), blocks[1] * MB_BLOCK)

    def score(qi, blocks, mats, sb, mb):
        s_all = _dot(kb_ref[key_slice(blocks), :],
                     qs_ref[:, 2 * qi * MB_BLOCK:2 * (qi + 1) * MB_BLOCK])
        slots = mb.shape[0] // (2 * nq)
        for hh in range(2):
            c = 2 * qi + hh
            for u in range(blocks[1]):
                piece = s_all[u * MB_BLOCK:(u + 1) * MB_BLOCK, hh * MB_BLOCK:(hh + 1) * MB_BLOCK]
                if mats[hh][u] is not None:
                    piece = piece + mats[hh][u]
                mb[c * slots + u:c * slots + u + 1, :] = jnp.max(piece, axis=0, keepdims=True)
                sb[u * MB_BLOCK:(u + 1) * MB_BLOCK, c * MB_BLOCK:(c + 1) * MB_BLOCK] = piece

    def consume(qi, blocks, rows, sb, mb, first):
        kpos = key_slice(blocks)
        slots = mb.shape[0] // (2 * nq)
        for hh in range(2):
            c = 2 * qi + hh
            m_blk = None
            for u in range(blocks[1]):
                m_u = mb[c * slots + u:c * slots + u + 1, :]
                if rows[hh][u] is not None:
                    m_u = m_u + rows[hh][u]
                m_blk = m_u if m_blk is None else jnp.maximum(m_blk, m_u)
            m_old = None if first else m_ref[c:c + 1, :]
            m_new = m_blk if first else jnp.maximum(m_old, m_blk)
            p = jnp.concatenate(
                [jnp.exp2(sb[u * MB_BLOCK:(u + 1) * MB_BLOCK, c * MB_BLOCK:(c + 1) * MB_BLOCK]
                          + ((-m_new) if rows[hh][u] is None else (rows[hh][u] - m_new)))
                 for u in range(blocks[1])], axis=0)
            pv = _dot(vt_ref[hh, :, kpos], p.astype(BF16))
            acc_ref[c] = pv if first else acc_ref[c] * jnp.exp2(m_old - m_new) + pv
            m_ref[c:c + 1, :] = m_new

    def far_row(hh, qi, j, limit):
        row = sel_ref[hh, pl.ds(jnp.maximum(j, 0), 1), qi * MB_BLOCK:(qi + 1) * MB_BLOCK]
        return jnp.where((j >= 0) & (j < limit), row + c_far[hh], NEG)

    def near_blocks(qi):
        return qb0, qi + 2

    def score_near(qi):
        mats = [[None] * qi + [bprev_ref[hh], bown_ref[hh]] for hh in range(2)]
        score(qi, near_blocks(qi), mats, sn_ref, mbn_ref)

    def consume_near(qi):
        qb = qb0 + qi
        rows = []
        for hh in range(2):
            far = [far_row(hh, qi, qb0 - 1 + u, qb - 1) for u in range(qi)]
            prev = sel_ref[hh, pl.ds(jnp.maximum(qb - 1, 0), 1), qi * MB_BLOCK:(qi + 1) * MB_BLOCK]
            rows.append(far + [jnp.where(qb >= 1, prev, NEG), None])
        consume(qi, near_blocks(qi), rows, sn_ref, mbn_ref, True)

    nfar = qb0 - 1

    def far_blocks(g, n=FAR_GROUP):
        return g * FAR_GROUP + 1, n

    def score_far(g, qi, sb, mb, n=FAR_GROUP):
        score(qi, far_blocks(g, n), [[None] * n] * 2, sb, mb)

    def consume_far(g, qi, sb, mb, n=FAR_GROUP):
        rows = [[far_row(hh, qi, g * FAR_GROUP + u, nfar) for u in range(n)]
                for hh in range(2)]
        consume(qi, far_blocks(g, n), rows, sb, mb, False)

    score_near(0)
    score_near(1)
    select_blocks()
    consume_near(0)
    for qi in range(2, nq):
        score_near(qi)
        consume_near(qi - 1)
    score_far(0, 0, sb0_ref, mb0_ref)
    consume_near(nq - 1)
    for qi in range(1, nq):
        score_far(0, qi, sb0_ref, mb0_ref)

    assert nq % (2 * FAR_GROUP) == 0 and FAR_GROUP > 1

    def two_groups(t, last):
        n_last = FAR_GROUP - 1 if last else FAR_GROUP
        for qi in range(nq):
            consume_far(2 * t, qi, sb0_ref, mb0_ref)
            score_far(2 * t + 1, qi, sb1_ref, mb1_ref, n_last)
        for qi in range(nq):
            consume_far(2 * t + 1, qi, sb1_ref, mb1_ref, n_last)
            if not last:
                score_far(2 * t + 2, qi, sb0_ref, mb0_ref)

    trips = (jnp.maximum(nfar, 0) + 2 * FAR_GROUP - 1) // (2 * FAR_GROUP)
    lax.fori_loop(0, trips - PEELED, lambda t, carry: (two_groups(t, False), carry)[1], 0)
    for n in range(1, PEELED + 1):
        @pl.when((trips == n) if n < PEELED else (trips >= n))
        def _tail(n=n):
            for i in range(n, 0, -1):
                two_groups(trips - i, i == 1)

    for qi in range(nq):
        outs = [acc_ref[2 * qi + hh, 0:hd, :] / acc_ref[2 * qi + hh, hd:hd + 1, :]
                for hh in range(2)]
        o_ref[0, qi * MB_BLOCK:(qi + 1) * MB_BLOCK, :] = jnp.concatenate(outs, axis=0).T


def _moba(q, k, v, rel_table):
    bsz, seq, _ = q.shape
    assert seq % MB_QTILE == 0
    nb = seq // MB_BLOCK
    npair = MB_HEADS // 2
    padded = MB_BLOCK + seq + PAD_END
    chains = 2 * MB_QTILE // MB_BLOCK
    return pl.pallas_call(
        functools.partial(_moba_kernel, _bucket_starts()),
        grid=(bsz, npair, seq // MB_QTILE),
        in_specs=[
            pl.BlockSpec(memory_space=pltpu.SMEM),
            pl.BlockSpec((1, MB_QTILE, LANES), lambda b, p, t: (b, t, p)),
            pl.BlockSpec((1, seq, LANES), lambda b, p, t: (b, 0, p)),
            pl.BlockSpec((1, seq, LANES), lambda b, p, t: (b, 0, p)),
        ],
        out_specs=pl.BlockSpec((1, MB_QTILE, LANES), lambda b, p, t: (b, t, p)),
        out_shape=jax.ShapeDtypeStruct((bsz, seq, MB_WIDTH), F32),
        scratch_shapes=[
            pltpu.VMEM((padded, LANES), BF16),
            pltpu.VMEM((2, MB_HEAD_DIM + ONES_ROWS, padded), BF16),
            pltpu.VMEM((nb, LANES), F32),
            pltpu.VMEM((2, MB_BLOCK, MB_BLOCK), F32),
            pltpu.VMEM((2, MB_BLOCK, MB_BLOCK), F32),
            pltpu.VMEM((2, nb + PAD_BLOCKS, MB_QTILE), F32),
            pltpu.VMEM((LANES, 2 * MB_QTILE), BF16),
            pltpu.VMEM((2 * MB_QTILE // MB_BLOCK, MB_BLOCK), F32),
            pltpu.VMEM((chains, MB_HEAD_DIM + ONES_ROWS, MB_BLOCK), F32),
            pltpu.VMEM((FAR_GROUP * MB_BLOCK, 2 * MB_QTILE), F32),
            pltpu.VMEM((FAR_GROUP * MB_BLOCK, 2 * MB_QTILE), F32),
            pltpu.VMEM((chains * FAR_GROUP, MB_BLOCK), F32),
            pltpu.VMEM((chains * FAR_GROUP, MB_BLOCK), F32),
            pltpu.VMEM((MB_QTILE + MB_BLOCK, 2 * MB_QTILE), F32),
            pltpu.VMEM((chains * (chains // 2 + 1), MB_BLOCK), F32),
        ],
        compiler_params=_params(("parallel", "parallel", "arbitrary")),
        name="moba",
    )(rel_table, q, k, v)


def _segment_ref(b, n):
    c = b.shape[0]
    if 2 * n >= SUBLANES:
        b3 = b.reshape(c // (2 * n), 2 * n, b.shape[1])
        return jnp.broadcast_to(b3[:, n - 1:n, :], b3.shape).reshape(b.shape)
    b3 = b.reshape(c // SUBLANES, SUBLANES, b.shape[1])
    seg = lax.broadcasted_iota(jnp.int32, b3.shape, 1) // (2 * n)
    out = jnp.broadcast_to(b3[:, n - 1:n, :], b3.shape)
    for s in range(1, SUBLANES // (2 * n)):
        r = s * 2 * n + n - 1
        out = jnp.where(seg == s, jnp.broadcast_to(b3[:, r:r + 1, :], b3.shape), out)
    return out.reshape(b.shape)


def _hgrn_kernel(q_ref, f_ref, v_ref, g_ref, nw_ref, o_ref, state_ref):
    c = HG_CHUNK

    @pl.when(pl.program_id(1) == 0)
    def _reset():
        state_ref[...] = jnp.zeros_like(state_ref)

    half = c // 2
    row = lax.broadcasted_iota(jnp.int32, (c, c), 0)
    col = lax.broadcasted_iota(jnp.int32, (c, c), 1)
    tril = jnp.where(row >= col, 1.0, 0.0).astype(BF16)
    rh = lax.broadcasted_iota(jnp.int32, (half, half), 0)
    ch = lax.broadcasted_iota(jnp.int32, (half, half), 1)
    split = jnp.where(rh > ch, rh ^ ch, 0)
    levels = [1 << i for i in range(int(math.log2(half)))]

    def chunk_head(rows, hh):
        sl = slice(hh * HG_DK, (hh + 1) * HG_DK)
        f = f_ref[0, rows, sl]
        logf = jnp.log(f)
        kk = 1.0 - f
        q = q_ref[0, rows, sl]
        v = v_ref[0, rows, sl]
        vb = v.astype(BF16)
        qb = q.astype(BF16)
        kb = kk.astype(BF16)

        parts = _dot(tril, jnp.concatenate(_split3(logf * LOG2E), axis=1))
        b = parts[:, :HG_DK] + (parts[:, HG_DK:2 * HG_DK] + parts[:, 2 * HG_DK:])

        def level_operands(n):
            if n == 1:
                return qb * f.astype(BF16), kb
            e_l = jnp.exp2(-jnp.abs(b - _segment_ref(b, n))).astype(BF16)
            return qb * e_l, kb * e_l

        d_lo = jnp.zeros((half, half), F32)
        d_hi = jnp.zeros((half, half), F32)
        for n in levels:
            q_l, k_l = level_operands(n)
            d_lo = jnp.where(split >= n, _dot_nt(q_l[:half], k_l[:half]), d_lo)
            d_hi = jnp.where(split >= n, _dot_nt(q_l[half:], k_l[half:]), d_hi)
        q_l, k_l = level_operands(half)
        cross = _dot_nt(q_l[half:], k_l[:half])

        st = state_ref[hh]
        intra = jnp.concatenate(
            [_dot(d_lo.astype(BF16), vb[:half]),
             _dot(jnp.concatenate([cross, d_hi], axis=1).astype(BF16), vb)], axis=0)
        o = (intra + jnp.sum(q * kk, axis=-1, keepdims=True) * v
             + _dot_nt((q * jnp.exp2(b)).astype(BF16), st.astype(BF16)))
        b_last = b[c - 1:c, :]
        k_dec = (kk * jnp.exp2(b_last - b)).astype(BF16)
        state_ref[hh] = st * jnp.exp2(b_last) + lax.dot_general(
            vb, k_dec, (((0,), (0,)), ((), ())), preferred_element_type=F32)

        o_ref[0, rows, sl] = _rms(o, nw_ref[:, :]) * g_ref[0, rows, sl]

    for sub in range(HG_STEP_CHUNKS):
        for hh in range(HG_HEADS):
            chunk_head(slice(sub * c, (sub + 1) * c), hh)


def _hgrn(q, f, v, g, norm_w):
    bsz, seq, _ = q.shape
    step_rows = HG_STEP_CHUNKS * HG_CHUNK
    assert seq % step_rows == 0
    blk = pl.BlockSpec((1, step_rows, HG_WIDTH), lambda b, t: (b, t, 0))
    return pl.pallas_call(
        _hgrn_kernel,
        grid=(bsz, seq // step_rows),
        in_specs=[blk, blk, blk, blk, pl.BlockSpec((1, HG_DV), lambda b, t: (0, 0))],
        out_specs=blk,
        out_shape=jax.ShapeDtypeStruct((bsz, seq, HG_WIDTH), F32),
        scratch_shapes=[pltpu.VMEM((HG_HEADS, HG_DV, HG_DK), F32)],
        compiler_params=_params(("parallel", "arbitrary")),
        name="hgrn",
    )(q, f, v, g, norm_w)


def kernel(x, p, w_ffn1_gu, w_ffn1_down, w_in, w_branch_a, w_branch_b, w_out,
           w_ffn2_gu, w_ffn2_down, w_ple, w_ple_gate, norm_gains, hg_norm_w,
           lb_param, rel_table):
    bsz, seq, _ = x.shape
    t = bsz * seq
    assert t % TOKEN_TILE == 0
    h = x.reshape(t, D_MODEL)
    for i in range(p.shape[0]):
        g = norm_gains[i]
        h = _ffn(h, g, w_ffn1_gu[i].astype(BF16), w_ffn1_down[i].astype(BF16), 0, 1)
        mq, mk, mv, hq, hf, hi, hg, gates = _inproj(h, g, lb_param, w_in[i].astype(BF16), i)
        r3 = lambda a: a.reshape(bsz, seq, a.shape[-1])
        o_a = _moba(r3(mq), r3(mk), r3(mv), rel_table)
        o_b = _hgrn(r3(hq), r3(hf), r3(hi), r3(hg), hg_norm_w[i:i + 1])
        h = _post(h, o_a.reshape(t, MB_WIDTH), o_b.reshape(t, HG_WIDTH), gates,
                  p[i].reshape(t, PLE_DIM), g,
                  w_branch_a[i].astype(BF16), w_branch_b[i].astype(BF16), w_out[i].astype(BF16),
                  w_ffn2_gu[i].astype(BF16), w_ffn2_down[i].astype(BF16),
                  w_ple[i].astype(BF16), w_ple_gate[i].astype(BF16))
    return h.reshape(bsz, seq, D_MODEL)
```

```python
import functools
import math

import numpy as np
import jax
import jax.numpy as jnp
from jax import lax
from jax.experimental import pallas as pl
from jax.experimental.pallas import tpu as pltpu

F32 = jnp.float32
BF16 = jnp.bfloat16

D_MODEL = 1024
PLE_DIM = 256
D_FF = 2816
MB_HEADS = 8
MB_HEAD_DIM = 64
MB_WIDTH = MB_HEADS * MB_HEAD_DIM
MB_BLOCK = 256
MB_TOPK = 3
HG_HEADS = 4
HG_DK = 128
HG_DV = 128
HG_WIDTH = HG_HEADS * HG_DV
REL_BUCKETS = 32
REL_MAX_EXACT = REL_BUCKETS // 2
REL_MAX_DIST = 128
N_NORMS = 7
EPS = 1e-6

LANES = 128
SUBLANES = 8
VMEM_LIMIT = 56 * 1024 * 1024
TOKEN_TILE = 512
FFN_TILE = 1024
FF_CHUNK = 256
HG_CHUNK = 256
HG_STEP_CHUNKS = 4
IN_CHUNK = 512
MB_QTILE = 1024
FAR_GROUP = 2
PAD_BLOCKS = 3 * FAR_GROUP
PAD_END = PAD_BLOCKS * MB_BLOCK
PEELED = 3
ONES_ROWS = 16
NEG = -1e30
LOG2E = 1.4426950408889634


def _rms(x, g):
    return x * lax.rsqrt(jnp.mean(x * x, axis=-1, keepdims=True) + EPS) * g


def _sigmoid(x):
    return 1.0 / (1.0 + jnp.exp(-x))


def _dot(a, b):
    return jnp.dot(a, b, preferred_element_type=F32)


def _dot_nt(a, b):
    return lax.dot_general(a, b, (((1,), (1,)), ((), ())), preferred_element_type=F32)


def _split3(x):
    def top(y):
        bits = pltpu.bitcast(y, jnp.uint32) & jnp.uint32(0xFFFF0000)
        return pltpu.bitcast(bits, F32)
    hi = top(x)
    r1 = x - hi
    mid = top(r1)
    return hi.astype(BF16), mid.astype(BF16), (r1 - mid).astype(BF16)


def _const_spec(shape):
    nd = len(shape)
    return pl.BlockSpec(shape, lambda *_: (0,) * nd, pipeline_mode=pl.Buffered(1))


def _params(sem):
    return pltpu.CompilerParams(dimension_semantics=sem, vmem_limit_bytes=VMEM_LIMIT)


def _ffn_step(x, g_ref, pre, post, wgu_ref, wd_ref, act_ref):
    u = _rms(x, g_ref[pre:pre + 1, :]).astype(BF16)
    for c in range(D_FF // FF_CHUNK):
        lo = c * FF_CHUNK
        g = _dot(u, wgu_ref[:, lo:lo + FF_CHUNK])
        v = _dot(u, wgu_ref[:, D_FF + lo:D_FF + lo + FF_CHUNK])
        act_ref[:, lo:lo + FF_CHUNK] = (g * _sigmoid(g) * v).astype(BF16)
    y = _dot(act_ref[...], wd_ref[...])
    return x + 0.5 * _rms(y, g_ref[post:post + 1, :])


def _ffn_kernel(pre, post, x_ref, g_ref, wgu_ref, wd_ref, o_ref, act_ref):
    o_ref[...] = _ffn_step(x_ref[...], g_ref, pre, post, wgu_ref, wd_ref, act_ref)


def _ffn(x, gains, w_gu, w_down, pre, post):
    t = x.shape[0]
    return pl.pallas_call(
        functools.partial(_ffn_kernel, pre, post),
        grid=(t // FFN_TILE,),
        in_specs=[
            pl.BlockSpec((FFN_TILE, D_MODEL), lambda i: (i, 0)),
            _const_spec((N_NORMS, D_MODEL)),
            _const_spec((D_MODEL, 2 * D_FF)),
            _const_spec((D_FF, D_MODEL)),
        ],
        out_specs=pl.BlockSpec((FFN_TILE, D_MODEL), lambda i: (i, 0)),
        out_shape=jax.ShapeDtypeStruct((t, D_MODEL), F32),
        scratch_shapes=[pltpu.VMEM((FFN_TILE, D_FF), BF16)],
        compiler_params=_params(("parallel",)),
        name="ffn",
    )(x, gains, w_gu, w_down)


IN_SPLITS = (MB_WIDTH, MB_WIDTH, MB_WIDTH, HG_WIDTH, HG_WIDTH, HG_WIDTH, HG_WIDTH, 2 * D_MODEL)
IN_COLS = sum(IN_SPLITS)


HG_Q, HG_F, HG_G = 3, 4, 6


def _inproj_kernel(layer, h_ref, g_ref, lbp_ref, w_ref, *out_refs):
    u = _rms(h_ref[...], g_ref[2:3, :]).astype(BF16)
    lp = lbp_ref[...]
    e = jnp.exp(lp - jnp.max(lp, axis=0, keepdims=True))
    sm = e / jnp.sum(e, axis=0, keepdims=True)
    lb = sm[0:1, :]
    for r in range(1, layer + 1):
        lb = lb + sm[r:r + 1, :]
    off = 0
    for idx, ref in enumerate(out_refs):
        n = ref.shape[-1]
        for c in range(0, n, IN_CHUNK):
            z = _dot(u, w_ref[:, off + c:off + c + IN_CHUNK])
            if idx in (HG_Q, HG_G):
                z = z * _sigmoid(z)
            elif idx == HG_F:
                z = lb + (1.0 - lb) * _sigmoid(z)
            ref[:, c:c + IN_CHUNK] = z
        off += n


def _inproj(h, gains, lb_param, w_in, layer):
    t = h.shape[0]
    return pl.pallas_call(
        functools.partial(_inproj_kernel, layer),
        grid=(t // TOKEN_TILE,),
        in_specs=[
            pl.BlockSpec((TOKEN_TILE, D_MODEL), lambda i: (i, 0)),
            _const_spec((N_NORMS, D_MODEL)),
            _const_spec(lb_param.shape),
            _const_spec((D_MODEL, IN_COLS)),
        ],
        out_specs=[pl.BlockSpec((TOKEN_TILE, n), lambda i: (i, 0)) for n in IN_SPLITS],
        out_shape=[jax.ShapeDtypeStruct((t, n), F32) for n in IN_SPLITS],
        compiler_params=_params(("parallel",)),
        name="in_proj",
    )(h, gains, lb_param, w_in)


def _post_kernel(h_ref, oa_ref, ob_ref, gg_ref, p_ref, g_ref, wa_ref, wb_ref, wo_ref,
                 wgu_ref, wd_ref, wp_ref, wg_ref, o_ref, act_ref):
    a = _dot(oa_ref[...].astype(BF16), wa_ref[...])
    b = _dot(ob_ref[...].astype(BF16), wb_ref[...])
    merged = _sigmoid(gg_ref[:, :D_MODEL]) * a + _sigmoid(gg_ref[:, D_MODEL:]) * b
    h = h_ref[...] + _rms(_dot(merged.astype(BF16), wo_ref[...]), g_ref[3:4, :])
    h = _ffn_step(h, g_ref, 4, 5, wgu_ref, wd_ref, act_ref)
    e = _dot(p_ref[...].astype(BF16), wp_ref[...])
    t = _dot(h.astype(BF16), wg_ref[...])
    o_ref[...] = h + _rms(_sigmoid(t) * e, g_ref[6:7, :])


def _post(h, o_a, o_b, gates, p, gains, w_a, w_b, w_out, w_gu, w_down, w_ple, w_ple_gate):
    t = h.shape[0]
    row = lambda n: pl.BlockSpec((TOKEN_TILE, n), lambda i: (i, 0))
    return pl.pallas_call(
        _post_kernel,
        grid=(t // TOKEN_TILE,),
        in_specs=[
            row(D_MODEL), row(MB_WIDTH), row(HG_WIDTH), row(2 * D_MODEL), row(PLE_DIM),
            _const_spec((N_NORMS, D_MODEL)),
            _const_spec((MB_WIDTH, D_MODEL)),
            _const_spec((HG_WIDTH, D_MODEL)),
            _const_spec((D_MODEL, D_MODEL)),
            _const_spec((D_MODEL, 2 * D_FF)),
            _const_spec((D_FF, D_MODEL)),
            _const_spec((PLE_DIM, D_MODEL)),
            _const_spec((D_MODEL, D_MODEL)),
        ],
        out_specs=row(D_MODEL),
        out_shape=jax.ShapeDtypeStruct((t, D_MODEL), F32),
        scratch_shapes=[pltpu.VMEM((TOKEN_TILE, D_FF), BF16)],
        compiler_params=_params(("parallel",)),
        name="post",
    )(h, o_a, o_b, gates, p, gains, w_a, w_b, w_out, w_gu, w_down, w_ple, w_ple_gate)


def _bucket_starts():
    n = np.arange(0, 2 * MB_BLOCK, dtype=np.int32)
    nf = np.maximum(n, 1).astype(np.float32)
    large = REL_MAX_EXACT + (
        np.log(nf / np.float32(REL_MAX_EXACT)) / np.float32(math.log(REL_MAX_DIST / REL_MAX_EXACT))
        * np.float32(REL_BUCKETS - REL_MAX_EXACT)).astype(np.int32)
    large = np.minimum(large, REL_BUCKETS - 1)
    bucket = np.where(n < REL_MAX_EXACT, n, large)
    assert np.all(np.diff(bucket) >= 0) and bucket[-1] == REL_BUCKETS - 1
    starts = [int(np.argmax(bucket >= b)) for b in range(REL_BUCKETS)]
    assert starts[-1] <= MB_BLOCK + 1
    return starts


def _moba_kernel(starts, rel_ref, q_ref, k_ref, v_ref, o_ref,
                 kb_ref, vt_ref, kmean_ref, bown_ref, bprev_ref, sel_ref, qs_ref,
                 m_ref, acc_ref, sb0_ref, sb1_ref, mb0_ref, mb1_ref, sn_ref, mbn_ref):
    pair = pl.program_id(1)
    step = pl.program_id(2)
    seq = k_ref.shape[1]
    nb = seq // MB_BLOCK
    hd = MB_HEAD_DIM
    qscale = (hd ** -0.5) * LOG2E

    @pl.when(step == 0)
    def _prepare():
        kb_ref[0:MB_BLOCK, :] = jnp.zeros((MB_BLOCK, LANES), BF16)
        kb_ref[MB_BLOCK + seq:, :] = jnp.zeros((PAD_END, LANES), BF16)
        sel_ref[:, nb:, :] = jnp.full((2, PAD_BLOCKS, MB_QTILE), NEG, F32)
        for hh in range(2):
            vt_ref[hh, :, 0:MB_BLOCK] = jnp.zeros((hd + ONES_ROWS, MB_BLOCK), BF16)
            vt_ref[hh, :, MB_BLOCK + seq:] = jnp.zeros((hd + ONES_ROWS, PAD_END), BF16)
            vt_ref[hh, hd:, MB_BLOCK:MB_BLOCK + seq] = jnp.ones((ONES_ROWS, seq), BF16)

        def blk(j, carry):
            r0 = pl.multiple_of(j * MB_BLOCK, MB_BLOCK)
            r1 = pl.multiple_of(r0 + MB_BLOCK, MB_BLOCK)
            kblk = k_ref[0, pl.ds(r0, MB_BLOCK), :]
            kb_ref[pl.ds(r1, MB_BLOCK), :] = kblk.astype(BF16)
            kmean_ref[pl.ds(j, 1), :] = jnp.mean(kblk, axis=0, keepdims=True)
            vt = v_ref[0, pl.ds(r0, MB_BLOCK), :].T.astype(BF16)
            vt_ref[0, 0:hd, pl.ds(r1, MB_BLOCK)] = vt[0:hd]
            vt_ref[1, 0:hd, pl.ds(r1, MB_BLOCK)] = vt[hd:2 * hd]
            return carry

        lax.fori_loop(0, nb, blk, 0, unroll=4)

        kk = lax.broadcasted_iota(jnp.int32, (MB_BLOCK, MB_BLOCK), 0)
        qq = lax.broadcasted_iota(jnp.int32, (MB_BLOCK, MB_BLOCK), 1)
        d_own = qq - kk
        d_prev = d_own + MB_BLOCK
        for hh in range(2):
            head = pair * 2 + hh
            b_own = jnp.full((MB_BLOCK, MB_BLOCK), rel_ref[0, head], F32)
            b_prev = jnp.full((MB_BLOCK, MB_BLOCK), rel_ref[0, head], F32)
            for b in range(1, REL_BUCKETS):
                val = rel_ref[b, head]
                b_own = jnp.where(d_own >= starts[b], val, b_own)
                b_prev = jnp.where(d_prev >= starts[b], val, b_prev)
            bown_ref[hh] = jnp.where(d_own >= 0, b_own * LOG2E, NEG)
            bprev_ref[hh] = b_prev * LOG2E

    nq = MB_QTILE // MB_BLOCK
    qb0 = step * nq
    c_far = [rel_ref[REL_BUCKETS - 1, pair * 2 + hh] * LOG2E for hh in range(2)]

    qt = q_ref[0].T
    row_head = lax.broadcasted_iota(jnp.int32, (2 * hd, MB_QTILE), 0) // hd
    for hh in range(2):
        qs = (jnp.where(row_head == hh, qt, 0.0) * qscale).astype(BF16)
        for qi in range(nq):
            c0 = (2 * qi + hh) * MB_BLOCK
            qs_ref[:, c0:c0 + MB_BLOCK] = qs[:, qi * MB_BLOCK:(qi + 1) * MB_BLOCK]

    def select_blocks():
        km = kmean_ref[...]
        lane_head = lax.broadcasted_iota(jnp.int32, (nb, 2 * hd), 1) // hd
        km_hi, km_mid, km_lo = _split3(jnp.concatenate(
            [jnp.where(lane_head == hh, km, 0.0) for hh in range(2)], axis=0))
        q_hi, q_mid, q_lo = _split3(qt)
        gates = (_dot(km_hi, q_hi) + (_dot(km_hi, q_mid) + _dot(km_mid, q_hi))
                 + (_dot(km_hi, q_lo) + _dot(km_mid, q_mid) + _dot(km_lo, q_hi)))
        blk_row = lax.broadcasted_iota(jnp.int32, (nb, MB_QTILE), 0)
        own_blk = qb0 + lax.broadcasted_iota(jnp.int32, (nb, MB_QTILE), 1) // MB_BLOCK
        for hh in range(2):
            gate = gates[hh * nb:(hh + 1) * nb]
            gate = jnp.where(blk_row < own_blk, gate, -jnp.inf)
            sel = jnp.full((nb, MB_QTILE), NEG, F32)
            for _ in range(MB_TOPK):
                mx = jnp.max(gate, axis=0, keepdims=True)
                cand = jnp.where((gate == mx) & (mx > -jnp.inf), blk_row, nb)
                pick = blk_row == jnp.min(cand, axis=0, keepdims=True)
                sel = jnp.where(pick, 0.0, sel)
                gate = jnp.where(pick, -jnp.inf, gate)
            sel_ref[hh, 0:nb, :] = sel

    def key_slice(blocks):
        return pl.ds(pl.multiple_of(blocks[0] * MB_BLOCK, MB_BLOCK), blocks[1] * MB_BLOCK)

    def score(qi, blocks, mats, sb, mb):
        s_all = _dot(kb_ref[key_slice(blocks), :],
                     qs_ref[:, 2 * qi * MB_BLOCK:2 * (qi + 1) * MB_BLOCK])
        slots = mb.shape[0] // (2 * nq)
        for hh in range(2):
            c = 2 * qi + hh
            for u in range(blocks[1]):
                piece = s_all[u * MB_BLOCK:(u + 1) * MB_BLOCK, hh * MB_BLOCK:(hh + 1) * MB_BLOCK]
                if mats[hh][u] is not None:
                    piece = piece + mats[hh][u]
                mb[c * slots + u:c * slots + u + 1, :] = jnp.max(piece, axis=0, keepdims=True)
                sb[u * MB_BLOCK:(u + 1) * MB_BLOCK, c * MB_BLOCK:(c + 1) * MB_BLOCK] = piece

    def consume(qi, blocks, rows, sb, mb, first):
        kpos = key_slice(blocks)
        slots = mb.shape[0] // (2 * nq)
        for hh in range(2):
            c = 2 * qi + hh
            m_blk = None
            for u in range(blocks[1]):
                m_u = mb[c * slots + u:c * slots + u + 1, :]
                if rows[hh][u] is not None:
                    m_u = m_u + rows[hh][u]
                m_blk = m_u if m_blk is None else jnp.maximum(m_blk, m_u)
            m_old = None if first else m_ref[c:c + 1, :]
            m_new = m_blk if first else jnp.maximum(m_old, m_blk)
            p = jnp.concatenate(
                [jnp.exp2(sb[u * MB_BLOCK:(u + 1) * MB_BLOCK, c * MB_BLOCK:(c + 1) * MB_BLOCK]
                          + ((-m_new) if rows[hh][u] is None else (rows[hh][u] - m_new)))
                 for u in range(blocks[1])], axis=0)
            pv = _dot(vt_ref[hh, :, kpos], p.astype(BF16))
            acc_ref[c] = pv if first else acc_ref[c] * jnp.exp2(m_old - m_new) + pv
            m_ref[c:c + 1, :] = m_new

    def far_row(hh, qi, j, limit):
        row = sel_ref[hh, pl.ds(jnp.maximum(j, 0), 1), qi * MB_BLOCK:(qi + 1) * MB_BLOCK]
        return jnp.where((j >= 0) & (j < limit), row + c_far[hh], NEG)

    def near_blocks(qi):
        return qb0, qi + 2

    def score_near(qi):
        mats = [[None] * qi + [bprev_ref[hh], bown_ref[hh]] for hh in range(2)]
        score(qi, near_blocks(qi), mats, sn_ref, mbn_ref)

    def consume_near(qi):
        qb = qb0 + qi
        rows = []
        for hh in range(2):
            far = [far_row(hh, qi, qb0 - 1 + u, qb - 1) for u in range(qi)]
            prev = sel_ref[hh, pl.ds(jnp.maximum(qb - 1, 0), 1), qi * MB_BLOCK:(qi + 1) * MB_BLOCK]
            rows.append(far + [jnp.where(qb >= 1, prev, NEG), None])
        consume(qi, near_blocks(qi), rows, sn_ref, mbn_ref, True)

    nfar = qb0 - 1

    def far_blocks(g, n=FAR_GROUP):
        return g * FAR_GROUP + 1, n

    def score_far(g, qi, sb, mb, n=FAR_GROUP):
        score(qi, far_blocks(g, n), [[None] * n] * 2, sb, mb)

    def consume_far(g, qi, sb, mb, n=FAR_GROUP):
        rows = [[far_row(hh, qi, g * FAR_GROUP + u, nfar) for u in range(n)]
                for hh in range(2)]
        consume(qi, far_blocks(g, n), rows, sb, mb, False)

    for qi in range(nq):
        score_near(qi)
    select_blocks()
    for qi in range(nq):
        consume_near(qi)
        score_far(0, qi, sb0_ref, mb0_ref)

    assert nq % (2 * FAR_GROUP) == 0 and FAR_GROUP > 1

    def two_groups(t, last):
        n_last = FAR_GROUP - 1 if last else FAR_GROUP
        for qi in range(nq):
            score_far(2 * t + 1, qi, sb1_ref, mb1_ref, n_last)
            consume_far(2 * t, qi, sb0_ref, mb0_ref)
        for qi in range(nq):
            if not last:
                score_far(2 * t + 2, qi, sb0_ref, mb0_ref)
            consume_far(2 * t + 1, qi, sb1_ref, mb1_ref, n_last)

    trips = (jnp.maximum(nfar, 0) + 2 * FAR_GROUP - 1) // (2 * FAR_GROUP)
    lax.fori_loop(0, trips - PEELED, lambda t, carry: (two_groups(t, False), carry)[1], 0)
    for n in range(1, PEELED + 1):
        @pl.when((trips == n) if n < PEELED else (trips >= n))
        def _tail(n=n):
            for i in range(n, 0, -1):
                two_groups(trips - i, i == 1)

    for qi in range(nq):
        outs = [acc_ref[2 * qi + hh, 0:hd, :] / acc_ref[2 * qi + hh, hd:hd + 1, :]
                for hh in range(2)]
        o_ref[0, qi * MB_BLOCK:(qi + 1) * MB_BLOCK, :] = jnp.concatenate(outs, axis=0).T


def _moba(q, k, v, rel_table):
    bsz, seq, _ = q.shape
    assert seq % MB_QTILE == 0
    nb = seq // MB_BLOCK
    npair = MB_HEADS // 2
    padded = MB_BLOCK + seq + PAD_END
    chains = 2 * MB_QTILE // MB_BLOCK
    return pl.pallas_call(
        functools.partial(_moba_kernel, _bucket_starts()),
        grid=(bsz, npair, seq // MB_QTILE),
        in_specs=[
            pl.BlockSpec(memory_space=pltpu.SMEM),
            pl.BlockSpec((1, MB_QTILE, LANES), lambda b, p, t: (b, t, p)),
            pl.BlockSpec((1, seq, LANES), lambda b, p, t: (b, 0, p)),
            pl.BlockSpec((1, seq, LANES), lambda b, p, t: (b, 0, p)),
        ],
        out_specs=pl.BlockSpec((1, MB_QTILE, LANES), lambda b, p, t: (b, t, p)),
        out_shape=jax.ShapeDtypeStruct((bsz, seq, MB_WIDTH), F32),
        scratch_shapes=[
            pltpu.VMEM((padded, LANES), BF16),
            pltpu.VMEM((2, MB_HEAD_DIM + ONES_ROWS, padded), BF16),
            pltpu.VMEM((nb, LANES), F32),
            pltpu.VMEM((2, MB_BLOCK, MB_BLOCK), F32),
            pltpu.VMEM((2, MB_BLOCK, MB_BLOCK), F32),
            pltpu.VMEM((2, nb + PAD_BLOCKS, MB_QTILE), F32),
            pltpu.VMEM((LANES, 2 * MB_QTILE), BF16),
            pltpu.VMEM((2 * MB_QTILE // MB_BLOCK, MB_BLOCK), F32),
            pltpu.VMEM((chains, MB_HEAD_DIM + ONES_ROWS, MB_BLOCK), F32),
            pltpu.VMEM((FAR_GROUP * MB_BLOCK, 2 * MB_QTILE), F32),
            pltpu.VMEM((FAR_GROUP * MB_BLOCK, 2 * MB_QTILE), F32),
            pltpu.VMEM((chains * FAR_GROUP, MB_BLOCK), F32),
            pltpu.VMEM((chains * FAR_GROUP, MB_BLOCK), F32),
            pltpu.VMEM((MB_QTILE + MB_BLOCK, 2 * MB_QTILE), F32),
            pltpu.VMEM((chains * (chains // 2 + 1), MB_BLOCK), F32),
        ],
        compiler_params=_params(("parallel", "parallel", "arbitrary")),
        name="moba",
    )(rel_table, q, k, v)


def _segment_ref(b, n):
    c = b.shape[0]
    if 2 * n >= SUBLANES:
        b3 = b.reshape(c // (2 * n), 2 * n, b.shape[1])
        return jnp.broadcast_to(b3[:, n - 1:n, :], b3.shape).reshape(b.shape)
    b3 = b.reshape(c // SUBLANES, SUBLANES, b.shape[1])
    seg = lax.broadcasted_iota(jnp.int32, b3.shape, 1) // (2 * n)
    out = jnp.broadcast_to(b3[:, n - 1:n, :], b3.shape)
    for s in range(1, SUBLANES // (2 * n)):
        r = s * 2 * n + n - 1
        out = jnp.where(seg == s, jnp.broadcast_to(b3[:, r:r + 1, :], b3.shape), out)
    return out.reshape(b.shape)


def _hgrn_kernel(q_ref, f_ref, v_ref, g_ref, nw_ref, o_ref, state_ref):
    c = HG_CHUNK

    @pl.when(pl.program_id(1) == 0)
    def _reset():
        state_ref[...] = jnp.zeros_like(state_ref)

    half = c // 2
    row = lax.broadcasted_iota(jnp.int32, (c, c), 0)
    col = lax.broadcasted_iota(jnp.int32, (c, c), 1)
    tril = jnp.where(row >= col, 1.0, 0.0).astype(BF16)
    rh = lax.broadcasted_iota(jnp.int32, (half, half), 0)
    ch = lax.broadcasted_iota(jnp.int32, (half, half), 1)
    split = jnp.where(rh > ch, rh ^ ch, 0)
    levels = [1 << i for i in range(int(math.log2(half)))]

    def chunk_head(rows, hh):
        sl = slice(hh * HG_DK, (hh + 1) * HG_DK)
        f = f_ref[0, rows, sl]
        logf = jnp.log(f)
        kk = 1.0 - f
        q = q_ref[0, rows, sl]
        v = v_ref[0, rows, sl]
        vb = v.astype(BF16)
        qb = q.astype(BF16)
        kb = kk.astype(BF16)

        parts = _dot(tril, jnp.concatenate(_split3(logf * LOG2E), axis=1))
        b = parts[:, :HG_DK] + (parts[:, HG_DK:2 * HG_DK] + parts[:, 2 * HG_DK:])

        def level_operands(n):
            if n == 1:
                return qb * f.astype(BF16), kb
            e_l = jnp.exp2(-jnp.abs(b - _segment_ref(b, n))).astype(BF16)
            return qb * e_l, kb * e_l

        d_lo = jnp.zeros((half, half), F32)
        d_hi = jnp.zeros((half, half), F32)
        for n in levels:
            q_l, k_l = level_operands(n)
            d_lo = jnp.where(split >= n, _dot_nt(q_l[:half], k_l[:half]), d_lo)
            d_hi = jnp.where(split >= n, _dot_nt(q_l[half:], k_l[half:]), d_hi)
        q_l, k_l = level_operands(half)
        cross = _dot_nt(q_l[half:], k_l[:half])

        st = state_ref[hh]
        intra = jnp.concatenate(
            [_dot(d_lo.astype(BF16), vb[:half]),
             _dot(jnp.concatenate([cross, d_hi], axis=1).astype(BF16), vb)], axis=0)
        o = (intra + jnp.sum(q * kk, axis=-1, keepdims=True) * v
             + _dot_nt((q * jnp.exp2(b)).astype(BF16), st.astype(BF16)))
        b_last = b[c - 1:c, :]
        k_dec = (kk * jnp.exp2(b_last - b)).astype(BF16)
        state_ref[hh] = st * jnp.exp2(b_last) + lax.dot_general(
            vb, k_dec, (((0,), (0,)), ((), ())), preferred_element_type=F32)

        o_ref[0, rows, sl] = _rms(o, nw_ref[:, :]) * g_ref[0, rows, sl]

    for sub in range(HG_STEP_CHUNKS):
        for hh in range(HG_HEADS):
            chunk_head(slice(sub * c, (sub + 1) * c), hh)


def _hgrn(q, f, v, g, norm_w):
    bsz, seq, _ = q.shape
    step_rows = HG_STEP_CHUNKS * HG_CHUNK
    assert seq % step_rows == 0
    blk = pl.BlockSpec((1, step_rows, HG_WIDTH), lambda b, t: (b, t, 0))
    return pl.pallas_call(
        _hgrn_kernel,
        grid=(bsz, seq // step_rows),
        in_specs=[blk, blk, blk, blk, pl.BlockSpec((1, HG_DV), lambda b, t: (0, 0))],
        out_specs=blk,
        out_shape=jax.ShapeDtypeStruct((bsz, seq, HG_WIDTH), F32),
        scratch_shapes=[pltpu.VMEM((HG_HEADS, HG_DV, HG_DK), F32)],
        compiler_params=_params(("parallel", "arbitrary")),
        name="hgrn",
    )(q, f, v, g, norm_w)


def kernel(x, p, w_ffn1_gu, w_ffn1_down, w_in, w_branch_a, w_branch_b, w_out,
           w_ffn2_gu, w_ffn2_down, w_ple, w_ple_gate, norm_gains, hg_norm_w,
           lb_param, rel_table):
    bsz, seq, _ = x.shape
    t = bsz * seq
    assert t % TOKEN_TILE == 0
    h = x.reshape(t, D_MODEL)
    for i in range(p.shape[0]):
        g = norm_gains[i]
        h = _ffn(h, g, w_ffn1_gu[i].astype(BF16), w_ffn1_down[i].astype(BF16), 0, 1)
        mq, mk, mv, hq, hf, hi, hg, gates = _inproj(h, g, lb_param, w_in[i].astype(BF16), i)
        r3 = lambda a: a.reshape(bsz, seq, a.shape[-1])
        o_a = _moba(r3(mq), r3(mk), r3(mv), rel_table)
        o_b = _hgrn(r3(hq), r3(hf), r3(hi), r3(hg), hg_norm_w[i:i + 1])
        h = _post(h, o_a.reshape(t, MB_WIDTH), o_b.reshape(t, HG_WIDTH), gates,
                  p[i].reshape(t, PLE_DIM), g,
                  w_branch_a[i].astype(BF16), w_branch_b[i].astype(BF16), w_out[i].astype(BF16),
                  w_ffn2_gu[i].astype(BF16), w_ffn2_down[i].astype(BF16),
                  w_ple[i].astype(BF16), w_ple_gate[i].astype(BF16))
    return h.reshape(bsz, seq, D_MODEL)
```

```python
import functools
import math

import numpy as np
import jax
import jax.numpy as jnp
from jax import lax
from jax.experimental import pallas as pl
from jax.experimental.pallas import tpu as pltpu

F32 = jnp.float32
BF16 = jnp.bfloat16

D_MODEL = 1024
PLE_DIM = 256
D_FF = 2816
MB_HEADS = 8
MB_HEAD_DIM = 64
MB_WIDTH = MB_HEADS * MB_HEAD_DIM
MB_BLOCK = 256
MB_TOPK = 3
HG_HEADS = 4
HG_DK = 128
HG_DV = 128
HG_WIDTH = HG_HEADS * HG_DV
REL_BUCKETS = 32
REL_MAX_EXACT = REL_BUCKETS // 2
REL_MAX_DIST = 128
N_NORMS = 7
EPS = 1e-6

LANES = 128
SUBLANES = 8
VMEM_LIMIT = 56 * 1024 * 1024
TOKEN_TILE = 512
FFN_TILE = 1024
FF_CHUNK = 256
HG_CHUNK = 256
HG_STEP_CHUNKS = 4
IN_CHUNK = 512
MB_QTILE = 1024
FAR_GROUP = 2
PAD_BLOCKS = 3 * FAR_GROUP
PAD_END = PAD_BLOCKS * MB_BLOCK
PEELED = 3
ONES_ROWS = 16
NEG = -1e30
LOG2E = 1.4426950408889634


def _rms(x, g):
    return x * lax.rsqrt(jnp.mean(x * x, axis=-1, keepdims=True) + EPS) * g


def _sigmoid(x):
    return 1.0 / (1.0 + jnp.exp(-x))


def _dot(a, b):
    return jnp.dot(a, b, preferred_element_type=F32)


def _dot_nt(a, b):
    return lax.dot_general(a, b, (((1,), (1,)), ((), ())), preferred_element_type=F32)


def _split3(x):
    def top(y):
        bits = pltpu.bitcast(y, jnp.uint32) & jnp.uint32(0xFFFF0000)
        return pltpu.bitcast(bits, F32)
    hi = top(x)
    r1 = x - hi
    mid = top(r1)
    return hi.astype(BF16), mid.astype(BF16), (r1 - mid).astype(BF16)


def _const_spec(shape):
    nd = len(shape)
    return pl.BlockSpec(shape, lambda *_: (0,) * nd, pipeline_mode=pl.Buffered(1))


def _params(sem):
    return pltpu.CompilerParams(dimension_semantics=sem, vmem_limit_bytes=VMEM_LIMIT)


def _ffn_step(x, g_ref, pre, post, wgu_ref, wd_ref, act_ref):
    u = _rms(x, g_ref[pre:pre + 1, :]).astype(BF16)
    for c in range(D_FF // FF_CHUNK):
        lo = c * FF_CHUNK
        g = _dot(u, wgu_ref[:, lo:lo + FF_CHUNK])
        v = _dot(u, wgu_ref[:, D_FF + lo:D_FF + lo + FF_CHUNK])
        act_ref[:, lo:lo + FF_CHUNK] = (g * _sigmoid(g) * v).astype(BF16)
    y = _dot(act_ref[...], wd_ref[...])
    return x + 0.5 * _rms(y, g_ref[post:post + 1, :])


def _ffn_kernel(pre, post, x_ref, g_ref, wgu_ref, wd_ref, o_ref, act_ref):
    o_ref[...] = _ffn_step(x_ref[...], g_ref, pre, post, wgu_ref, wd_ref, act_ref)


def _ffn(x, gains, w_gu, w_down, pre, post):
    t = x.shape[0]
    return pl.pallas_call(
        functools.partial(_ffn_kernel, pre, post),
        grid=(t // FFN_TILE,),
        in_specs=[
            pl.BlockSpec((FFN_TILE, D_MODEL), lambda i: (i, 0)),
            _const_spec((N_NORMS, D_MODEL)),
            _const_spec((D_MODEL, 2 * D_FF)),
            _const_spec((D_FF, D_MODEL)),
        ],
        out_specs=pl.BlockSpec((FFN_TILE, D_MODEL), lambda i: (i, 0)),
        out_shape=jax.ShapeDtypeStruct((t, D_MODEL), F32),
        scratch_shapes=[pltpu.VMEM((FFN_TILE, D_FF), BF16)],
        compiler_params=_params(("parallel",)),
        name="ffn",
    )(x, gains, w_gu, w_down)


IN_SPLITS = (MB_WIDTH, MB_WIDTH, MB_WIDTH, HG_WIDTH, HG_WIDTH, HG_WIDTH, HG_WIDTH, 2 * D_MODEL)
IN_COLS = sum(IN_SPLITS)


HG_Q, HG_F, HG_G = 3, 4, 6


def _inproj_kernel(layer, h_ref, g_ref, lbp_ref, w_ref, *out_refs):
    u = _rms(h_ref[...], g_ref[2:3, :]).astype(BF16)
    lp = lbp_ref[...]
    e = jnp.exp(lp - jnp.max(lp, axis=0, keepdims=True))
    sm = e / jnp.sum(e, axis=0, keepdims=True)
    lb = sm[0:1, :]
    for r in range(1, layer + 1):
        lb = lb + sm[r:r + 1, :]
    off = 0
    for idx, ref in enumerate(out_refs):
        n = ref.shape[-1]
        for c in range(0, n, IN_CHUNK):
            z = _dot(u, w_ref[:, off + c:off + c + IN_CHUNK])
            if idx in (HG_Q, HG_G):
                z = z * _sigmoid(z)
            elif idx == HG_F:
                z = lb + (1.0 - lb) * _sigmoid(z)
            ref[:, c:c + IN_CHUNK] = z
        off += n


def _inproj(h, gains, lb_param, w_in, layer):
    t = h.shape[0]
    return pl.pallas_call(
        functools.partial(_inproj_kernel, layer),
        grid=(t // TOKEN_TILE,),
        in_specs=[
            pl.BlockSpec((TOKEN_TILE, D_MODEL), lambda i: (i, 0)),
            _const_spec((N_NORMS, D_MODEL)),
            _const_spec(lb_param.shape),
            _const_spec((D_MODEL, IN_COLS)),
        ],
        out_specs=[pl.BlockSpec((TOKEN_TILE, n), lambda i: (i, 0)) for n in IN_SPLITS],
        out_shape=[jax.ShapeDtypeStruct((t, n), F32) for n in IN_SPLITS],
        compiler_params=_params(("parallel",)),
        name="in_proj",
    )(h, gains, lb_param, w_in)


def _post_kernel(h_ref, oa_ref, ob_ref, gg_ref, p_ref, g_ref, wa_ref, wb_ref, wo_ref,
                 wgu_ref, wd_ref, wp_ref, wg_ref, o_ref, act_ref):
    a = _dot(oa_ref[...].astype(BF16), wa_ref[...])
    b = _dot(ob_ref[...].astype(BF16), wb_ref[...])
    merged = _sigmoid(gg_ref[:, :D_MODEL]) * a + _sigmoid(gg_ref[:, D_MODEL:]) * b
    h = h_ref[...] + _rms(_dot(merged.astype(BF16), wo_ref[...]), g_ref[3:4, :])
    h = _ffn_step(h, g_ref, 4, 5, wgu_ref, wd_ref, act_ref)
    e = _dot(p_ref[...].astype(BF16), wp_ref[...])
    t = _dot(h.astype(BF16), wg_ref[...])
    o_ref[...] = h + _rms(_sigmoid(t) * e, g_ref[6:7, :])


def _post(h, o_a, o_b, gates, p, gains, w_a, w_b, w_out, w_gu, w_down, w_ple, w_ple_gate):
    t = h.shape[0]
    row = lambda n: pl.BlockSpec((TOKEN_TILE, n), lambda i: (i, 0))
    return pl.pallas_call(
        _post_kernel,
        grid=(t // TOKEN_TILE,),
        in_specs=[
            row(D_MODEL), row(MB_WIDTH), row(HG_WIDTH), row(2 * D_MODEL), row(PLE_DIM),
            _const_spec((N_NORMS, D_MODEL)),
            _const_spec((MB_WIDTH, D_MODEL)),
            _const_spec((HG_WIDTH, D_MODEL)),
            _const_spec((D_MODEL, D_MODEL)),
            _const_spec((D_MODEL, 2 * D_FF)),
            _const_spec((D_FF, D_MODEL)),
            _const_spec((PLE_DIM, D_MODEL)),
            _const_spec((D_MODEL, D_MODEL)),
        ],
        out_specs=row(D_MODEL),
        out_shape=jax.ShapeDtypeStruct((t, D_MODEL), F32),
        scratch_shapes=[pltpu.VMEM((TOKEN_TILE, D_FF), BF16)],
        compiler_params=_params(("parallel",)),
        name="post",
    )(h, o_a, o_b, gates, p, gains, w_a, w_b, w_out, w_gu, w_down, w_ple, w_ple_gate)


def _bucket_starts():
    n = np.arange(0, 2 * MB_BLOCK, dtype=np.int32)
    nf = np.maximum(n, 1).astype(np.float32)
    large = REL_MAX_EXACT + (
        np.log(nf / np.float32(REL_MAX_EXACT)) / np.float32(math.log(REL_MAX_DIST / REL_MAX_EXACT))
        * np.float32(REL_BUCKETS - REL_MAX_EXACT)).astype(np.int32)
    large = np.minimum(large, REL_BUCKETS - 1)
    bucket = np.where(n < REL_MAX_EXACT, n, large)
    assert np.all(np.diff(bucket) >= 0) and bucket[-1] == REL_BUCKETS - 1
    starts = [int(np.argmax(bucket >= b)) for b in range(REL_BUCKETS)]
    assert starts[-1] <= MB_BLOCK + 1
    return starts


def _moba_kernel(starts, rel_ref, q_ref, k_ref, v_ref, o_ref,
                 kb_ref, vt_ref, kmean_ref, bown_ref, bprev_ref, sel_ref, qs_ref,
                 m_ref, acc_ref, sb0_ref, sb1_ref, mb0_ref, mb1_ref, sn_ref, mbn_ref):
    pair = pl.program_id(1)
    step = pl.program_id(2)
    seq = k_ref.shape[1]
    nb = seq // MB_BLOCK
    hd = MB_HEAD_DIM
    qscale = (hd ** -0.5) * LOG2E

    @pl.when(step == 0)
    def _prepare():
        kb_ref[0:MB_BLOCK, :] = jnp.zeros((MB_BLOCK, LANES), BF16)
        kb_ref[MB_BLOCK + seq:, :] = jnp.zeros((PAD_END, LANES), BF16)
        sel_ref[:, nb:, :] = jnp.full((2, PAD_BLOCKS, MB_QTILE), NEG, F32)
        for hh in range(2):
            vt_ref[hh, :, 0:MB_BLOCK] = jnp.zeros((hd + ONES_ROWS, MB_BLOCK), BF16)
            vt_ref[hh, :, MB_BLOCK + seq:] = jnp.zeros((hd + ONES_ROWS, PAD_END), BF16)
            vt_ref[hh, hd:, MB_BLOCK:MB_BLOCK + seq] = jnp.ones((ONES_ROWS, seq), BF16)

        def blk(j, carry):
            r0 = pl.multiple_of(j * MB_BLOCK, MB_BLOCK)
            r1 = pl.multiple_of(r0 + MB_BLOCK, MB_BLOCK)
            kblk = k_ref[0, pl.ds(r0, MB_BLOCK), :]
            kb_ref[pl.ds(r1, MB_BLOCK), :] = kblk.astype(BF16)
            kmean_ref[pl.ds(j, 1), :] = jnp.mean(kblk, axis=0, keepdims=True)
            vt = v_ref[0, pl.ds(r0, MB_BLOCK), :].T.astype(BF16)
            vt_ref[0, 0:hd, pl.ds(r1, MB_BLOCK)] = vt[0:hd]
            vt_ref[1, 0:hd, pl.ds(r1, MB_BLOCK)] = vt[hd:2 * hd]
            return carry

        lax.fori_loop(0, nb, blk, 0, unroll=4)

        kk = lax.broadcasted_iota(jnp.int32, (MB_BLOCK, MB_BLOCK), 0)
        qq = lax.broadcasted_iota(jnp.int32, (MB_BLOCK, MB_BLOCK), 1)
        d_own = qq - kk
        d_prev = d_own + MB_BLOCK
        for hh in range(2):
            head = pair * 2 + hh
            b_own = jnp.full((MB_BLOCK, MB_BLOCK), rel_ref[0, head], F32)
            b_prev = jnp.full((MB_BLOCK, MB_BLOCK), rel_ref[0, head], F32)
            for b in range(1, REL_BUCKETS):
                val = rel_ref[b, head]
                b_own = jnp.where(d_own >= starts[b], val, b_own)
                b_prev = jnp.where(d_prev >= starts[b], val, b_prev)
            bown_ref[hh] = jnp.where(d_own >= 0, b_own * LOG2E, NEG)
            bprev_ref[hh] = b_prev * LOG2E

    nq = MB_QTILE // MB_BLOCK
    qb0 = step * nq
    c_far = [rel_ref[REL_BUCKETS - 1, pair * 2 + hh] * LOG2E for hh in range(2)]

    qt = q_ref[0].T
    row_head = lax.broadcasted_iota(jnp.int32, (2 * hd, MB_QTILE), 0) // hd
    for hh in range(2):
        qs = (jnp.where(row_head == hh, qt, 0.0) * qscale).astype(BF16)
        for qi in range(nq):
            c0 = (2 * qi + hh) * MB_BLOCK
            qs_ref[:, c0:c0 + MB_BLOCK] = qs[:, qi * MB_BLOCK:(qi + 1) * MB_BLOCK]

    def select_blocks():
        km = kmean_ref[...]
        lane_head = lax.broadcasted_iota(jnp.int32, (nb, 2 * hd), 1) // hd
        km_hi, km_mid, km_lo = _split3(jnp.concatenate(
            [jnp.where(lane_head == hh, km, 0.0) for hh in range(2)], axis=0))
        q_hi, q_mid, q_lo = _split3(qt)
        gates = (_dot(km_hi, q_hi) + (_dot(km_hi, q_mid) + _dot(km_mid, q_hi))
                 + (_dot(km_hi, q_lo) + _dot(km_mid, q_mid) + _dot(km_lo, q_hi)))
        blk_row = lax.broadcasted_iota(jnp.int32, (nb, MB_QTILE), 0)
        own_blk = qb0 + lax.broadcasted_iota(jnp.int32, (nb, MB_QTILE), 1) // MB_BLOCK
        for hh in range(2):
            gate = gates[hh * nb:(hh + 1) * nb]
            gate = jnp.where(blk_row < own_blk, gate, -jnp.inf)
            sel = jnp.full((nb, MB_QTILE), NEG, F32)
            for _ in range(MB_TOPK):
                mx = jnp.max(gate, axis=0, keepdims=True)
                cand = jnp.where((gate == mx) & (mx > -jnp.inf), blk_row, nb)
                pick = blk_row == jnp.min(cand, axis=0, keepdims=True)
                sel = jnp.where(pick, 0.0, sel)
                gate = jnp.where(pick, -jnp.inf, gate)
            sel_ref[hh, 0:nb, :] = sel

    def key_slice(blocks):
        return pl.ds(pl.multiple_of(blocks[0] * MB_BLOCK, MB_BLOCK), blocks[1] * MB_BLOCK)

    def score(qi, blocks, mats, sb, mb):
        s_all = _dot(kb_ref[key_slice(blocks), :],
                     qs_ref[:, 2 * qi * MB_BLOCK:2 * (qi + 1) * MB_BLOCK])
        slots = mb.shape[0] // (2 * nq)
        for u in range(blocks[1]):
            for hh in range(2):
                c = 2 * qi + hh
                piece = s_all[u * MB_BLOCK:(u + 1) * MB_BLOCK, hh * MB_BLOCK:(hh + 1) * MB_BLOCK]
                if mats[hh][u] is not None:
                    piece = piece + mats[hh][u]
                mb[c * slots + u:c * slots + u + 1, :] = jnp.max(piece, axis=0, keepdims=True)
                sb[u * MB_BLOCK:(u + 1) * MB_BLOCK, c * MB_BLOCK:(c + 1) * MB_BLOCK] = piece

    def consume(qi, blocks, rows, sb, mb, first):
        kpos = key_slice(blocks)
        slots = mb.shape[0] // (2 * nq)
        for hh in range(2):
            c = 2 * qi + hh
            m_blk = None
            for u in range(blocks[1]):
                m_u = mb[c * slots + u:c * slots + u + 1, :]
                if rows[hh][u] is not None:
                    m_u = m_u + rows[hh][u]
                m_blk = m_u if m_blk is None else jnp.maximum(m_blk, m_u)
            m_old = None if first else m_ref[c:c + 1, :]
            m_new = m_blk if first else jnp.maximum(m_old, m_blk)
            p = jnp.concatenate(
                [jnp.exp2(sb[u * MB_BLOCK:(u + 1) * MB_BLOCK, c * MB_BLOCK:(c + 1) * MB_BLOCK]
                          + ((-m_new) if rows[hh][u] is None else (rows[hh][u] - m_new)))
                 for u in range(blocks[1])], axis=0)
            pv = _dot(vt_ref[hh, :, kpos], p.astype(BF16))
            acc_ref[c] = pv if first else acc_ref[c] * jnp.exp2(m_old - m_new) + pv
            m_ref[c:c + 1, :] = m_new

    def far_row(hh, qi, j, limit):
        row = sel_ref[hh, pl.ds(jnp.maximum(j, 0), 1), qi * MB_BLOCK:(qi + 1) * MB_BLOCK]
        return jnp.where((j >= 0) & (j < limit), row + c_far[hh], NEG)

    def near_blocks(qi):
        return qb0, qi + 2

    def score_near(qi):
        mats = [[None] * qi + [bprev_ref[hh], bown_ref[hh]] for hh in range(2)]
        score(qi, near_blocks(qi), mats, sn_ref, mbn_ref)

    def consume_near(qi):
        qb = qb0 + qi
        rows = []
        for hh in range(2):
            far = [far_row(hh, qi, qb0 - 1 + u, qb - 1) for u in range(qi)]
            prev = sel_ref[hh, pl.ds(jnp.maximum(qb - 1, 0), 1), qi * MB_BLOCK:(qi + 1) * MB_BLOCK]
            rows.append(far + [jnp.where(qb >= 1, prev, NEG), None])
        consume(qi, near_blocks(qi), rows, sn_ref, mbn_ref, True)

    nfar = qb0 - 1

    def far_blocks(g, n=FAR_GROUP):
        return g * FAR_GROUP + 1, n

    def score_far(g, qi, sb, mb, n=FAR_GROUP):
        score(qi, far_blocks(g, n), [[None] * n] * 2, sb, mb)

    def consume_far(g, qi, sb, mb, n=FAR_GROUP):
        rows = [[far_row(hh, qi, g * FAR_GROUP + u, nfar) for u in range(n)]
                for hh in range(2)]
        consume(qi, far_blocks(g, n), rows, sb, mb, False)

    for qi in range(nq):
        score_near(qi)
    select_blocks()
    for qi in range(nq):
        consume_near(qi)
        score_far(0, qi, sb0_ref, mb0_ref)

    assert nq % (2 * FAR_GROUP) == 0 and FAR_GROUP > 1

    def two_groups(t, last):
        n_last = FAR_GROUP - 1 if last else FAR_GROUP
        for qi in range(nq):
            score_far(2 * t + 1, qi, sb1_ref, mb1_ref, n_last)
            consume_far(2 * t, qi, sb0_ref, mb0_ref)
        for qi in range(nq):
            if not last:
                score_far(2 * t + 2, qi, sb0_ref, mb0_ref)
            consume_far(2 * t + 1, qi, sb1_ref, mb1_ref, n_last)

    trips = (jnp.maximum(nfar, 0) + 2 * FAR_GROUP - 1) // (2 * FAR_GROUP)
    lax.fori_loop(0, trips - PEELED, lambda t, carry: (two_groups(t, False), carry)[1], 0)
    for n in range(1, PEELED + 1):
        @pl.when((trips == n) if n < PEELED else (trips >= n))
        def _tail(n=n):
            for i in range(n, 0, -1):
                two_groups(trips - i, i == 1)

    for qi in range(nq):
        outs = [acc_ref[2 * qi + hh, 0:hd, :] / acc_ref[2 * qi + hh, hd:hd + 1, :]
                for hh in range(2)]
        o_ref[0, qi * MB_BLOCK:(qi + 1) * MB_BLOCK, :] = jnp.concatenate(outs, axis=0).T


def _moba(q, k, v, rel_table):
    bsz, seq, _ = q.shape
    assert seq % MB_QTILE == 0
    nb = seq // MB_BLOCK
    npair = MB_HEADS // 2
    padded = MB_BLOCK + seq + PAD_END
    chains = 2 * MB_QTILE // MB_BLOCK
    return pl.pallas_call(
        functools.partial(_moba_kernel, _bucket_starts()),
        grid=(bsz, npair, seq // MB_QTILE),
        in_specs=[
            pl.BlockSpec(memory_space=pltpu.SMEM),
            pl.BlockSpec((1, MB_QTILE, LANES), lambda b, p, t: (b, t, p)),
            pl.BlockSpec((1, seq, LANES), lambda b, p, t: (b, 0, p)),
            pl.BlockSpec((1, seq, LANES), lambda b, p, t: (b, 0, p)),
        ],
        out_specs=pl.BlockSpec((1, MB_QTILE, LANES), lambda b, p, t: (b, t, p)),
        out_shape=jax.ShapeDtypeStruct((bsz, seq, MB_WIDTH), F32),
        scratch_shapes=[
            pltpu.VMEM((padded, LANES), BF16),
            pltpu.VMEM((2, MB_HEAD_DIM + ONES_ROWS, padded), BF16),
            pltpu.VMEM((nb, LANES), F32),
            pltpu.VMEM((2, MB_BLOCK, MB_BLOCK), F32),
            pltpu.VMEM((2, MB_BLOCK, MB_BLOCK), F32),
            pltpu.VMEM((2, nb + PAD_BLOCKS, MB_QTILE), F32),
            pltpu.VMEM((LANES, 2 * MB_QTILE), BF16),
            pltpu.VMEM((2 * MB_QTILE // MB_BLOCK, MB_BLOCK), F32),
            pltpu.VMEM((chains, MB_HEAD_DIM + ONES_ROWS, MB_BLOCK), F32),
            pltpu.VMEM((FAR_GROUP * MB_BLOCK, 2 * MB_QTILE), F32),
            pltpu.VMEM((FAR_GROUP * MB_BLOCK, 2 * MB_QTILE), F32),
            pltpu.VMEM((chains * FAR_GROUP, MB_BLOCK), F32),
            pltpu.VMEM((chains * FAR_GROUP, MB_BLOCK), F32),
            pltpu.VMEM((MB_QTILE + MB_BLOCK, 2 * MB_QTILE), F32),
            pltpu.VMEM((chains * (chains // 2 + 1), MB_BLOCK), F32),
        ],
        compiler_params=_params(("parallel", "parallel", "arbitrary")),
        name="moba",
    )(rel_table, q, k, v)


def _segment_ref(b, n):
    c = b.shape[0]
    if 2 * n >= SUBLANES:
        b3 = b.reshape(c // (2 * n), 2 * n, b.shape[1])
        return jnp.broadcast_to(b3[:, n - 1:n, :], b3.shape).reshape(b.shape)
    b3 = b.reshape(c // SUBLANES, SUBLANES, b.shape[1])
    seg = lax.broadcasted_iota(jnp.int32, b3.shape, 1) // (2 * n)
    out = jnp.broadcast_to(b3[:, n - 1:n, :], b3.shape)
    for s in range(1, SUBLANES // (2 * n)):
        r = s * 2 * n + n - 1
        out = jnp.where(seg == s, jnp.broadcast_to(b3[:, r:r + 1, :], b3.shape), out)
    return out.reshape(b.shape)


def _hgrn_kernel(q_ref, f_ref, v_ref, g_ref, nw_ref, o_ref, state_ref):
    c = HG_CHUNK

    @pl.when(pl.program_id(1) == 0)
    def _reset():
        state_ref[...] = jnp.zeros_like(state_ref)

    half = c // 2
    row = lax.broadcasted_iota(jnp.int32, (c, c), 0)
    col = lax.broadcasted_iota(jnp.int32, (c, c), 1)
    tril = jnp.where(row >= col, 1.0, 0.0).astype(BF16)
    rh = lax.broadcasted_iota(jnp.int32, (half, half), 0)
    ch = lax.broadcasted_iota(jnp.int32, (half, half), 1)
    split = jnp.where(rh > ch, rh ^ ch, 0)
    levels = [1 << i for i in range(int(math.log2(half)))]

    def chunk_head(rows, hh):
        sl = slice(hh * HG_DK, (hh + 1) * HG_DK)
        f = f_ref[0, rows, sl]
        logf = jnp.log(f)
        kk = 1.0 - f
        q = q_ref[0, rows, sl]
        v = v_ref[0, rows, sl]
        vb = v.astype(BF16)
        qb = q.astype(BF16)
        kb = kk.astype(BF16)

        parts = _dot(tril, jnp.concatenate(_split3(logf * LOG2E), axis=1))
        b = parts[:, :HG_DK] + (parts[:, HG_DK:2 * HG_DK] + parts[:, 2 * HG_DK:])

        def level_operands(n):
            if n == 1:
                return qb * f.astype(BF16), kb
            e_l = jnp.exp2(-jnp.abs(b - _segment_ref(b, n))).astype(BF16)
            return qb * e_l, kb * e_l

        d_lo = jnp.zeros((half, half), F32)
        d_hi = jnp.zeros((half, half), F32)
        for n in levels:
            q_l, k_l = level_operands(n)
            d_lo = jnp.where(split >= n, _dot_nt(q_l[:half], k_l[:half]), d_lo)
            d_hi = jnp.where(split >= n, _dot_nt(q_l[half:], k_l[half:]), d_hi)
        q_l, k_l = level_operands(half)
        cross = _dot_nt(q_l[half:], k_l[:half])

        st = state_ref[hh]
        intra = jnp.concatenate(
            [_dot(d_lo.astype(BF16), vb[:half]),
             _dot(jnp.concatenate([cross, d_hi], axis=1).astype(BF16), vb)], axis=0)
        o = (intra + jnp.sum(q * kk, axis=-1, keepdims=True) * v
             + _dot_nt((q * jnp.exp2(b)).astype(BF16), st.astype(BF16)))
        b_last = b[c - 1:c, :]
        k_dec = (kk * jnp.exp2(b_last - b)).astype(BF16)
        state_ref[hh] = st * jnp.exp2(b_last) + lax.dot_general(
            vb, k_dec, (((0,), (0,)), ((), ())), preferred_element_type=F32)

        o_ref[0, rows, sl] = _rms(o, nw_ref[:, :]) * g_ref[0, rows, sl]

    for sub in range(HG_STEP_CHUNKS):
        for hh in range(HG_HEADS):
            chunk_head(slice(sub * c, (sub + 1) * c), hh)


def _hgrn(q, f, v, g, norm_w):
    bsz, seq, _ = q.shape
    step_rows = HG_STEP_CHUNKS * HG_CHUNK
    assert seq % step_rows == 0
    blk = pl.BlockSpec((1, step_rows, HG_WIDTH), lambda b, t: (b, t, 0))
    return pl.pallas_call(
        _hgrn_kernel,
        grid=(bsz, seq // step_rows),
        in_specs=[blk, blk, blk, blk, pl.BlockSpec((1, HG_DV), lambda b, t: (0, 0))],
        out_specs=blk,
        out_shape=jax.ShapeDtypeStruct((bsz, seq, HG_WIDTH), F32),
        scratch_shapes=[pltpu.VMEM((HG_HEADS, HG_DV, HG_DK), F32)],
        compiler_params=_params(("parallel", "arbitrary")),
        name="hgrn",
    )(q, f, v, g, norm_w)


def kernel(x, p, w_ffn1_gu, w_ffn1_down, w_in, w_branch_a, w_branch_b, w_out,
           w_ffn2_gu, w_ffn2_down, w_ple, w_ple_gate, norm_gains, hg_norm_w,
           lb_param, rel_table):
    bsz, seq, _ = x.shape
    t = bsz * seq
    assert t % TOKEN_TILE == 0
    h = x.reshape(t, D_MODEL)
    for i in range(p.shape[0]):
        g = norm_gains[i]
        h = _ffn(h, g, w_ffn1_gu[i].astype(BF16), w_ffn1_down[i].astype(BF16), 0, 1)
        mq, mk, mv, hq, hf, hi, hg, gates = _inproj(h, g, lb_param, w_in[i].astype(BF16), i)
        r3 = lambda a: a.reshape(bsz, seq, a.shape[-1])
        o_a = _moba(r3(mq), r3(mk), r3(mv), rel_table)
        o_b = _hgrn(r3(hq), r3(hf), r3(hi), r3(hg), hg_norm_w[i:i + 1])
        h = _post(h, o_a.reshape(t, MB_WIDTH), o_b.reshape(t, HG_WIDTH), gates,
                  p[i].reshape(t, PLE_DIM), g,
                  w_branch_a[i].astype(BF16), w_branch_b[i].astype(BF16), w_out[i].astype(BF16),
                  w_ffn2_gu[i].astype(BF16), w_ffn2_down[i].astype(BF16),
                  w_ple[i].astype(BF16), w_ple_gate[i].astype(BF16))
    return h.reshape(bsz, seq, D_MODEL)
```
